```python
import math
import jax, jax.numpy as jnp
from jax import lax
import numpy as np

D_MODEL = 1024
BATCH = 8
SEQ = 2048
DEPTH = 2

N_MEM = 256
HEAD_DIM = 64
MIX_WIDTH = 3 * D_MODEL // 4
MIX_HEADS = MIX_WIDTH // HEAD_DIM
MEM_WIDTH = D_MODEL - MIX_WIDTH
MEM_HEADS = 4
MEM_HEAD_DIM = MEM_WIDTH // MEM_HEADS
DECAY_LORA = 64
AAA_LORA = 64
GATE_LORA = 128
LORA_TOTAL = DECAY_LORA + AAA_LORA + GATE_LORA
RWKV_COLS = 3 * MIX_WIDTH + LORA_TOTAL
CHUNK = 128
GMLP_GROUPS = MIX_HEADS
D_FF = 11 * D_MODEL // 4
N_EXPERTS = 8
TOP_K = 2
N_A = (DEPTH + 1) // 2
N_B = DEPTH // 2
RMS_EPS = 1e-6
GN_EPS = 64e-5
LN_EPS = 1e-5

kernel_name = "hybrid_rwkv7_gmlp_memxattn_moe"


def rms_norm(x, g):
    xf = x.astype(jnp.float32)
    y = xf * lax.rsqrt(jnp.mean(xf * xf, axis=-1, keepdims=True) + RMS_EPS)
    return (y * g.astype(jnp.float32)).astype(x.dtype)


def token_shift_lerp(p, mu):
    prev = jnp.pad(p[:, :-1], ((0, 0), (1, 0), (0, 0)))
    return p + (prev - p) * mu


def rwkv7_time_mix(p, w0, w2, a0, a2, g2, k_k, k_a, r_k, lnx_w, lnx_b):
    p = p.astype(jnp.float32)
    b_, t_, _ = p.shape
    cuts = np.cumsum([MIX_WIDTH, MIX_WIDTH, MIX_WIDTH, DECAY_LORA, AAA_LORA]).tolist()
    r, k, v, wd, ad, gd = jnp.split(p, cuts, axis=-1)
    w = -jax.nn.softplus(-(w0 + jnp.tanh(wd) @ w2)) - 0.5
    decay = jnp.exp(-jnp.exp(w))
    a = jax.nn.sigmoid(a0 + ad @ a2)
    g = jax.nn.sigmoid(gd) @ g2

    def heads(t):
        return t.reshape(b_, t_, MIX_HEADS, HEAD_DIM)

    kk = heads(k * k_k)
    kk = kk / jnp.maximum(jnp.linalg.norm(kk, axis=-1, keepdims=True), 1e-12)
    k = k * (1.0 + (a - 1.0) * k_a)

    def seq_first(t):
        return jnp.moveaxis(t, 1, 0)

    def step(s, inp):
        r_t, w_t, k_t, v_t, kk_t, a_t = inp
        s_kk = jnp.einsum('bhvk,bhk->bhv', s, kk_t)
        s = (s * w_t[:, :, None, :]
             - s_kk[..., None] * (kk_t * a_t)[:, :, None, :]
             + v_t[..., None] * k_t[:, :, None, :])
        y_t = jnp.einsum('bhvk,bhk->bhv', s, r_t)
        return s, y_t

    s0 = jnp.zeros((b_, MIX_HEADS, HEAD_DIM, HEAD_DIM), jnp.float32)
    _, y = lax.scan(step, s0, (seq_first(heads(r)), seq_first(heads(decay)), seq_first(heads(k)),
                               seq_first(heads(v)), seq_first(kk), seq_first(heads(a))))
    y = jnp.moveaxis(y, 0, 1)
    mu = jnp.mean(y, axis=-1, keepdims=True)
    var = jnp.mean(jnp.square(y - mu), axis=-1, keepdims=True)
    y = ((y - mu) * lax.rsqrt(var + GN_EPS)).reshape(b_, t_, MIX_WIDTH) * lnx_w + lnx_b
    bonus = jnp.sum(heads(r) * heads(k) * r_k, axis=-1, keepdims=True) * heads(v)
    return (y + bonus.reshape(b_, t_, MIX_WIDTH)) * g


def chunked_gmlp(p, v_ln_g, v_ln_b, w_s, b_s):
    b_, t_, _ = p.shape
    h = jax.nn.gelu(p.astype(jnp.float32), approximate=False)
    u, v = jnp.split(h, 2, axis=-1)
    mu = jnp.mean(v, axis=-1, keepdims=True)
    var = jnp.mean(jnp.square(v - mu), axis=-1, keepdims=True)
    v = (v - mu) * lax.rsqrt(var + LN_EPS) * v_ln_g + v_ln_b
    v = v.reshape(b_, t_ // CHUNK, CHUNK, GMLP_GROUPS, MIX_WIDTH // GMLP_GROUPS)
    causal = jnp.tril(jnp.ones((CHUNK, CHUNK), jnp.float32))
    ws = w_s.astype(jnp.float32) * causal
    mixed = jnp.einsum('gts,bcsgd->bctgd', ws, v) + jnp.transpose(b_s)[None, None, :, :, None]
    return u * mixed.reshape(b_, t_, MIX_WIDTH)


def memory_attention(q, mem_n, w_kv):
    b_, t_, _ = q.shape
    k, v = jnp.split(mem_n @ w_kv, 2, axis=-1)
    q = q.reshape(b_, t_, MEM_HEADS, MEM_HEAD_DIM).astype(jnp.float32)
    k = k.reshape(b_, N_MEM, MEM_HEADS, MEM_HEAD_DIM).astype(jnp.float32)
    v = v.reshape(b_, N_MEM, MEM_HEADS, MEM_HEAD_DIM).astype(jnp.float32)
    s = jnp.einsum('bthd,bmhd->bhtm', q, k) * (1.0 / math.sqrt(MEM_HEAD_DIM))
    pr = jax.nn.softmax(s, axis=-1)
    o = jnp.einsum('bhtm,bmhd->bthd', pr, v)
    return o.reshape(b_, t_, MEM_WIDTH)


def swiglu(x, w_gu, w_down):
    gate, up = jnp.split(x @ w_gu, 2, axis=-1)
    return (jax.nn.silu(gate) * up) @ w_down


def moe_swiglu(x, w_router, w_gu, w_down):
    b_, t_, d_ = x.shape
    xt = x.reshape(b_ * t_, d_)
    logits = (xt @ w_router).astype(jnp.float32)
    top_v, top_i = lax.top_k(logits, TOP_K)
    gates = jax.nn.softmax(top_v, axis=-1)
    combine = jnp.sum(jax.nn.one_hot(top_i, N_EXPERTS, dtype=jnp.float32) * gates[..., None], axis=1)
    out = jnp.zeros((b_ * t_, d_), jnp.float32)
    for e in range(N_EXPERTS):
        out = out + combine[:, e:e + 1] * swiglu(xt, w_gu[e], w_down[e]).astype(jnp.float32)
    return out.reshape(b_, t_, d_).astype(x.dtype)


def setup_inputs(seed: int = 0) -> dict:
    key = jax.random.key(seed)
    ks = iter(jax.random.split(key, 40))
    nrm = lambda shape, scale: jax.random.normal(next(ks), shape, jnp.float32) * scale
    gain = lambda shape: 1.0 + nrm(shape, 0.02)
    d = D_MODEL
    return {
        "x": nrm((BATCH, SEQ, d), 1.0),
        "mem": nrm((BATCH, N_MEM, d), 1.0),
        "mem_norm_g": gain((d,)),
        "norm1_g": gain((DEPTH, d)),
        "w_kv_mem": nrm((DEPTH, d, 2 * MEM_WIDTH), d ** -0.5),
        "w_out": nrm((DEPTH, d, d), d ** -0.5),
        "norm2_g": gain((DEPTH, d)),
        "rwkv_w_in": nrm((N_A, d, RWKV_COLS + MEM_WIDTH), d ** -0.5),
        "rwkv_mu": jax.random.uniform(next(ks), (N_A, RWKV_COLS), jnp.float32, 0.0, 1.0),
        "rwkv_w0": jax.random.uniform(next(ks), (N_A, MIX_WIDTH), jnp.float32, -5.0, -0.5),
        "rwkv_w2": nrm((N_A, DECAY_LORA, MIX_WIDTH), 0.5 * DECAY_LORA ** -0.5),
        "rwkv_a0": nrm((N_A, MIX_WIDTH), 0.1),
        "rwkv_a2": nrm((N_A, AAA_LORA, MIX_WIDTH), AAA_LORA ** -0.5),
        "rwkv_g2": nrm((N_A, GATE_LORA, MIX_WIDTH), GATE_LORA ** -0.5),
        "rwkv_k_k": 0.85 + nrm((N_A, MIX_WIDTH), 0.05),
        "rwkv_k_a": 1.0 + nrm((N_A, MIX_WIDTH), 0.05),
        "rwkv_r_k": nrm((N_A, MIX_HEADS, HEAD_DIM), 0.1),
        "rwkv_lnx_w": gain((N_A, MIX_WIDTH)),
        "rwkv_lnx_b": nrm((N_A, MIX_WIDTH), 0.02),
        "ffn_w_gu": nrm((N_A, d, 2 * D_FF), d ** -0.5),
        "ffn_w_down": nrm((N_A, D_FF, d), D_FF ** -0.5),
        "gmlp_w_in": nrm((N_B, d, 2 * MIX_WIDTH + MEM_WIDTH), d ** -0.5),
        "gmlp_v_ln_g": gain((N_B, MIX_WIDTH)),
        "gmlp_v_ln_b": nrm((N_B, MIX_WIDTH), 0.02),
        "gmlp_w_s": nrm((N_B, GMLP_GROUPS, CHUNK, CHUNK), 0.5 * CHUNK ** -0.5),
        "gmlp_b_s": 1.0 + nrm((N_B, GMLP_GROUPS, CHUNK), 0.02),
        "moe_router": nrm((N_B, d, N_EXPERTS), d ** -0.5),
        "moe_w_gu": nrm((N_B, N_EXPERTS, d, 2 * D_FF), d ** -0.5),
        "moe_w_down": nrm((N_B, N_EXPERTS, D_FF, d), D_FF ** -0.5),
        "final_norm_g": gain((d,)),
    }


def reference(x, mem, mem_norm_g, norm1_g, w_kv_mem, w_out, norm2_g,
              rwkv_w_in, rwkv_mu, rwkv_w0, rwkv_w2, rwkv_a0, rwkv_a2, rwkv_g2,
              rwkv_k_k, rwkv_k_a, rwkv_r_k, rwkv_lnx_w, rwkv_lnx_b,
              ffn_w_gu, ffn_w_down,
              gmlp_w_in, gmlp_v_ln_g, gmlp_v_ln_b, gmlp_w_s, gmlp_b_s,
              moe_router, moe_w_gu, moe_w_down, final_norm_g):
    mem_n = rms_norm(mem, mem_norm_g)
    for i in range(DEPTH):
        j = i // 2
        n = rms_norm(x, norm1_g[i])
        if i % 2 == 0:
            p = n @ rwkv_w_in[j]
            p_mix, q = p[..., :RWKV_COLS], p[..., RWKV_COLS:]
            p_mix = token_shift_lerp(p_mix, rwkv_mu[j])
            y = rwkv7_time_mix(p_mix, rwkv_w0[j], rwkv_w2[j], rwkv_a0[j], rwkv_a2[j], rwkv_g2[j],
                               rwkv_k_k[j], rwkv_k_a[j], rwkv_r_k[j], rwkv_lnx_w[j], rwkv_lnx_b[j])
        else:
            p = n @ gmlp_w_in[j]
            p_mix, q = p[..., :2 * MIX_WIDTH], p[..., 2 * MIX_WIDTH:]
            y = chunked_gmlp(p_mix, gmlp_v_ln_g[j], gmlp_v_ln_b[j], gmlp_w_s[j], gmlp_b_s[j])
        o = memory_attention(q, mem_n, w_kv_mem[i])
        mixed = jnp.concatenate([y.astype(x.dtype), o.astype(x.dtype)], axis=-1)
        x = x + mixed @ w_out[i]
        n2 = rms_norm(x, norm2_g[i])
        if i % 2 == 0:
            x = x + swiglu(n2, ffn_w_gu[j], ffn_w_down[j])
        else:
            x = x + moe_swiglu(n2, moe_router[j], moe_w_gu[j], moe_w_down[j])
    return rms_norm(x, final_norm_g)
```

```python
import functools
import math

import jax
import jax.numpy as jnp
from jax import lax
from jax.experimental import pallas as pl
from jax.experimental.pallas import tpu as pltpu

F32 = jnp.float32
BF16 = jnp.bfloat16

D_MODEL = 1024
HEAD = 64
MIX = 768
MIX_HEADS = MIX // HEAD
MEMW = 256
MEM_HEADS = 4
N_MEM = 256
LORA_W = 256
RWKV_COLS = 3 * MIX + LORA_W
GCHUNK = 128
D_FF = 2816
N_EXPERTS = 8
RMS_EPS = 1e-6
GN_EPS = 64e-5
LN_EPS = 1e-5

LANES = 128
PAIRS = MIX // LANES
SCAN_CHUNK = 64
VMEM_LIMIT = 56 * 1024 * 1024


def _cparams(sem):
    return pltpu.CompilerParams(dimension_semantics=sem, vmem_limit_bytes=VMEM_LIMIT)


def _rms(x, g):
    ms = jnp.mean(x * x, axis=-1, keepdims=True)
    return x * lax.rsqrt(ms + RMS_EPS) * g


def _dot(a, b):
    return jnp.dot(a, b, preferred_element_type=F32)


def _dot_nt(a, b):
    return lax.dot_general(a, b, (((1,), (1,)), ((), ())), preferred_element_type=F32)


def _dot_tn(a, b):
    return lax.dot_general(a, b, (((0,), (0,)), ((), ())), preferred_element_type=F32)


def _dot_split(x, m_bf16):
    hi = x.astype(BF16)
    lo = (x - hi.astype(F32)).astype(BF16)
    return _dot(hi, m_bf16) + _dot(lo, m_bf16)


def _split_dot(m_bf16, x):
    hi = x.astype(BF16)
    lo = (x - hi.astype(F32)).astype(BF16)
    return _dot(m_bf16, hi) + _dot(m_bf16, lo)


def _norm_matmul_kernel(x_ref, g_ref, w_ref, o_ref):
    n = _rms(x_ref[...], g_ref[...]).astype(BF16)
    o_ref[...] = _dot(n, w_ref[...])


def _norm_matmul(x, g, w, tm):
    m, k = x.shape
    n = w.shape[1]
    return pl.pallas_call(
        _norm_matmul_kernel,
        grid=(m // tm,),
        in_specs=[
            pl.BlockSpec((tm, k), lambda i: (i, 0)),
            pl.BlockSpec((1, k), lambda i: (0, 0)),
            pl.BlockSpec((k, n), lambda i: (0, 0)),
        ],
        out_specs=pl.BlockSpec((tm, n), lambda i: (i, 0)),
        out_shape=jax.ShapeDtypeStruct((m, n), F32),
        compiler_params=_cparams(("parallel",)),
        name="norm_matmul",
    )(x, g.reshape(1, k), w)


def _mem_attn_kernel(q_ref, kv_ref, o_ref):
    q = q_ref[0]
    kv = kv_ref[0]
    k = kv[:, :MEMW].astype(BF16)
    v = kv[:, MEMW:].astype(BF16)
    lane = lax.broadcasted_iota(jnp.int32, (1, MEMW), 1)
    acc = jnp.zeros(q.shape, F32)
    for h in range(MEM_HEADS):
        m = (lane >= h * HEAD) & (lane < (h + 1) * HEAD)
        qh = jnp.where(m, q, 0.0).astype(BF16)
        s = _dot_nt(qh, k) * (1.0 / math.sqrt(HEAD))
        s = s - jnp.max(s, axis=-1, keepdims=True)
        p = jnp.exp(s)
        l = jnp.sum(p, axis=-1, keepdims=True)
        pv = _dot(p.astype(BF16), v)
        acc = acc + jnp.where(m, pv / l, 0.0)
    o_ref[0] = acc


def _mem_attn(p3, q_block, kv3, tt):
    b, t, _ = p3.shape
    return pl.pallas_call(
        _mem_attn_kernel,
        grid=(b, t // tt),
        in_specs=[
            pl.BlockSpec((1, tt, MEMW), lambda i, j: (i, j, q_block)),
            pl.BlockSpec((1, N_MEM, 2 * MEMW), lambda i, j: (i, 0, 0)),
        ],
        out_specs=pl.BlockSpec((1, tt, MEMW), lambda i, j: (i, j, 0)),
        out_shape=jax.ShapeDtypeStruct((b, t, MEMW), F32),
        compiler_params=_cparams(("parallel", "parallel")),
        name="mem_attn",
    )(p3, kv3)


def _mixer_out_kernel(x_ref, y_ref, o_ref, wy_ref, wo_ref, out_ref):
    out_ref[...] = (x_ref[...]
                    + _dot(y_ref[...].astype(BF16), wy_ref[...])
                    + _dot(o_ref[...].astype(BF16), wo_ref[...]))


def _mixer_out(x, y, o, wy, wo, tm):
    m, d = x.shape
    return pl.pallas_call(
        _mixer_out_kernel,
        grid=(m // tm,),
        in_specs=[
            pl.BlockSpec((tm, d), lambda i: (i, 0)),
            pl.BlockSpec((tm, MIX), lambda i: (i, 0)),
            pl.BlockSpec((tm, MEMW), lambda i: (i, 0)),
            pl.BlockSpec((MIX, d), lambda i: (0, 0)),
            pl.BlockSpec((MEMW, d), lambda i: (0, 0)),
        ],
        out_specs=pl.BlockSpec((tm, d), lambda i: (i, 0)),
        out_shape=jax.ShapeDtypeStruct((m, d), F32),
        compiler_params=_cparams(("parallel",)),
        name="mixer_out",
    )(x, y, o, wy, wo)


def _swiglu_kernel(*refs, n_inner, weighted):
    if weighted:
        x_ref, g_ref, c_ref, wg_ref, wu_ref, wd_ref, out_ref, n_scr, acc_scr = refs
        step = pl.program_id(1) * pl.num_programs(2) + pl.program_id(2)
    else:
        x_ref, g_ref, wg_ref, wu_ref, wd_ref, out_ref, n_scr, acc_scr = refs
        step = pl.program_id(1)

    @pl.when(step == 0)
    def _():
        n_scr[...] = _rms(x_ref[...], g_ref[...]).astype(BF16)
        acc_scr[...] = jnp.zeros_like(acc_scr)

    n = n_scr[...]
    gate = _dot(n, wg_ref[...])
    up = _dot(n, wu_ref[...])
    h = gate * jax.nn.sigmoid(gate) * up
    if weighted:
        lane = lax.broadcasted_iota(jnp.int32, c_ref.shape, 1)
        c = jnp.sum(jnp.where(lane == pl.program_id(1), c_ref[...], 0.0), axis=-1, keepdims=True)
        h = h * c
    acc_scr[...] += _dot(h.astype(BF16), wd_ref[...])

    @pl.when(step == n_inner - 1)
    def _():
        out_ref[...] = x_ref[...] + acc_scr[...]


def _ffn(x, g, w_gu, w_down, tm, tf):
    m, d = x.shape
    nf = D_FF // tf
    return pl.pallas_call(
        functools.partial(_swiglu_kernel, n_inner=nf, weighted=False),
        grid=(m // tm, nf),
        in_specs=[
            pl.BlockSpec((tm, d), lambda i, j: (i, 0)),
            pl.BlockSpec((1, d), lambda i, j: (0, 0)),
            pl.BlockSpec((d, tf), lambda i, j: (0, j)),
            pl.BlockSpec((d, tf), lambda i, j: (0, nf + j)),
            pl.BlockSpec((tf, d), lambda i, j: (j, 0)),
        ],
        out_specs=pl.BlockSpec((tm, d), lambda i, j: (i, 0)),
        out_shape=jax.ShapeDtypeStruct((m, d), F32),
        scratch_shapes=[pltpu.VMEM((tm, d), BF16), pltpu.VMEM((tm, d), F32)],
        compiler_params=_cparams(("parallel", "arbitrary")),
        name="ffn",
    )(x, g.reshape(1, d), w_gu, w_gu, w_down)


def _moe_dense(x, g, combine, w_gu, w_down, tm, tf):
    m, d = x.shape
    nf = D_FF // tf
    return pl.pallas_call(
        functools.partial(_swiglu_kernel, n_inner=N_EXPERTS * nf, weighted=True),
        grid=(m // tm, N_EXPERTS, nf),
        in_specs=[
            pl.BlockSpec((tm, d), lambda i, e, j: (i, 0)),
            pl.BlockSpec((1, d), lambda i, e, j: (0, 0)),
            pl.BlockSpec((tm, LANES), lambda i, e, j: (i, 0)),
            pl.BlockSpec((None, d, tf), lambda i, e, j: (e, 0, j)),
            pl.BlockSpec((None, d, tf), lambda i, e, j: (e, 0, nf + j)),
            pl.BlockSpec((None, tf, d), lambda i, e, j: (e, j, 0)),
        ],
        out_specs=pl.BlockSpec((tm, d), lambda i, e, j: (i, 0)),
        out_shape=jax.ShapeDtypeStruct((m, d), F32),
        scratch_shapes=[pltpu.VMEM((tm, d), BF16), pltpu.VMEM((tm, d), F32)],
        compiler_params=_cparams(("parallel", "arbitrary", "arbitrary")),
        name="moe_dense",
    )(x, g.reshape(1, d), combine, w_gu, w_gu, w_down)


def _router_kernel(x_ref, g_ref, wr_ref, c_ref):
    n = _rms(x_ref[...], g_ref[...])
    logits = jnp.dot(n, wr_ref[...], preferred_element_type=F32, precision=lax.Precision.HIGHEST)
    lane = lax.broadcasted_iota(jnp.int32, logits.shape, 1)
    neg = jnp.float32(-jnp.inf)
    lg = jnp.where(lane < N_EXPERTS, logits, neg)
    m1 = jnp.max(lg, axis=-1, keepdims=True)
    i1 = jnp.min(jnp.where(lg == m1, lane, LANES), axis=-1, keepdims=True)
    lg2 = jnp.where(lane == i1, neg, lg)
    m2 = jnp.max(lg2, axis=-1, keepdims=True)
    i2 = jnp.min(jnp.where(lg2 == m2, lane, LANES), axis=-1, keepdims=True)
    e2 = jnp.exp(m2 - m1)
    g1 = 1.0 / (1.0 + e2)
    g2 = e2 / (1.0 + e2)
    c_ref[...] = jnp.where(lane == i1, g1, 0.0) + jnp.where(lane == i2, g2, 0.0)


def _router(x, g, wr_pad, tm):
    m, d = x.shape
    return pl.pallas_call(
        _router_kernel,
        grid=(m // tm,),
        in_specs=[
            pl.BlockSpec((tm, d), lambda i: (i, 0)),
            pl.BlockSpec((1, d), lambda i: (0, 0)),
            pl.BlockSpec((d, LANES), lambda i: (0, 0)),
        ],
        out_specs=pl.BlockSpec((tm, LANES), lambda i: (i, 0)),
        out_shape=jax.ShapeDtypeStruct((m, LANES), F32),
        compiler_params=_cparams(("parallel",)),
        name="router",
    )(x, g.reshape(1, d), wr_pad)


def _final_norm_kernel(x_ref, g_ref, o_ref):
    o_ref[...] = _rms(x_ref[...], g_ref[...])


def _final_norm(x, g, tm):
    m, d = x.shape
    return pl.pallas_call(
        _final_norm_kernel,
        grid=(m // tm,),
        in_specs=[pl.BlockSpec((tm, d), lambda i: (i, 0)), pl.BlockSpec((1, d), lambda i: (0, 0))],
        out_specs=pl.BlockSpec((tm, d), lambda i: (i, 0)),
        out_shape=jax.ShapeDtypeStruct((m, d), F32),
        compiler_params=_cparams(("parallel",)),
        name="final_norm",
    )(x, g.reshape(1, d))


def _gelu(x):
    return 0.5 * x * (1.0 + lax.erf(x * (1.0 / math.sqrt(2.0))))


def _gmlp_kernel(u_ref, v_ref, lng_ref, lnb_ref, ws_ref, bias_ref, y_ref):
    tt = u_ref.shape[1]
    v = _gelu(v_ref[0])
    mu = jnp.mean(v, axis=-1, keepdims=True)
    d = v - mu
    var = jnp.mean(d * d, axis=-1, keepdims=True)
    vn = d * lax.rsqrt(var + LN_EPS) * lng_ref[...] + lnb_ref[...]
    row = lax.broadcasted_iota(jnp.int32, (GCHUNK, GCHUNK), 0)
    col = lax.broadcasted_iota(jnp.int32, (GCHUNK, GCHUNK), 1)
    causal = row >= col
    first_head = lax.broadcasted_iota(jnp.int32, (1, LANES), 1) < HEAD
    ws = [jnp.where(causal, ws_ref[gi], 0.0).astype(BF16) for gi in range(MIX_HEADS)]
    for c in range(tt // GCHUNK):
        rows = slice(c * GCHUNK, (c + 1) * GCHUNK)
        outs = []
        for j in range(PAIRS):
            vp = vn[rows, j * LANES:(j + 1) * LANES].astype(BF16)
            outs.append(jnp.where(first_head, _dot(ws[2 * j], vp), _dot(ws[2 * j + 1], vp)))
        mixed = jnp.concatenate(outs, axis=1) + bias_ref[...]
        y_ref[0, rows, :] = _gelu(u_ref[0, rows, :]) * mixed


def _gmlp(p3, ln_g, ln_b, w_s, bias, tt):
    b, t, _ = p3.shape
    return pl.pallas_call(
        _gmlp_kernel,
        grid=(b, t // tt),
        in_specs=[
            pl.BlockSpec((1, tt, MIX), lambda i, j: (i, j, 0)),
            pl.BlockSpec((1, tt, MIX), lambda i, j: (i, j, 1)),
            pl.BlockSpec((1, MIX), lambda i, j: (0, 0)),
            pl.BlockSpec((1, MIX), lambda i, j: (0, 0)),
            pl.BlockSpec((MIX_HEADS, GCHUNK, GCHUNK), lambda i, j: (0, 0, 0)),
            pl.BlockSpec((GCHUNK, MIX), lambda i, j: (0, 0)),
        ],
        out_specs=pl.BlockSpec((1, tt, MIX), lambda i, j: (i, j, 0)),
        out_shape=jax.ShapeDtypeStruct((b, t, MIX), F32),
        compiler_params=_cparams(("parallel", "parallel")),
        name="gmlp",
    )(p3, p3, ln_g.reshape(1, MIX), ln_b.reshape(1, MIX), w_s, bias)


def _token_shift(cur, prev_scr, mu):
    rows = cur.shape[0]
    row = lax.broadcasted_iota(jnp.int32, cur.shape, 0)
    prev = jnp.where(row == 0, prev_scr[...], pltpu.roll(cur, 1, axis=0))
    prev_scr[...] = cur[rows - 1:rows, :]
    return cur + (prev - cur) * mu


def _rwkv_kernel(r_ref, k_ref, v_ref, l_ref,
                 mur_ref, muk_ref, muv_ref, mul_ref, wcat_ref,
                 w0_ref, a0_ref, kk_ref, ka_ref, rk_ref, lnw_ref, lnb_ref,
                 jmat_ref, tril_ref,
                 y_ref,
                 pr_scr, pk_scr, pv_scr, pl_scr, state_scr,
                 r_scr, k_scr, v_scr, kk_scr, b_scr, ld_scr, g_scr, y_scr):
    tt = r_ref.shape[1]
    L = SCAN_CHUNK

    @pl.when(pl.program_id(1) == 0)
    def _():
        pr_scr[...] = jnp.zeros_like(pr_scr)
        pk_scr[...] = jnp.zeros_like(pk_scr)
        pv_scr[...] = jnp.zeros_like(pv_scr)
        pl_scr[...] = jnp.zeros_like(pl_scr)
        state_scr[...] = jnp.zeros_like(state_scr)

    r = _token_shift(r_ref[0], pr_scr, mur_ref[...])
    k = _token_shift(k_ref[0], pk_scr, muk_ref[...])
    v = _token_shift(v_ref[0], pv_scr, muv_ref[...])
    lo = _token_shift(l_ref[0], pl_scr, mul_ref[...])
    llane = lax.broadcasted_iota(jnp.int32, lo.shape, 1)
    z = jnp.where(llane < 64, jnp.tanh(lo), jnp.where(llane < 128, lo, jax.nn.sigmoid(lo)))
    proj = _dot(z.astype(BF16), wcat_ref[...])
    w = -jax.nn.softplus(-(w0_ref[...] + proj[:, :MIX])) - 0.5
    a = jax.nn.sigmoid(a0_ref[...] + proj[:, MIX:2 * MIX])
    jmat = jmat_ref[...]
    kk = k * kk_ref[...]
    kk_sq = kk * kk
    ss = jnp.concatenate(
        [_dot_split(kk_sq[:, j * LANES:(j + 1) * LANES], jmat) for j in range(PAIRS)], axis=1)
    kk = kk / jnp.maximum(jnp.sqrt(ss), 1e-12)
    k = k * (1.0 + (a - 1.0) * ka_ref[...])
    r_scr[...] = r
    k_scr[...] = k
    v_scr[...] = v
    kk_scr[...] = kk
    b_scr[...] = kk * a
    ld_scr[...] = -jnp.exp(w)
    g_scr[...] = proj[:, 2 * MIX:]

    lane = lax.broadcasted_iota(jnp.int32, (1, LANES), 1)
    h0 = lane < HEAD
    row = lax.broadcasted_iota(jnp.int32, (L, LANES), 0)
    col = lax.broadcasted_iota(jnp.int32, (L, LANES), 1) % HEAD
    strict = row > col
    incl = row >= col
    srow = lax.broadcasted_iota(jnp.int32, (LANES, LANES), 0) < HEAD
    scol = lax.broadcasted_iota(jnp.int32, (LANES, LANES), 1) < HEAD
    same_head = srow == scol
    tril = tril_ref[...]

    def cat(x, y):
        return jnp.concatenate([x, y], axis=0)

    def chunk_body(c, carry):
        rows = pl.ds(pl.multiple_of(c * L, L), L)
        for j in range(PAIRS):
            cols = slice(j * LANES, (j + 1) * LANES)
            r_c = r_scr[rows, cols]
            k_c = k_scr[rows, cols]
            v_c = v_scr[rows, cols]
            kk_c = kk_scr[rows, cols]
            b_c = b_scr[rows, cols]
            ld_c = ld_scr[rows, cols]
            cum = _split_dot(tril, ld_c)
            cum_last = cum[L - 1:L, :]
            p_inv = jnp.exp(-cum)
            at = -(kk_c * jnp.exp(cum - ld_c))
            bt = b_c * p_inv
            kt = k_c * p_inv
            rt = r_c * jnp.exp(cum)
            to_end = jnp.exp(cum_last - cum)
            zero = jnp.zeros_like(at)
            lhs0 = cat(jnp.where(h0, at, zero), jnp.where(h0, rt, zero)).astype(BF16)
            lhs1 = cat(jnp.where(h0, zero, at), jnp.where(h0, zero, rt)).astype(BF16)
            g_0 = _dot_nt(lhs0, cat(bt, kt).astype(BF16))
            g_1 = _dot_nt(lhs1, cat(kt, bt).astype(BF16))
            top0 = jnp.where(strict, g_0[:L], 0.0)
            top1 = jnp.where(strict, g_1[:L], 0.0)
            bot0 = jnp.where(incl, g_0[L:], 0.0)
            bot1 = jnp.where(incl, g_1[L:], 0.0)
            s_j = state_scr[j]
            ars = _dot_nt(cat(at, rt).astype(BF16), s_j.astype(BF16))
            vv = cat(v_c, v_c).astype(BF16)
            akv0 = _dot(jnp.where(h0, 0.0, top0).astype(BF16), vv)
            akv1 = _dot(jnp.where(h0, top1, 0.0).astype(BF16), vv)
            rhs = ars[:L] + jnp.where(h0, akv0, akv1)
            x = cat(jnp.where(h0, rhs, 0.0), jnp.where(h0, 0.0, rhs))
            apow = cat(jnp.where(h0, top0, 0.0), jnp.where(h0, 0.0, top1))
            n_steps = 6
            for i in range(n_steps):
                ab = apow.astype(BF16)
                x = x + _dot(ab, x.astype(BF16))
                if i + 1 < n_steps:
                    apow = _dot(ab, ab)
            u = x[:L] + x[L:]
            uv = cat(u, v_c)
            y = ars[L:] + jnp.where(h0,
                                    _dot(bot0.astype(BF16), uv.astype(BF16)),
                                    _dot(bot1.astype(BF16), cat(v_c, u).astype(BF16)))
            y_scr[rows, cols] = y
            upd = _dot_tn(uv.astype(BF16), cat(b_c * to_end, k_c * to_end).astype(BF16))
            state_scr[j] = s_j * jnp.exp(cum_last) + jnp.where(same_head, upd, 0.0)
        return carry

    lax.fori_loop(0, tt // L, chunk_body, 0)

    outs = []
    for j in range(PAIRS):
        cols = slice(j * LANES, (j + 1) * LANES)
        y = y_scr[:, cols]
        mean = _dot_split(y, jmat) * (1.0 / HEAD)
        d = y - mean
        var = _dot_split(d * d, jmat) * (1.0 / HEAD)
        yn = d * lax.rsqrt(var + GN_EPS) * lnw_ref[:, cols] + lnb_ref[:, cols]
        rk = _dot_split(r_scr[:, cols] * k_scr[:, cols] * rk_ref[:, cols], jmat)
        outs.append((yn + rk * v_scr[:, cols]) * g_scr[:, cols])
    y_ref[0] = jnp.concatenate(outs, axis=1)


def _rwkv(p3, mu, wcat, w0, a0, k_k, k_a, r_k, lnx_w, lnx_b, tt):
    b, t, _ = p3.shape
    row = lambda x: x.reshape(1, -1)
    head_of = jnp.arange(LANES) // HEAD
    jmat = (head_of[:, None] == head_of[None, :]).astype(BF16)
    idx = jnp.arange(SCAN_CHUNK)
    tril = (idx[:, None] >= idx[None, :]).astype(BF16)
    const = lambda shape: pl.BlockSpec(shape, lambda i, j: (0,) * len(shape))
    tile = lambda width: pltpu.VMEM((tt, width), F32)
    return pl.pallas_call(
        _rwkv_kernel,
        grid=(b, t // tt),
        in_specs=[
            pl.BlockSpec((1, tt, MIX), lambda i, j: (i, j, 0)),
            pl.BlockSpec((1, tt, MIX), lambda i, j: (i, j, 1)),
            pl.BlockSpec((1, tt, MIX), lambda i, j: (i, j, 2)),
            pl.BlockSpec((1, tt, LORA_W), lambda i, j: (i, j, 3 * MIX // LORA_W)),
            const((1, MIX)), const((1, MIX)), const((1, MIX)), const((1, LORA_W)),
            const((LORA_W, 3 * MIX)),
            const((1, MIX)), const((1, MIX)), const((1, MIX)), const((1, MIX)),
            const((1, MIX)), const((1, MIX)), const((1, MIX)),
            const((LANES, LANES)), const((SCAN_CHUNK, SCAN_CHUNK)),
        ],
        out_specs=pl.BlockSpec((1, tt, MIX), lambda i, j: (i, j, 0)),
        out_shape=jax.ShapeDtypeStruct((b, t, MIX), F32),
        scratch_shapes=[
            pltpu.VMEM((1, MIX), F32), pltpu.VMEM((1, MIX), F32), pltpu.VMEM((1, MIX), F32),
            pltpu.VMEM((1, LORA_W), F32),
            pltpu.VMEM((PAIRS, LANES, LANES), F32),
            tile(MIX), tile(MIX), tile(MIX), tile(MIX), tile(MIX), tile(MIX), tile(MIX), tile(MIX),
        ],
        compiler_params=_cparams(("parallel", "arbitrary")),
        name="rwkv7",
    )(p3, p3, p3, p3,
      row(mu[:MIX]), row(mu[MIX:2 * MIX]), row(mu[2 * MIX:3 * MIX]), row(mu[3 * MIX:]),
      wcat, row(w0), row(a0), row(k_k), row(k_a), row(r_k), row(lnx_w), row(lnx_b),
      jmat, tril)


def kernel(x, mem, mem_norm_g, norm1_g, w_kv_mem, w_out, norm2_g, rwkv_w_in, rwkv_mu, rwkv_w0, rwkv_w2, rwkv_a0, rwkv_a2, rwkv_g2, rwkv_k_k, rwkv_k_a, rwkv_r_k, rwkv_lnx_w, rwkv_lnx_b, ffn_w_gu, ffn_w_down, gmlp_w_in, gmlp_v_ln_g, gmlp_v_ln_b, gmlp_w_s, gmlp_b_s, moe_router, moe_w_gu, moe_w_down, final_norm_g):
    b, t, d = x.shape
    n_tok = b * t
    depth = norm1_g.shape[0]
    xs = x.reshape(n_tok, d)
    mem2 = mem.reshape(b * N_MEM, d)
    for i in range(depth):
        j = i // 2
        kv = _norm_matmul(mem2, mem_norm_g, w_kv_mem[i].astype(BF16), 512).reshape(b, N_MEM, 2 * MEMW)
        if i % 2 == 0:
            p = _norm_matmul(xs, norm1_g[i], rwkv_w_in[j].astype(BF16), 512)
            p3 = p.reshape(b, t, RWKV_COLS + MEMW)
            wcat = jnp.zeros((LORA_W, 3 * MIX), F32)
            wcat = wcat.at[0:64, 0:MIX].set(rwkv_w2[j])
            wcat = wcat.at[64:128, MIX:2 * MIX].set(rwkv_a2[j])
            wcat = wcat.at[128:256, 2 * MIX:].set(rwkv_g2[j])
            y = _rwkv(p3, rwkv_mu[j], wcat.astype(BF16), rwkv_w0[j], rwkv_a0[j], rwkv_k_k[j],
                      rwkv_k_a[j], rwkv_r_k[j].reshape(MIX), rwkv_lnx_w[j], rwkv_lnx_b[j], 256)
            q_block = RWKV_COLS // MEMW
        else:
            p = _norm_matmul(xs, norm1_g[i], gmlp_w_in[j].astype(BF16), 512)
            p3 = p.reshape(b, t, 2 * MIX + MEMW)
            bias = jnp.repeat(jnp.transpose(gmlp_b_s[j]), HEAD, axis=1)
            y = _gmlp(p3, gmlp_v_ln_g[j], gmlp_v_ln_b[j], gmlp_w_s[j], bias, 256)
            q_block = 2 * MIX // MEMW
        o = _mem_attn(p3, q_block, kv, 512)
        wo = w_out[i].astype(BF16)
        xs = _mixer_out(xs, y.reshape(n_tok, MIX), o.reshape(n_tok, MEMW), wo[:MIX], wo[MIX:], 512)
        if i % 2 == 0:
            xs = _ffn(xs, norm2_g[i], ffn_w_gu[j].astype(BF16), ffn_w_down[j].astype(BF16), 512, 1408)
        else:
            wr_pad = jnp.zeros((d, LANES), F32).at[:, :N_EXPERTS].set(moe_router[j])
            combine = _router(xs, norm2_g[i], wr_pad, 512)
            xs = _moe_dense(xs, norm2_g[i], combine, moe_w_gu[j].astype(BF16),
                            moe_w_down[j].astype(BF16), 512, 1408)
    return _final_norm(xs, final_norm_g, 512).reshape(b, t, d)
```

```python
import functools
import math

import jax
import jax.numpy as jnp
from jax import lax
from jax.experimental import pallas as pl
from jax.experimental.pallas import tpu as pltpu

F32 = jnp.float32
BF16 = jnp.bfloat16

D_MODEL = 1024
HEAD = 64
MIX = 768
MIX_HEADS = MIX // HEAD
MEMW = 256
MEM_HEADS = 4
N_MEM = 256
LORA_W = 256
RWKV_COLS = 3 * MIX + LORA_W
GCHUNK = 128
D_FF = 2816
N_EXPERTS = 8
RMS_EPS = 1e-6
GN_EPS = 64e-5
LN_EPS = 1e-5

LANES = 128
PAIRS = MIX // LANES
SCAN_CHUNK = 64
VMEM_LIMIT = 56 * 1024 * 1024


def _cparams(sem):
    return pltpu.CompilerParams(dimension_semantics=sem, vmem_limit_bytes=VMEM_LIMIT)


def _rms(x, g):
    ms = jnp.mean(x * x, axis=-1, keepdims=True)
    return x * lax.rsqrt(ms + RMS_EPS) * g


def _dot(a, b):
    return jnp.dot(a, b, preferred_element_type=F32)


def _dot_nt(a, b):
    return lax.dot_general(a, b, (((1,), (1,)), ((), ())), preferred_element_type=F32)


def _dot_tn(a, b):
    return lax.dot_general(a, b, (((0,), (0,)), ((), ())), preferred_element_type=F32)


def _dot_split(x, m_bf16):
    hi = x.astype(BF16)
    lo = (x - hi.astype(F32)).astype(BF16)
    return _dot(hi, m_bf16) + _dot(lo, m_bf16)


def _split_dot(m_bf16, x):
    hi = x.astype(BF16)
    lo = (x - hi.astype(F32)).astype(BF16)
    return _dot(m_bf16, hi) + _dot(m_bf16, lo)


def _norm_matmul_kernel(x_ref, g_ref, w_ref, o_ref):
    n = _rms(x_ref[...], g_ref[...]).astype(BF16)
    o_ref[...] = _dot(n, w_ref[...])


def _norm_matmul(x, g, w, tm):
    m, k = x.shape
    n = w.shape[1]
    return pl.pallas_call(
        _norm_matmul_kernel,
        grid=(m // tm,),
        in_specs=[
            pl.BlockSpec((tm, k), lambda i: (i, 0)),
            pl.BlockSpec((1, k), lambda i: (0, 0)),
            pl.BlockSpec((k, n), lambda i: (0, 0)),
        ],
        out_specs=pl.BlockSpec((tm, n), lambda i: (i, 0)),
        out_shape=jax.ShapeDtypeStruct((m, n), F32),
        compiler_params=_cparams(("parallel",)),
        name="norm_matmul",
    )(x, g.reshape(1, k), w)


def _mem_attn_kernel(q_ref, kv_ref, o_ref):
    q = q_ref[0]
    kv = kv_ref[0]
    k = kv[:, :MEMW].astype(BF16)
    v = kv[:, MEMW:].astype(BF16)
    lane = lax.broadcasted_iota(jnp.int32, (1, MEMW), 1)
    acc = jnp.zeros(q.shape, F32)
    for h in range(MEM_HEADS):
        m = (lane >= h * HEAD) & (lane < (h + 1) * HEAD)
        qh = jnp.where(m, q, 0.0).astype(BF16)
        s = _dot_nt(qh, k) * (1.0 / math.sqrt(HEAD))
        s = s - jnp.max(s, axis=-1, keepdims=True)
        p = jnp.exp(s)
        l = jnp.sum(p, axis=-1, keepdims=True)
        pv = _dot(p.astype(BF16), v)
        acc = acc + jnp.where(m, pv / l, 0.0)
    o_ref[0] = acc


def _mem_attn(p3, q_block, kv3, tt):
    b, t, _ = p3.shape
    return pl.pallas_call(
        _mem_attn_kernel,
        grid=(b, t // tt),
        in_specs=[
            pl.BlockSpec((1, tt, MEMW), lambda i, j: (i, j, q_block)),
            pl.BlockSpec((1, N_MEM, 2 * MEMW), lambda i, j: (i, 0, 0)),
        ],
        out_specs=pl.BlockSpec((1, tt, MEMW), lambda i, j: (i, j, 0)),
        out_shape=jax.ShapeDtypeStruct((b, t, MEMW), F32),
        compiler_params=_cparams(("parallel", "parallel")),
        name="mem_attn",
    )(p3, kv3)


def _mixer_out_kernel(x_ref, y_ref, o_ref, wy_ref, wo_ref, out_ref):
    out_ref[...] = (x_ref[...]
                    + _dot(y_ref[...].astype(BF16), wy_ref[...])
                    + _dot(o_ref[...].astype(BF16), wo_ref[...]))


def _mixer_out(x, y, o, wy, wo, tm):
    m, d = x.shape
    return pl.pallas_call(
        _mixer_out_kernel,
        grid=(m // tm,),
        in_specs=[
            pl.BlockSpec((tm, d), lambda i: (i, 0)),
            pl.BlockSpec((tm, MIX), lambda i: (i, 0)),
            pl.BlockSpec((tm, MEMW), lambda i: (i, 0)),
            pl.BlockSpec((MIX, d), lambda i: (0, 0)),
            pl.BlockSpec((MEMW, d), lambda i: (0, 0)),
        ],
        out_specs=pl.BlockSpec((tm, d), lambda i: (i, 0)),
        out_shape=jax.ShapeDtypeStruct((m, d), F32),
        compiler_params=_cparams(("parallel",)),
        name="mixer_out",
    )(x, y, o, wy, wo)


def _swiglu_kernel(*refs, n_inner, weighted):
    if weighted:
        x_ref, g_ref, c_ref, wg_ref, wu_ref, wd_ref, out_ref, n_scr, acc_scr = refs
        step = pl.program_id(1) * pl.num_programs(2) + pl.program_id(2)
    else:
        x_ref, g_ref, wg_ref, wu_ref, wd_ref, out_ref, n_scr, acc_scr = refs
        step = pl.program_id(1)

    @pl.when(step == 0)
    def _():
        n_scr[...] = _rms(x_ref[...], g_ref[...]).astype(BF16)
        acc_scr[...] = jnp.zeros_like(acc_scr)

    n = n_scr[...]
    gate = _dot(n, wg_ref[...])
    up = _dot(n, wu_ref[...])
    h = gate * jax.nn.sigmoid(gate) * up
    if weighted:
        lane = lax.broadcasted_iota(jnp.int32, c_ref.shape, 1)
        c = jnp.sum(jnp.where(lane == pl.program_id(1), c_ref[...], 0.0), axis=-1, keepdims=True)
        h = h * c
    acc_scr[...] += _dot(h.astype(BF16), wd_ref[...])

    @pl.when(step == n_inner - 1)
    def _():
        out_ref[...] = x_ref[...] + acc_scr[...]


def _ffn(x, g, w_gu, w_down, tm, tf):
    m, d = x.shape
    nf = D_FF // tf
    return pl.pallas_call(
        functools.partial(_swiglu_kernel, n_inner=nf, weighted=False),
        grid=(m // tm, nf),
        in_specs=[
            pl.BlockSpec((tm, d), lambda i, j: (i, 0)),
            pl.BlockSpec((1, d), lambda i, j: (0, 0)),
            pl.BlockSpec((d, tf), lambda i, j: (0, j)),
            pl.BlockSpec((d, tf), lambda i, j: (0, nf + j)),
            pl.BlockSpec((tf, d), lambda i, j: (j, 0)),
        ],
        out_specs=pl.BlockSpec((tm, d), lambda i, j: (i, 0)),
        out_shape=jax.ShapeDtypeStruct((m, d), F32),
        scratch_shapes=[pltpu.VMEM((tm, d), BF16), pltpu.VMEM((tm, d), F32)],
        compiler_params=_cparams(("parallel", "arbitrary")),
        name="ffn",
    )(x, g.reshape(1, d), w_gu, w_gu, w_down)


def _moe_dense(x, g, combine, w_gu, w_down, tm, tf):
    m, d = x.shape
    nf = D_FF // tf
    return pl.pallas_call(
        functools.partial(_swiglu_kernel, n_inner=N_EXPERTS * nf, weighted=True),
        grid=(m // tm, N_EXPERTS, nf),
        in_specs=[
            pl.BlockSpec((tm, d), lambda i, e, j: (i, 0)),
            pl.BlockSpec((1, d), lambda i, e, j: (0, 0)),
            pl.BlockSpec((tm, LANES), lambda i, e, j: (i, 0)),
            pl.BlockSpec((None, d, tf), lambda i, e, j: (e, 0, j)),
            pl.BlockSpec((None, d, tf), lambda i, e, j: (e, 0, nf + j)),
            pl.BlockSpec((None, tf, d), lambda i, e, j: (e, j, 0)),
        ],
        out_specs=pl.BlockSpec((tm, d), lambda i, e, j: (i, 0)),
        out_shape=jax.ShapeDtypeStruct((m, d), F32),
        scratch_shapes=[pltpu.VMEM((tm, d), BF16), pltpu.VMEM((tm, d), F32)],
        compiler_params=_cparams(("parallel", "arbitrary", "arbitrary")),
        name="moe_dense",
    )(x, g.reshape(1, d), combine, w_gu, w_gu, w_down)


def _router_kernel(x_ref, g_ref, wr_ref, c_ref):
    n = _rms(x_ref[...], g_ref[...])
    logits = jnp.dot(n, wr_ref[...], preferred_element_type=F32, precision=lax.Precision.HIGHEST)
    lane = lax.broadcasted_iota(jnp.int32, logits.shape, 1)
    neg = jnp.float32(-jnp.inf)
    lg = jnp.where(lane < N_EXPERTS, logits, neg)
    m1 = jnp.max(lg, axis=-1, keepdims=True)
    i1 = jnp.min(jnp.where(lg == m1, lane, LANES), axis=-1, keepdims=True)
    lg2 = jnp.where(lane == i1, neg, lg)
    m2 = jnp.max(lg2, axis=-1, keepdims=True)
    i2 = jnp.min(jnp.where(lg2 == m2, lane, LANES), axis=-1, keepdims=True)
    e2 = jnp.exp(m2 - m1)
    g1 = 1.0 / (1.0 + e2)
    g2 = e2 / (1.0 + e2)
    c_ref[...] = jnp.where(lane == i1, g1, 0.0) + jnp.where(lane == i2, g2, 0.0)


def _router(x, g, wr_pad, tm):
    m, d = x.shape
    return pl.pallas_call(
        _router_kernel,
        grid=(m // tm,),
        in_specs=[
            pl.BlockSpec((tm, d), lambda i: (i, 0)),
            pl.BlockSpec((1, d), lambda i: (0, 0)),
            pl.BlockSpec((d, LANES), lambda i: (0, 0)),
        ],
        out_specs=pl.BlockSpec((tm, LANES), lambda i: (i, 0)),
        out_shape=jax.ShapeDtypeStruct((m, LANES), F32),
        compiler_params=_cparams(("parallel",)),
        name="router",
    )(x, g.reshape(1, d), wr_pad)


def _final_norm_kernel(x_ref, g_ref, o_ref):
    o_ref[...] = _rms(x_ref[...], g_ref[...])


def _final_norm(x, g, tm):
    m, d = x.shape
    return pl.pallas_call(
        _final_norm_kernel,
        grid=(m // tm,),
        in_specs=[pl.BlockSpec((tm, d), lambda i: (i, 0)), pl.BlockSpec((1, d), lambda i: (0, 0))],
        out_specs=pl.BlockSpec((tm, d), lambda i: (i, 0)),
        out_shape=jax.ShapeDtypeStruct((m, d), F32),
        compiler_params=_cparams(("parallel",)),
        name="final_norm",
    )(x, g.reshape(1, d))


def _gelu(x):
    return 0.5 * x * (1.0 + lax.erf(x * (1.0 / math.sqrt(2.0))))


def _gmlp_kernel(u_ref, v_ref, lng_ref, lnb_ref, ws_ref, bias_ref, y_ref):
    tt = u_ref.shape[1]
    v = _gelu(v_ref[0])
    mu = jnp.mean(v, axis=-1, keepdims=True)
    d = v - mu
    var = jnp.mean(d * d, axis=-1, keepdims=True)
    vn = d * lax.rsqrt(var + LN_EPS) * lng_ref[...] + lnb_ref[...]
    row = lax.broadcasted_iota(jnp.int32, (GCHUNK, GCHUNK), 0)
    col = lax.broadcasted_iota(jnp.int32, (GCHUNK, GCHUNK), 1)
    causal = row >= col
    first_head = lax.broadcasted_iota(jnp.int32, (1, LANES), 1) < HEAD
    ws = [jnp.where(causal, ws_ref[gi], 0.0).astype(BF16) for gi in range(MIX_HEADS)]
    for c in range(tt // GCHUNK):
        rows = slice(c * GCHUNK, (c + 1) * GCHUNK)
        outs = []
        for j in range(PAIRS):
            vp = vn[rows, j * LANES:(j + 1) * LANES].astype(BF16)
            outs.append(jnp.where(first_head, _dot(ws[2 * j], vp), _dot(ws[2 * j + 1], vp)))
        mixed = jnp.concatenate(outs, axis=1) + bias_ref[...]
        y_ref[0, rows, :] = _gelu(u_ref[0, rows, :]) * mixed


def _gmlp(p3, ln_g, ln_b, w_s, bias, tt):
    b, t, _ = p3.shape
    return pl.pallas_call(
        _gmlp_kernel,
        grid=(b, t // tt),
        in_specs=[
            pl.BlockSpec((1, tt, MIX), lambda i, j: (i, j, 0)),
            pl.BlockSpec((1, tt, MIX), lambda i, j: (i, j, 1)),
            pl.BlockSpec((1, MIX), lambda i, j: (0, 0)),
            pl.BlockSpec((1, MIX), lambda i, j: (0, 0)),
            pl.BlockSpec((MIX_HEADS, GCHUNK, GCHUNK), lambda i, j: (0, 0, 0)),
            pl.BlockSpec((GCHUNK, MIX), lambda i, j: (0, 0)),
        ],
        out_specs=pl.BlockSpec((1, tt, MIX), lambda i, j: (i, j, 0)),
        out_shape=jax.ShapeDtypeStruct((b, t, MIX), F32),
        compiler_params=_cparams(("parallel", "parallel")),
        name="gmlp",
    )(p3, p3, ln_g.reshape(1, MIX), ln_b.reshape(1, MIX), w_s, bias)


def _token_shift(cur, prev_scr, mu):
    rows = cur.shape[0]
    row = lax.broadcasted_iota(jnp.int32, cur.shape, 0)
    prev = jnp.where(row == 0, prev_scr[...], pltpu.roll(cur, 1, axis=0))
    prev_scr[...] = cur[rows - 1:rows, :]
    return cur + (prev - cur) * mu


def _rwkv_kernel(r_ref, k_ref, v_ref, l_ref,
                 mur_ref, muk_ref, muv_ref, mul_ref, wcat_ref,
                 w0_ref, a0_ref, kk_ref, ka_ref, rk_ref, lnw_ref, lnb_ref,
                 jmat_ref, tril_ref, same_ref,
                 y_ref,
                 pr_scr, pk_scr, pv_scr, pl_scr, state_scr,
                 at_scr, bt_scr, kt_scr, rt_scr, bp_scr, kp_scr, v_scr, pe_scr,
                 rk_scr, g_scr, y_scr):
    tt = r_ref.shape[1]
    L = SCAN_CHUNK

    @pl.when(pl.program_id(1) == 0)
    def _():
        pr_scr[...] = jnp.zeros_like(pr_scr)
        pk_scr[...] = jnp.zeros_like(pk_scr)
        pv_scr[...] = jnp.zeros_like(pv_scr)
        pl_scr[...] = jnp.zeros_like(pl_scr)
        state_scr[...] = jnp.zeros_like(state_scr)

    r = _token_shift(r_ref[0], pr_scr, mur_ref[...])
    k = _token_shift(k_ref[0], pk_scr, muk_ref[...])
    v = _token_shift(v_ref[0], pv_scr, muv_ref[...])
    lo = _token_shift(l_ref[0], pl_scr, mul_ref[...])
    llane = lax.broadcasted_iota(jnp.int32, lo.shape, 1)
    z = jnp.where(llane < 64, jnp.tanh(lo), jnp.where(llane < 128, lo, jax.nn.sigmoid(lo)))
    proj = _dot(z.astype(BF16), wcat_ref[...])
    w = -jax.nn.softplus(-(w0_ref[...] + proj[:, :MIX])) - 0.5
    a = jax.nn.sigmoid(a0_ref[...] + proj[:, MIX:2 * MIX])
    jmat = jmat_ref[...]
    kk = k * kk_ref[...]
    kk_sq = kk * kk
    ss = jnp.concatenate(
        [_dot_split(kk_sq[:, j * LANES:(j + 1) * LANES], jmat) for j in range(PAIRS)], axis=1)
    kk = kk / jnp.maximum(jnp.sqrt(ss), 1e-12)
    k = k * (1.0 + (a - 1.0) * ka_ref[...])
    b = kk * a
    ld = -jnp.exp(w)
    cum = _split_dot(tril_ref[...], ld)
    cum_end = _split_dot(same_ref[...], ld)
    p_inv = jnp.exp(-cum)
    to_end = jnp.exp(cum_end - cum)
    at_scr[...] = -(kk * jnp.exp(cum - ld))
    bt_scr[...] = b * p_inv
    kt_scr[...] = k * p_inv
    rt_scr[...] = r * jnp.exp(cum)
    bp_scr[...] = b * to_end
    kp_scr[...] = k * to_end
    v_scr[...] = v
    pe_scr[...] = jnp.exp(cum_end)
    rk_scr[...] = r * k * rk_ref[...]
    g_scr[...] = proj[:, 2 * MIX:]

    lane = lax.broadcasted_iota(jnp.int32, (1, LANES), 1)
    h0 = lane < HEAD
    row = lax.broadcasted_iota(jnp.int32, (L, LANES), 0)
    col = lax.broadcasted_iota(jnp.int32, (L, LANES), 1) % HEAD
    strict = row > col
    incl = row >= col
    srow = lax.broadcasted_iota(jnp.int32, (LANES, LANES), 0) < HEAD
    scol = lax.broadcasted_iota(jnp.int32, (LANES, LANES), 1) < HEAD
    same_head = srow == scol
    pairs = range(PAIRS)

    def cat(x, y):
        return jnp.concatenate([x, y], axis=0)

    def first(x):
        return jnp.where(h0, x, 0.0)

    def second(x):
        return jnp.where(h0, 0.0, x)

    def chunk_body(c, carry):
        start = pl.multiple_of(c * L, L)
        rows = pl.ds(start, L)
        ld_tile = lambda ref: [ref[rows, j * LANES:(j + 1) * LANES] for j in pairs]
        at, bt, kt, rt = ld_tile(at_scr), ld_tile(bt_scr), ld_tile(kt_scr), ld_tile(rt_scr)
        bp, kp, vc = ld_tile(bp_scr), ld_tile(kp_scr), ld_tile(v_scr)
        pe = [pe_scr[pl.ds(start, 1), j * LANES:(j + 1) * LANES] for j in pairs]
        s = [state_scr[j] for j in pairs]
        g_0 = [_dot_nt(cat(first(at[j]), first(rt[j])).astype(BF16), cat(bt[j], kt[j]).astype(BF16))
               for j in pairs]
        g_1 = [_dot_nt(cat(second(at[j]), second(rt[j])).astype(BF16), cat(kt[j], bt[j]).astype(BF16))
               for j in pairs]
        ars = [_dot_nt(cat(at[j], rt[j]).astype(BF16), s[j].astype(BF16)) for j in pairs]
        top0 = [jnp.where(strict, g_0[j][:L], 0.0) for j in pairs]
        top1 = [jnp.where(strict, g_1[j][:L], 0.0) for j in pairs]
        vv = [cat(vc[j], vc[j]).astype(BF16) for j in pairs]
        akv0 = [_dot(second(top0[j]).astype(BF16), vv[j]) for j in pairs]
        akv1 = [_dot(first(top1[j]).astype(BF16), vv[j]) for j in pairs]
        rhs = [ars[j][:L] + jnp.where(h0, akv0[j], akv1[j]) for j in pairs]
        x = [cat(first(rhs[j]), second(rhs[j])) for j in pairs]
        apow = [cat(first(top0[j]), second(top1[j])).astype(BF16) for j in pairs]
        n_steps = 6
        for i in range(n_steps):
            x = [x[j] + _dot(apow[j], x[j].astype(BF16)) for j in pairs]
            if i + 1 < n_steps:
                apow = [_dot(apow[j], apow[j]).astype(BF16) for j in pairs]
        u = [x[j][:L] + x[j][L:] for j in pairs]
        uv = [cat(u[j], vc[j]).astype(BF16) for j in pairs]
        vu = [cat(vc[j], u[j]).astype(BF16) for j in pairs]
        y0 = [_dot(jnp.where(incl, g_0[j][L:], 0.0).astype(BF16), uv[j]) for j in pairs]
        y1 = [_dot(jnp.where(incl, g_1[j][L:], 0.0).astype(BF16), vu[j]) for j in pairs]
        upd = [_dot_tn(uv[j], cat(bp[j], kp[j]).astype(BF16)) for j in pairs]
        for j in pairs:
            y_scr[rows, j * LANES:(j + 1) * LANES] = ars[j][L:] + jnp.where(h0, y0[j], y1[j])
            state_scr[j] = s[j] * pe[j] + jnp.where(same_head, upd[j], 0.0)
        return carry

    lax.fori_loop(0, tt // L, chunk_body, 0)

    outs = []
    for j in pairs:
        cols = slice(j * LANES, (j + 1) * LANES)
        y = y_scr[:, cols]
        mean = _dot_split(y, jmat) * (1.0 / HEAD)
        d = y - mean
        var = _dot_split(d * d, jmat) * (1.0 / HEAD)
        yn = d * lax.rsqrt(var + GN_EPS) * lnw_ref[:, cols] + lnb_ref[:, cols]
        rk = _dot_split(rk_scr[:, cols], jmat)
        outs.append((yn + rk * v_scr[:, cols]) * g_scr[:, cols])
    y_ref[0] = jnp.concatenate(outs, axis=1)


def _rwkv(p3, mu, wcat, w0, a0, k_k, k_a, r_k, lnx_w, lnx_b, tt):
    b, t, _ = p3.shape
    row = lambda x: x.reshape(1, -1)
    head_of = jnp.arange(LANES) // HEAD
    jmat = (head_of[:, None] == head_of[None, :]).astype(BF16)
    idx = jnp.arange(tt)
    same = (idx[:, None] // SCAN_CHUNK) == (idx[None, :] // SCAN_CHUNK)
    tril = (same & (idx[:, None] >= idx[None, :])).astype(BF16)
    const = lambda shape: pl.BlockSpec(shape, lambda i, j: (0,) * len(shape))
    tile = pltpu.VMEM((tt, MIX), F32)
    return pl.pallas_call(
        _rwkv_kernel,
        grid=(b, t // tt),
        in_specs=[
            pl.BlockSpec((1, tt, MIX), lambda i, j: (i, j, 0)),
            pl.BlockSpec((1, tt, MIX), lambda i, j: (i, j, 1)),
            pl.BlockSpec((1, tt, MIX), lambda i, j: (i, j, 2)),
            pl.BlockSpec((1, tt, LORA_W), lambda i, j: (i, j, 3 * MIX // LORA_W)),
            const((1, MIX)), const((1, MIX)), const((1, MIX)), const((1, LORA_W)),
            const((LORA_W, 3 * MIX)),
            const((1, MIX)), const((1, MIX)), const((1, MIX)), const((1, MIX)),
            const((1, MIX)), const((1, MIX)), const((1, MIX)),
            const((LANES, LANES)), const((tt, tt)), const((tt, tt)),
        ],
        out_specs=pl.BlockSpec((1, tt, MIX), lambda i, j: (i, j, 0)),
        out_shape=jax.ShapeDtypeStruct((b, t, MIX), F32),
        scratch_shapes=[
            pltpu.VMEM((1, MIX), F32), pltpu.VMEM((1, MIX), F32), pltpu.VMEM((1, MIX), F32),
            pltpu.VMEM((1, LORA_W), F32),
            pltpu.VMEM((PAIRS, LANES, LANES), F32),
        ] + [tile] * 11,
        compiler_params=_cparams(("parallel", "arbitrary")),
        name="rwkv7",
    )(p3, p3, p3, p3,
      row(mu[:MIX]), row(mu[MIX:2 * MIX]), row(mu[2 * MIX:3 * MIX]), row(mu[3 * MIX:]),
      wcat, row(w0), row(a0), row(k_k), row(k_a), row(r_k), row(lnx_w), row(lnx_b),
      jmat, tril, same.astype(BF16))


def kernel(x, mem, mem_norm_g, norm1_g, w_kv_mem, w_out, norm2_g, rwkv_w_in, rwkv_mu, rwkv_w0, rwkv_w2, rwkv_a0, rwkv_a2, rwkv_g2, rwkv_k_k, rwkv_k_a, rwkv_r_k, rwkv_lnx_w, rwkv_lnx_b, ffn_w_gu, ffn_w_down, gmlp_w_in, gmlp_v_ln_g, gmlp_v_ln_b, gmlp_w_s, gmlp_b_s, moe_router, moe_w_gu, moe_w_down, final_norm_g):
    b, t, d = x.shape
    n_tok = b * t
    depth = norm1_g.shape[0]
    xs = x.reshape(n_tok, d)
    mem2 = mem.reshape(b * N_MEM, d)
    for i in range(depth):
        j = i // 2
        kv = _norm_matmul(mem2, mem_norm_g, w_kv_mem[i].astype(BF16), 512).reshape(b, N_MEM, 2 * MEMW)
        if i % 2 == 0:
            p = _norm_matmul(xs, norm1_g[i], rwkv_w_in[j].astype(BF16), 512)
            p3 = p.reshape(b, t, RWKV_COLS + MEMW)
            wcat = jnp.zeros((LORA_W, 3 * MIX), F32)
            wcat = wcat.at[0:64, 0:MIX].set(rwkv_w2[j])
            wcat = wcat.at[64:128, MIX:2 * MIX].set(rwkv_a2[j])
            wcat = wcat.at[128:256, 2 * MIX:].set(rwkv_g2[j])
            y = _rwkv(p3, rwkv_mu[j], wcat.astype(BF16), rwkv_w0[j], rwkv_a0[j], rwkv_k_k[j],
                      rwkv_k_a[j], rwkv_r_k[j].reshape(MIX), rwkv_lnx_w[j], rwkv_lnx_b[j], 256)
            q_block = RWKV_COLS // MEMW
        else:
            p = _norm_matmul(xs, norm1_g[i], gmlp_w_in[j].astype(BF16), 512)
            p3 = p.reshape(b, t, 2 * MIX + MEMW)
            bias = jnp.repeat(jnp.transpose(gmlp_b_s[j]), HEAD, axis=1)
            y = _gmlp(p3, gmlp_v_ln_g[j], gmlp_v_ln_b[j], gmlp_w_s[j], bias, 256)
            q_block = 2 * MIX // MEMW
        o = _mem_attn(p3, q_block, kv, 512)
        wo = w_out[i].astype(BF16)
        xs = _mixer_out(xs, y.reshape(n_tok, MIX), o.reshape(n_tok, MEMW), wo[:MIX], wo[MIX:], 512)
        if i % 2 == 0:
            xs = _ffn(xs, norm2_g[i], ffn_w_gu[j].astype(BF16), ffn_w_down[j].astype(BF16), 512, 1408)
        else:
            wr_pad = jnp.zeros((d, LANES), F32).at[:, :N_EXPERTS].set(moe_router[j])
            combine = _router(xs, norm2_g[i], wr_pad, 512)
            xs = _moe_dense(xs, norm2_g[i], combine, moe_w_gu[j].astype(BF16),
                            moe_w_down[j].astype(BF16), 512, 1408)
    return _final_norm(xs, final_norm_g, 512).reshape(b, t, d)
```

```python
import functools
import math

import jax
import jax.numpy as jnp
from jax import lax
from jax.experimental import pallas as pl
from jax.experimental.pallas import tpu as pltpu

F32 = jnp.float32
BF16 = jnp.bfloat16

D_MODEL = 1024
HEAD = 64
MIX = 768
MIX_HEADS = MIX // HEAD
MEMW = 256
MEM_HEADS = 4
N_MEM = 256
LORA_W = 256
RWKV_COLS = 3 * MIX + LORA_W
GCHUNK = 128
D_FF = 2816
N_EXPERTS = 8
RMS_EPS = 1e-6
GN_EPS = 64e-5
LN_EPS = 1e-5

LANES = 128
PAIRS = MIX // LANES
SCAN_CHUNK = 64
VMEM_LIMIT = 56 * 1024 * 1024


def _cparams(sem):
    return pltpu.CompilerParams(dimension_semantics=sem, vmem_limit_bytes=VMEM_LIMIT)


def _rms(x, g):
    ms = jnp.mean(x * x, axis=-1, keepdims=True)
    return x * lax.rsqrt(ms + RMS_EPS) * g


def _dot(a, b):
    return jnp.dot(a, b, preferred_element_type=F32)


def _dot_nt(a, b):
    return lax.dot_general(a, b, (((1,), (1,)), ((), ())), preferred_element_type=F32)


def _dot_tn(a, b):
    return lax.dot_general(a, b, (((0,), (0,)), ((), ())), preferred_element_type=F32)


def _dot_split(x, m_bf16):
    hi = x.astype(BF16)
    lo = (x - hi.astype(F32)).astype(BF16)
    return _dot(hi, m_bf16) + _dot(lo, m_bf16)


def _split_dot(m_bf16, x):
    hi = x.astype(BF16)
    lo = (x - hi.astype(F32)).astype(BF16)
    return _dot(m_bf16, hi) + _dot(m_bf16, lo)


def _norm_matmul_kernel(x_ref, g_ref, w_ref, o_ref):
    n = _rms(x_ref[...], g_ref[...]).astype(BF16)
    o_ref[...] = _dot(n, w_ref[...])


def _norm_matmul(x, g, w, tm):
    m, k = x.shape
    n = w.shape[1]
    return pl.pallas_call(
        _norm_matmul_kernel,
        grid=(m // tm,),
        in_specs=[
            pl.BlockSpec((tm, k), lambda i: (i, 0)),
            pl.BlockSpec((1, k), lambda i: (0, 0)),
            pl.BlockSpec((k, n), lambda i: (0, 0)),
        ],
        out_specs=pl.BlockSpec((tm, n), lambda i: (i, 0)),
        out_shape=jax.ShapeDtypeStruct((m, n), F32),
        compiler_params=_cparams(("parallel",)),
        name="norm_matmul",
    )(x, g.reshape(1, k), w)


def _mem_attn_kernel(q_ref, kv_ref, o_ref):
    q = q_ref[0]
    kv = kv_ref[0]
    k = kv[:, :MEMW].astype(BF16)
    v = kv[:, MEMW:].astype(BF16)
    lane = lax.broadcasted_iota(jnp.int32, (1, MEMW), 1)
    acc = jnp.zeros(q.shape, F32)
    for h in range(MEM_HEADS):
        m = (lane >= h * HEAD) & (lane < (h + 1) * HEAD)
        qh = jnp.where(m, q, 0.0).astype(BF16)
        s = _dot_nt(qh, k) * (1.0 / math.sqrt(HEAD))
        s = s - jnp.max(s, axis=-1, keepdims=True)
        p = jnp.exp(s)
        l = jnp.sum(p, axis=-1, keepdims=True)
        pv = _dot(p.astype(BF16), v)
        acc = acc + jnp.where(m, pv / l, 0.0)
    o_ref[0] = acc


def _mem_attn(p3, q_block, kv3, tt):
    b, t, _ = p3.shape
    return pl.pallas_call(
        _mem_attn_kernel,
        grid=(b, t // tt),
        in_specs=[
            pl.BlockSpec((1, tt, MEMW), lambda i, j: (i, j, q_block)),
            pl.BlockSpec((1, N_MEM, 2 * MEMW), lambda i, j: (i, 0, 0)),
        ],
        out_specs=pl.BlockSpec((1, tt, MEMW), lambda i, j: (i, j, 0)),
        out_shape=jax.ShapeDtypeStruct((b, t, MEMW), F32),
        compiler_params=_cparams(("parallel", "parallel")),
        name="mem_attn",
    )(p3, kv3)


def _mixer_out_kernel(x_ref, y_ref, o_ref, wy_ref, wo_ref, out_ref):
    out_ref[...] = (x_ref[...]
                    + _dot(y_ref[...].astype(BF16), wy_ref[...])
                    + _dot(o_ref[...].astype(BF16), wo_ref[...]))


def _mixer_out(x, y, o, wy, wo, tm):
    m, d = x.shape
    return pl.pallas_call(
        _mixer_out_kernel,
        grid=(m // tm,),
        in_specs=[
            pl.BlockSpec((tm, d), lambda i: (i, 0)),
            pl.BlockSpec((tm, MIX), lambda i: (i, 0)),
            pl.BlockSpec((tm, MEMW), lambda i: (i, 0)),
            pl.BlockSpec((MIX, d), lambda i: (0, 0)),
            pl.BlockSpec((MEMW, d), lambda i: (0, 0)),
        ],
        out_specs=pl.BlockSpec((tm, d), lambda i: (i, 0)),
        out_shape=jax.ShapeDtypeStruct((m, d), F32),
        compiler_params=_cparams(("parallel",)),
        name="mixer_out",
    )(x, y, o, wy, wo)


def _swiglu_kernel(x_ref, g_ref, wg_ref, wu_ref, wd_ref, out_ref, n_scr, acc_scr):
    step = pl.program_id(1)

    @pl.when(step == 0)
    def _():
        n_scr[...] = _rms(x_ref[...], g_ref[...]).astype(BF16)
        acc_scr[...] = jnp.zeros_like(acc_scr)

    n = n_scr[...]
    gate = _dot(n, wg_ref[...])
    up = _dot(n, wu_ref[...])
    h = gate * jax.nn.sigmoid(gate) * up
    acc_scr[...] += _dot(h.astype(BF16), wd_ref[...])

    @pl.when(step == pl.num_programs(1) - 1)
    def _():
        out_ref[...] = x_ref[...] + acc_scr[...]


def _ffn(x, g, w_gu, w_down, tm, tf):
    m, d = x.shape
    nf = D_FF // tf
    return pl.pallas_call(
        _swiglu_kernel,
        grid=(m // tm, nf),
        in_specs=[
            pl.BlockSpec((tm, d), lambda i, j: (i, 0)),
            pl.BlockSpec((1, d), lambda i, j: (0, 0)),
            pl.BlockSpec((d, tf), lambda i, j: (0, j)),
            pl.BlockSpec((d, tf), lambda i, j: (0, nf + j)),
            pl.BlockSpec((tf, d), lambda i, j: (j, 0)),
        ],
        out_specs=pl.BlockSpec((tm, d), lambda i, j: (i, 0)),
        out_shape=jax.ShapeDtypeStruct((m, d), F32),
        scratch_shapes=[pltpu.VMEM((tm, d), BF16), pltpu.VMEM((tm, d), F32)],
        compiler_params=_cparams(("parallel", "arbitrary")),
        name="ffn",
    )(x, g.reshape(1, d), w_gu, w_gu, w_down)


ROUTE_E1, ROUTE_E2, ROUTE_R1, ROUTE_R2, ROUTE_G1, ROUTE_G2 = range(6)


def _route_kernel(x_ref, g_ref, wr_ref, tril_ref, route_ref, cnt_ref, carry_scr):
    @pl.when(pl.program_id(0) == 0)
    def _():
        carry_scr[...] = jnp.zeros_like(carry_scr)

    n = _rms(x_ref[...], g_ref[...])
    logits = jnp.dot(n, wr_ref[...], preferred_element_type=F32, precision=lax.Precision.HIGHEST)
    lane = lax.broadcasted_iota(jnp.int32, logits.shape, 1)
    neg = jnp.float32(-jnp.inf)
    lg = jnp.where(lane < N_EXPERTS, logits, neg)
    m1 = jnp.max(lg, axis=-1, keepdims=True)
    i1 = jnp.min(jnp.where(lg == m1, lane, LANES), axis=-1, keepdims=True)
    lg2 = jnp.where(lane == i1, neg, lg)
    m2 = jnp.max(lg2, axis=-1, keepdims=True)
    i2 = jnp.min(jnp.where(lg2 == m2, lane, LANES), axis=-1, keepdims=True)
    e2 = jnp.exp(m2 - m1)
    g1 = 1.0 / (1.0 + e2)
    g2 = e2 / (1.0 + e2)
    chosen = jnp.where(lane == i1, 1.0, jnp.where(lane == i2, 1.0, 0.0))
    before = _dot(tril_ref[...], chosen.astype(BF16)) + carry_scr[...]
    r1 = jnp.sum(jnp.where(lane == i1, before, 0.0), axis=-1, keepdims=True)
    r2 = jnp.sum(jnp.where(lane == i2, before, 0.0), axis=-1, keepdims=True)
    carry_scr[...] += jnp.sum(chosen, axis=0, keepdims=True)
    cnt_ref[...] = carry_scr[...]
    fields = (i1.astype(F32), i2.astype(F32), r1, r2, g1, g2)
    route = jnp.zeros(logits.shape, F32)
    for idx, val in enumerate(fields):
        route = jnp.where(lane == idx, val, route)
    route_ref[...] = route


def _route(x, g, wr_pad, tm):
    m, d = x.shape
    idx = jnp.arange(tm)
    tril = (idx[:, None] > idx[None, :]).astype(BF16)
    return pl.pallas_call(
        _route_kernel,
        grid=(m // tm,),
        in_specs=[
            pl.BlockSpec((tm, d), lambda i: (i, 0)),
            pl.BlockSpec((1, d), lambda i: (0, 0)),
            pl.BlockSpec((d, LANES), lambda i: (0, 0)),
            pl.BlockSpec((tm, tm), lambda i: (0, 0)),
        ],
        out_specs=[pl.BlockSpec((tm, LANES), lambda i: (i, 0)),
                   pl.BlockSpec((1, LANES), lambda i: (0, 0))],
        out_shape=[jax.ShapeDtypeStruct((m, LANES), F32), jax.ShapeDtypeStruct((1, LANES), F32)],
        scratch_shapes=[pltpu.VMEM((1, LANES), F32)],
        compiler_params=_cparams(("arbitrary",)),
        name="route",
    )(x, g.reshape(1, d), wr_pad, tril)


def _row_copy(src, src_row, dst, dst_row, sem):
    return pltpu.make_async_copy(src.at[pl.ds(src_row, 1)], dst.at[pl.ds(dst_row, 1)], sem)


def _dispatch_kernel(p1_ref, p2_ref, x_ref, g_ref, init_ref, xs_ref, n_scr, sem):
    del init_ref
    tm = x_ref.shape[0]
    base = pl.program_id(0) * tm
    n_scr[...] = _rms(x_ref[...], g_ref[...])

    def issue(i, carry):
        _row_copy(n_scr, i, xs_ref, p1_ref[base + i], sem).start()
        _row_copy(n_scr, i, xs_ref, p2_ref[base + i], sem).start()
        return carry

    def drain(i, carry):
        _row_copy(n_scr, i, xs_ref, p1_ref[base + i], sem).wait()
        _row_copy(n_scr, i, xs_ref, p2_ref[base + i], sem).wait()
        return carry

    lax.fori_loop(0, tm, issue, 0)
    lax.fori_loop(0, tm, drain, 0)


def _dispatch(x, g, pos1, pos2, rows, tm):
    m, d = x.shape
    return pl.pallas_call(
        _dispatch_kernel,
        grid_spec=pltpu.PrefetchScalarGridSpec(
            num_scalar_prefetch=2,
            grid=(m // tm,),
            in_specs=[
                pl.BlockSpec((tm, d), lambda i, p1, p2: (i, 0)),
                pl.BlockSpec((1, d), lambda i, p1, p2: (0, 0)),
                pl.BlockSpec(memory_space=pl.ANY),
            ],
            out_specs=pl.BlockSpec(memory_space=pl.ANY),
            scratch_shapes=[pltpu.VMEM((tm, d), F32), pltpu.SemaphoreType.DMA],
        ),
        out_shape=jax.ShapeDtypeStruct((rows, d), F32),
        input_output_aliases={4: 0},
        compiler_params=_cparams(("arbitrary",)),
        name="dispatch",
    )(pos1, pos2, x, g.reshape(1, d), jnp.zeros((rows, d), F32))


def _experts_kernel(te_ref, nv_ref, x_ref, wg_ref, wu_ref, wd_ref, y_ref, n_scr, acc_scr):
    del te_ref
    step = pl.program_id(1)
    valid = pl.program_id(0) < nv_ref[0]

    @pl.when(step == 0)
    def _():
        n_scr[...] = x_ref[...].astype(BF16)
        acc_scr[...] = jnp.zeros_like(acc_scr)

    @pl.when(valid)
    def _():
        n = n_scr[...]
        gate = _dot(n, wg_ref[...])
        up = _dot(n, wu_ref[...])
        h = gate * jax.nn.sigmoid(gate) * up
        acc_scr[...] += _dot(h.astype(BF16), wd_ref[...])

    @pl.when(step == pl.num_programs(1) - 1)
    def _():
        y_ref[...] = acc_scr[...]


def _experts(xs, tile_expert, n_valid, w_gu, w_down, tm, tf):
    rows, d = xs.shape
    nf = D_FF // tf
    return pl.pallas_call(
        _experts_kernel,
        grid_spec=pltpu.PrefetchScalarGridSpec(
            num_scalar_prefetch=2,
            grid=(rows // tm, nf),
            in_specs=[
                pl.BlockSpec((tm, d), lambda t, j, te, nv: (t, 0)),
                pl.BlockSpec((None, d, tf), lambda t, j, te, nv: (te[t], 0, j)),
                pl.BlockSpec((None, d, tf), lambda t, j, te, nv: (te[t], 0, nf + j)),
                pl.BlockSpec((None, tf, d), lambda t, j, te, nv: (te[t], j, 0)),
            ],
            out_specs=pl.BlockSpec((tm, d), lambda t, j, te, nv: (t, 0)),
            scratch_shapes=[pltpu.VMEM((tm, d), BF16), pltpu.VMEM((tm, d), F32)],
        ),
        out_shape=jax.ShapeDtypeStruct((rows, d), F32),
        compiler_params=_cparams(("arbitrary", "arbitrary")),
        name="experts",
    )(tile_expert, n_valid, xs, w_gu, w_gu, w_down)


def _combine_kernel(p1_ref, p2_ref, x_ref, route_ref, fg_ref, ys_ref, out_ref, buf, sem, *, final):
    tm = x_ref.shape[0]
    base = pl.program_id(0) * tm

    def issue(i, carry):
        _row_copy(ys_ref, p1_ref[base + i], buf.at[0], i, sem).start()
        _row_copy(ys_ref, p2_ref[base + i], buf.at[1], i, sem).start()
        return carry

    def drain(i, carry):
        _row_copy(ys_ref, p1_ref[base + i], buf.at[0], i, sem).wait()
        _row_copy(ys_ref, p2_ref[base + i], buf.at[1], i, sem).wait()
        return carry

    lax.fori_loop(0, tm, issue, 0)
    lax.fori_loop(0, tm, drain, 0)
    route = route_ref[...]
    g1 = route[:, ROUTE_G1:ROUTE_G1 + 1]
    g2 = route[:, ROUTE_G2:ROUTE_G2 + 1]
    out = x_ref[...] + g1 * buf[0] + g2 * buf[1]
    out_ref[...] = _rms(out, fg_ref[...]) if final else out


def _combine(x, route, ys, pos1, pos2, final_g, tm):
    m, d = x.shape
    final = final_g is not None
    fg = final_g.reshape(1, d) if final else jnp.ones((1, d), F32)
    return pl.pallas_call(
        functools.partial(_combine_kernel, final=final),
        grid_spec=pltpu.PrefetchScalarGridSpec(
            num_scalar_prefetch=2,
            grid=(m // tm,),
            in_specs=[
                pl.BlockSpec((tm, d), lambda i, p1, p2: (i, 0)),
                pl.BlockSpec((tm, LANES), lambda i, p1, p2: (i, 0)),
                pl.BlockSpec((1, d), lambda i, p1, p2: (0, 0)),
                pl.BlockSpec(memory_space=pl.ANY),
            ],
            out_specs=pl.BlockSpec((tm, d), lambda i, p1, p2: (i, 0)),
            scratch_shapes=[pltpu.VMEM((2, tm, d), F32), pltpu.SemaphoreType.DMA],
        ),
        out_shape=jax.ShapeDtypeStruct((m, d), F32),
        compiler_params=_cparams(("arbitrary",)),
        name="combine",
    )(pos1, pos2, x, route, fg, ys)


def _moe(x, g, w_router, w_gu, w_down, final_g, tm_route, tm_rows, tm_expert, tf):
    m, d = x.shape
    wr_pad = jnp.zeros((d, LANES), F32).at[:, :N_EXPERTS].set(w_router)
    route, counts = _route(x, g, wr_pad, tm_route)
    counts = counts[0, :N_EXPERTS].astype(jnp.int32)
    tiles = (counts + tm_expert - 1) // tm_expert
    tile_end = jnp.cumsum(tiles)
    row_start = (tile_end - tiles) * tm_expert
    e1 = route[:, ROUTE_E1].astype(jnp.int32)
    e2 = route[:, ROUTE_E2].astype(jnp.int32)
    pos1 = row_start[e1] + route[:, ROUTE_R1].astype(jnp.int32)
    pos2 = row_start[e2] + route[:, ROUTE_R2].astype(jnp.int32)
    n_tiles = 2 * m // tm_expert + N_EXPERTS
    tile_expert = jnp.minimum(
        jnp.searchsorted(tile_end, jnp.arange(n_tiles, dtype=jnp.int32), side="right"),
        N_EXPERTS - 1).astype(jnp.int32)
    n_valid = tile_end[-1:].astype(jnp.int32)
    xs = _dispatch(x, g, pos1, pos2, n_tiles * tm_expert, tm_rows)
    ys = _experts(xs, tile_expert, n_valid, w_gu, w_down, tm_expert, tf)
    return _combine(x, route, ys, pos1, pos2, final_g, tm_rows)


def _final_norm_kernel(x_ref, g_ref, o_ref):
    o_ref[...] = _rms(x_ref[...], g_ref[...])


def _final_norm(x, g, tm):
    m, d = x.shape
    return pl.pallas_call(
        _final_norm_kernel,
        grid=(m // tm,),
        in_specs=[pl.BlockSpec((tm, d), lambda i: (i, 0)), pl.BlockSpec((1, d), lambda i: (0, 0))],
        out_specs=pl.BlockSpec((tm, d), lambda i: (i, 0)),
        out_shape=jax.ShapeDtypeStruct((m, d), F32),
        compiler_params=_cparams(("parallel",)),
        name="final_norm",
    )(x, g.reshape(1, d))


def _gelu(x):
    return 0.5 * x * (1.0 + lax.erf(x * (1.0 / math.sqrt(2.0))))


def _gmlp_kernel(u_ref, v_ref, lng_ref, lnb_ref, ws_ref, bias_ref, y_ref):
    tt = u_ref.shape[1]
    v = _gelu(v_ref[0])
    mu = jnp.mean(v, axis=-1, keepdims=True)
    d = v - mu
    var = jnp.mean(d * d, axis=-1, keepdims=True)
    vn = d * lax.rsqrt(var + LN_EPS) * lng_ref[...] + lnb_ref[...]
    row = lax.broadcasted_iota(jnp.int32, (GCHUNK, GCHUNK), 0)
    col = lax.broadcasted_iota(jnp.int32, (GCHUNK, GCHUNK), 1)
    causal = row >= col
    first_head = lax.broadcasted_iota(jnp.int32, (1, LANES), 1) < HEAD
    ws = [jnp.where(causal, ws_ref[gi], 0.0).astype(BF16) for gi in range(MIX_HEADS)]
    for c in range(tt // GCHUNK):
        rows = slice(c * GCHUNK, (c + 1) * GCHUNK)
        outs = []
        for j in range(PAIRS):
            vp = vn[rows, j * LANES:(j + 1) * LANES].astype(BF16)
            outs.append(jnp.where(first_head, _dot(ws[2 * j], vp), _dot(ws[2 * j + 1], vp)))
        mixed = jnp.concatenate(outs, axis=1) + bias_ref[...]
        y_ref[0, rows, :] = _gelu(u_ref[0, rows, :]) * mixed


def _gmlp(p3, ln_g, ln_b, w_s, bias, tt):
    b, t, _ = p3.shape
    return pl.pallas_call(
        _gmlp_kernel,
        grid=(b, t // tt),
        in_specs=[
            pl.BlockSpec((1, tt, MIX), lambda i, j: (i, j, 0)),
            pl.BlockSpec((1, tt, MIX), lambda i, j: (i, j, 1)),
            pl.BlockSpec((1, MIX), lambda i, j: (0, 0)),
            pl.BlockSpec((1, MIX), lambda i, j: (0, 0)),
            pl.BlockSpec((MIX_HEADS, GCHUNK, GCHUNK), lambda i, j: (0, 0, 0)),
            pl.BlockSpec((GCHUNK, MIX), lambda i, j: (0, 0)),
        ],
        out_specs=pl.BlockSpec((1, tt, MIX), lambda i, j: (i, j, 0)),
        out_shape=jax.ShapeDtypeStruct((b, t, MIX), F32),
        compiler_params=_cparams(("parallel", "parallel")),
        name="gmlp",
    )(p3, p3, ln_g.reshape(1, MIX), ln_b.reshape(1, MIX), w_s, bias)


def _token_shift(cur, prev_scr, mu):
    rows = cur.shape[0]
    row = lax.broadcasted_iota(jnp.int32, cur.shape, 0)
    prev = jnp.where(row == 0, prev_scr[...], pltpu.roll(cur, 1, axis=0))
    prev_scr[...] = cur[rows - 1:rows, :]
    return cur + (prev - cur) * mu


def _rwkv_kernel(r_ref, k_ref, v_ref, l_ref,
                 mur_ref, muk_ref, muv_ref, mul_ref, wcat_ref,
                 w0_ref, a0_ref, kk_ref, ka_ref, rk_ref, lnw_ref, lnb_ref,
                 jmat_ref, tril_ref, same_ref,
                 y_ref,
                 pr_scr, pk_scr, pv_scr, pl_scr, state_scr,
                 at_scr, bt_scr, kt_scr, rt_scr, bp_scr, kp_scr, v_scr, pe_scr,
                 rk_scr, g_scr, y_scr):
    tt = r_ref.shape[1]
    L = SCAN_CHUNK

    @pl.when(pl.program_id(1) == 0)
    def _():
        pr_scr[...] = jnp.zeros_like(pr_scr)
        pk_scr[...] = jnp.zeros_like(pk_scr)
        pv_scr[...] = jnp.zeros_like(pv_scr)
        pl_scr[...] = jnp.zeros_like(pl_scr)
        state_scr[...] = jnp.zeros_like(state_scr)

    r = _token_shift(r_ref[0], pr_scr, mur_ref[...])
    k = _token_shift(k_ref[0], pk_scr, muk_ref[...])
    v = _token_shift(v_ref[0], pv_scr, muv_ref[...])
    lo = _token_shift(l_ref[0], pl_scr, mul_ref[...])
    llane = lax.broadcasted_iota(jnp.int32, lo.shape, 1)
    z = jnp.where(llane < 64, jnp.tanh(lo), jnp.where(llane < 128, lo, jax.nn.sigmoid(lo)))
    proj = _dot(z.astype(BF16), wcat_ref[...])
    w = -jax.nn.softplus(-(w0_ref[...] + proj[:, :MIX])) - 0.5
    a = jax.nn.sigmoid(a0_ref[...] + proj[:, MIX:2 * MIX])
    jmat = jmat_ref[...]
    kk = k * kk_ref[...]
    kk_sq = kk * kk
    ss = jnp.concatenate(
        [_dot_split(kk_sq[:, j * LANES:(j + 1) * LANES], jmat) for j in range(PAIRS)], axis=1)
    kk = kk / jnp.maximum(jnp.sqrt(ss), 1e-12)
    k = k * (1.0 + (a - 1.0) * ka_ref[...])
    b = kk * a
    ld = -jnp.exp(w)
    cum = _split_dot(tril_ref[...], ld)
    cum_end = _split_dot(same_ref[...], ld)
    p_inv = jnp.exp(-cum)
    to_end = jnp.exp(cum_end - cum)
    at_scr[...] = -(kk * jnp.exp(cum - ld))
    bt_scr[...] = b * p_inv
    kt_scr[...] = k * p_inv
    rt_scr[...] = r * jnp.exp(cum)
    bp_scr[...] = b * to_end
    kp_scr[...] = k * to_end
    v_scr[...] = v
    pe_scr[...] = jnp.exp(cum_end)
    rk_scr[...] = r * k * rk_ref[...]
    g_scr[...] = proj[:, 2 * MIX:]

    lane = lax.broadcasted_iota(jnp.int32, (1, LANES), 1)
    h0 = lane < HEAD
    row = lax.broadcasted_iota(jnp.int32, (L, LANES), 0)
    col = lax.broadcasted_iota(jnp.int32, (L, LANES), 1) % HEAD
    strict = row > col
    incl = row >= col
    srow = lax.broadcasted_iota(jnp.int32, (LANES, LANES), 0) < HEAD
    scol = lax.broadcasted_iota(jnp.int32, (LANES, LANES), 1) < HEAD
    same_head = srow == scol
    pairs = range(PAIRS)

    def cat(x, y):
        return jnp.concatenate([x, y], axis=0)

    def first(x):
        return jnp.where(h0, x, 0.0)

    def second(x):
        return jnp.where(h0, 0.0, x)

    def chunk_body(c, carry):
        start = pl.multiple_of(c * L, L)
        rows = pl.ds(start, L)
        ld_tile = lambda ref: [ref[rows, j * LANES:(j + 1) * LANES] for j in pairs]
        at, bt, kt, rt = ld_tile(at_scr), ld_tile(bt_scr), ld_tile(kt_scr), ld_tile(rt_scr)
        bp, kp, vc = ld_tile(bp_scr), ld_tile(kp_scr), ld_tile(v_scr)
        pe = [pe_scr[pl.ds(start, 1), j * LANES:(j + 1) * LANES] for j in pairs]
        s = [state_scr[j] for j in pairs]
        g_0 = [_dot_nt(cat(first(at[j]), first(rt[j])).astype(BF16), cat(bt[j], kt[j]).astype(BF16))
               for j in pairs]
        g_1 = [_dot_nt(cat(second(at[j]), second(rt[j])).astype(BF16), cat(kt[j], bt[j]).astype(BF16))
               for j in pairs]
        ars = [_dot_nt(cat(at[j], rt[j]).astype(BF16), s[j].astype(BF16)) for j in pairs]
        top0 = [jnp.where(strict, g_0[j][:L], 0.0) for j in pairs]
        top1 = [jnp.where(strict, g_1[j][:L], 0.0) for j in pairs]
        vv = [cat(vc[j], vc[j]).astype(BF16) for j in pairs]
        akv0 = [_dot(second(top0[j]).astype(BF16), vv[j]) for j in pairs]
        akv1 = [_dot(first(top1[j]).astype(BF16), vv[j]) for j in pairs]
        rhs = [ars[j][:L] + jnp.where(h0, akv0[j], akv1[j]) for j in pairs]
        x = [cat(first(rhs[j]), second(rhs[j])) for j in pairs]
        apow = [cat(first(top0[j]), second(top1[j])).astype(BF16) for j in pairs]
        n_steps = 6
        for i in range(n_steps):
            x = [x[j] + _dot(apow[j], x[j].astype(BF16)) for j in pairs]
            if i + 1 < n_steps:
                apow = [_dot(apow[j], apow[j]).astype(BF16) for j in pairs]
        u = [x[j][:L] + x[j][L:] for j in pairs]
        uv = [cat(u[j], vc[j]).astype(BF16) for j in pairs]
        vu = [cat(vc[j], u[j]).astype(BF16) for j in pairs]
        y0 = [_dot(jnp.where(incl, g_0[j][L:], 0.0).astype(BF16), uv[j]) for j in pairs]
        y1 = [_dot(jnp.where(incl, g_1[j][L:], 0.0).astype(BF16), vu[j]) for j in pairs]
        upd = [_dot_tn(uv[j], cat(bp[j], kp[j]).astype(BF16)) for j in pairs]
        for j in pairs:
            y_scr[rows, j * LANES:(j + 1) * LANES] = ars[j][L:] + jnp.where(h0, y0[j], y1[j])
            state_scr[j] = s[j] * pe[j] + jnp.where(same_head, upd[j], 0.0)
        return carry

    lax.fori_loop(0, tt // L, chunk_body, 0)

    outs = []
    for j in pairs:
        cols = slice(j * LANES, (j + 1) * LANES)
        y = y_scr[:, cols]
        mean = _dot_split(y, jmat) * (1.0 / HEAD)
        d = y - mean
        var = _dot_split(d * d, jmat) * (1.0 / HEAD)
        yn = d * lax.rsqrt(var + GN_EPS) * lnw_ref[:, cols] + lnb_ref[:, cols]
        rk = _dot_split(rk_scr[:, cols], jmat)
        outs.append((yn + rk * v_scr[:, cols]) * g_scr[:, cols])
    y_ref[0] = jnp.concatenate(outs, axis=1)


def _rwkv(p3, mu, wcat, w0, a0, k_k, k_a, r_k, lnx_w, lnx_b, tt):
    b, t, _ = p3.shape
    row = lambda x: x.reshape(1, -1)
    head_of = jnp.arange(LANES) // HEAD
    jmat = (head_of[:, None] == head_of[None, :]).astype(BF16)
    idx = jnp.arange(tt)
    same = (idx[:, None] // SCAN_CHUNK) == (idx[None, :] // SCAN_CHUNK)
    tril = (same & (idx[:, None] >= idx[None, :])).astype(BF16)
    const = lambda shape: pl.BlockSpec(shape, lambda i, j: (0,) * len(shape))
    tile = pltpu.VMEM((tt, MIX), F32)
    return pl.pallas_call(
        _rwkv_kernel,
        grid=(b, t // tt),
        in_specs=[
            pl.BlockSpec((1, tt, MIX), lambda i, j: (i, j, 0)),
            pl.BlockSpec((1, tt, MIX), lambda i, j: (i, j, 1)),
            pl.BlockSpec((1, tt, MIX), lambda i, j: (i, j, 2)),
            pl.BlockSpec((1, tt, LORA_W), lambda i, j: (i, j, 3 * MIX // LORA_W)),
            const((1, MIX)), const((1, MIX)), const((1, MIX)), const((1, LORA_W)),
            const((LORA_W, 3 * MIX)),
            const((1, MIX)), const((1, MIX)), const((1, MIX)), const((1, MIX)),
            const((1, MIX)), const((1, MIX)), const((1, MIX)),
            const((LANES, LANES)), const((tt, tt)), const((tt, tt)),
        ],
        out_specs=pl.BlockSpec((1, tt, MIX), lambda i, j: (i, j, 0)),
        out_shape=jax.ShapeDtypeStruct((b, t, MIX), F32),
        scratch_shapes=[
            pltpu.VMEM((1, MIX), F32), pltpu.VMEM((1, MIX), F32), pltpu.VMEM((1, MIX), F32),
            pltpu.VMEM((1, LORA_W), F32),
            pltpu.VMEM((PAIRS, LANES, LANES), F32),
        ] + [tile] * 11,
        compiler_params=_cparams(("parallel", "arbitrary")),
        name="rwkv7",
    )(p3, p3, p3, p3,
      row(mu[:MIX]), row(mu[MIX:2 * MIX]), row(mu[2 * MIX:3 * MIX]), row(mu[3 * MIX:]),
      wcat, row(w0), row(a0), row(k_k), row(k_a), row(r_k), row(lnx_w), row(lnx_b),
      jmat, tril, same.astype(BF16))


def kernel(x, mem, mem_norm_g, norm1_g, w_kv_mem, w_out, norm2_g, rwkv_w_in, rwkv_mu, rwkv_w0, rwkv_w2, rwkv_a0, rwkv_a2, rwkv_g2, rwkv_k_k, rwkv_k_a, rwkv_r_k, rwkv_lnx_w, rwkv_lnx_b, ffn_w_gu, ffn_w_down, gmlp_w_in, gmlp_v_ln_g, gmlp_v_ln_b, gmlp_w_s, gmlp_b_s, moe_router, moe_w_gu, moe_w_down, final_norm_g):
    b, t, d = x.shape
    n_tok = b * t
    depth = norm1_g.shape[0]
    xs = x.reshape(n_tok, d)
    mem2 = mem.reshape(b * N_MEM, d)
    for i in range(depth):
        j = i // 2
        kv = _norm_matmul(mem2, mem_norm_g, w_kv_mem[i].astype(BF16), 512).reshape(b, N_MEM, 2 * MEMW)
        if i % 2 == 0:
            p = _norm_matmul(xs, norm1_g[i], rwkv_w_in[j].astype(BF16), 512)
            p3 = p.reshape(b, t, RWKV_COLS + MEMW)
            wcat = jnp.zeros((LORA_W, 3 * MIX), F32)
            wcat = wcat.at[0:64, 0:MIX].set(rwkv_w2[j])
            wcat = wcat.at[64:128, MIX:2 * MIX].set(rwkv_a2[j])
            wcat = wcat.at[128:256, 2 * MIX:].set(rwkv_g2[j])
            y = _rwkv(p3, rwkv_mu[j], wcat.astype(BF16), rwkv_w0[j], rwkv_a0[j], rwkv_k_k[j],
                      rwkv_k_a[j], rwkv_r_k[j].reshape(MIX), rwkv_lnx_w[j], rwkv_lnx_b[j], 256)
            q_block = RWKV_COLS // MEMW
        else:
            p = _norm_matmul(xs, norm1_g[i], gmlp_w_in[j].astype(BF16), 512)
            p3 = p.reshape(b, t, 2 * MIX + MEMW)
            bias = jnp.repeat(jnp.transpose(gmlp_b_s[j]), HEAD, axis=1)
            y = _gmlp(p3, gmlp_v_ln_g[j], gmlp_v_ln_b[j], gmlp_w_s[j], bias, 256)
            q_block = 2 * MIX // MEMW
        o = _mem_attn(p3, q_block, kv, 512)
        wo = w_out[i].astype(BF16)
        xs = _mixer_out(xs, y.reshape(n_tok, MIX), o.reshape(n_tok, MEMW), wo[:MIX], wo[MIX:], 512)
        if i % 2 == 0:
            xs = _ffn(xs, norm2_g[i], ffn_w_gu[j].astype(BF16), ffn_w_down[j].astype(BF16), 512, 1408)
        else:
            last = i == depth - 1
            xs = _moe(xs, norm2_g[i], moe_router[j], moe_w_gu[j].astype(BF16),
                      moe_w_down[j].astype(BF16), final_norm_g if last else None,
                      512, 256, 512, 1408)
    if depth % 2 == 1:
        xs = _final_norm(xs, final_norm_g, 512)
    return xs.reshape(b, t, d)
```

```python
import functools
import math

import jax
import jax.numpy as jnp
from jax import lax
from jax.experimental import pallas as pl
from jax.experimental.pallas import tpu as pltpu

F32 = jnp.float32
BF16 = jnp.bfloat16

D_MODEL = 1024
HEAD = 64
MIX = 768
MIX_HEADS = MIX // HEAD
MEMW = 256
MEM_HEADS = 4
N_MEM = 256
LORA_W = 256
RWKV_COLS = 3 * MIX + LORA_W
GCHUNK = 128
D_FF = 2816
N_EXPERTS = 8
RMS_EPS = 1e-6
GN_EPS = 64e-5
LN_EPS = 1e-5

LANES = 128
PAIRS = MIX // LANES
SCAN_CHUNK = 64
VMEM_LIMIT = 56 * 1024 * 1024


def _cparams(sem):
    return pltpu.CompilerParams(dimension_semantics=sem, vmem_limit_bytes=VMEM_LIMIT)


def _rms(x, g):
    ms = jnp.mean(x * x, axis=-1, keepdims=True)
    return x * lax.rsqrt(ms + RMS_EPS) * g


def _dot(a, b):
    return jnp.dot(a, b, preferred_element_type=F32)


def _dot_nt(a, b):
    return lax.dot_general(a, b, (((1,), (1,)), ((), ())), preferred_element_type=F32)


def _dot_tn(a, b):
    return lax.dot_general(a, b, (((0,), (0,)), ((), ())), preferred_element_type=F32)


def _dot_split(x, m_bf16):
    hi = x.astype(BF16)
    lo = (x - hi.astype(F32)).astype(BF16)
    return _dot(hi, m_bf16) + _dot(lo, m_bf16)


def _split_dot(m_bf16, x):
    hi = x.astype(BF16)
    lo = (x - hi.astype(F32)).astype(BF16)
    return _dot(m_bf16, hi) + _dot(m_bf16, lo)


def _norm_matmul_kernel(x_ref, g_ref, w_ref, o_ref):
    n = _rms(x_ref[...], g_ref[...]).astype(BF16)
    o_ref[...] = _dot(n, w_ref[...])


def _norm_matmul(x, g, w, tm):
    m, k = x.shape
    n = w.shape[1]
    return pl.pallas_call(
        _norm_matmul_kernel,
        grid=(m // tm,),
        in_specs=[
            pl.BlockSpec((tm, k), lambda i: (i, 0)),
            pl.BlockSpec((1, k), lambda i: (0, 0)),
            pl.BlockSpec((k, n), lambda i: (0, 0)),
        ],
        out_specs=pl.BlockSpec((tm, n), lambda i: (i, 0)),
        out_shape=jax.ShapeDtypeStruct((m, n), F32),
        compiler_params=_cparams(("parallel",)),
        name="norm_matmul",
    )(x, g.reshape(1, k), w)


def _mem_attn_kernel(q_ref, kv_ref, o_ref):
    q = q_ref[0]
    kv = kv_ref[0]
    k = kv[:, :MEMW].astype(BF16)
    v = kv[:, MEMW:].astype(BF16)
    lane = lax.broadcasted_iota(jnp.int32, (1, MEMW), 1)
    acc = jnp.zeros(q.shape, F32)
    for h in range(MEM_HEADS):
        m = (lane >= h * HEAD) & (lane < (h + 1) * HEAD)
        qh = jnp.where(m, q, 0.0).astype(BF16)
        s = _dot_nt(qh, k) * (1.0 / math.sqrt(HEAD))
        s = s - jnp.max(s, axis=-1, keepdims=True)
        p = jnp.exp(s)
        l = jnp.sum(p, axis=-1, keepdims=True)
        pv = _dot(p.astype(BF16), v)
        acc = acc + jnp.where(m, pv / l, 0.0)
    o_ref[0] = acc


def _mem_attn(p3, q_block, kv3, tt):
    b, t, _ = p3.shape
    return pl.pallas_call(
        _mem_attn_kernel,
        grid=(b, t // tt),
        in_specs=[
            pl.BlockSpec((1, tt, MEMW), lambda i, j: (i, j, q_block)),
            pl.BlockSpec((1, N_MEM, 2 * MEMW), lambda i, j: (i, 0, 0)),
        ],
        out_specs=pl.BlockSpec((1, tt, MEMW), lambda i, j: (i, j, 0)),
        out_shape=jax.ShapeDtypeStruct((b, t, MEMW), F32),
        compiler_params=_cparams(("parallel", "parallel")),
        name="mem_attn",
    )(p3, kv3)


def _mixer_out_kernel(x_ref, y_ref, o_ref, wy_ref, wo_ref, out_ref):
    out_ref[...] = (x_ref[...]
                    + _dot(y_ref[...].astype(BF16), wy_ref[...])
                    + _dot(o_ref[...].astype(BF16), wo_ref[...]))


def _mixer_out(x, y, o, wy, wo, tm):
    m, d = x.shape
    return pl.pallas_call(
        _mixer_out_kernel,
        grid=(m // tm,),
        in_specs=[
            pl.BlockSpec((tm, d), lambda i: (i, 0)),
            pl.BlockSpec((tm, MIX), lambda i: (i, 0)),
            pl.BlockSpec((tm, MEMW), lambda i: (i, 0)),
            pl.BlockSpec((MIX, d), lambda i: (0, 0)),
            pl.BlockSpec((MEMW, d), lambda i: (0, 0)),
        ],
        out_specs=pl.BlockSpec((tm, d), lambda i: (i, 0)),
        out_shape=jax.ShapeDtypeStruct((m, d), F32),
        compiler_params=_cparams(("parallel",)),
        name="mixer_out",
    )(x, y, o, wy, wo)


def _swiglu(n, wg_ref, wu_ref, wd_ref):
    gate = _dot(n, wg_ref[...])
    up = _dot(n, wu_ref[...])
    h = gate * jax.nn.sigmoid(gate) * up
    return _dot(h.astype(BF16), wd_ref[...])


def _ffn_kernel(x_ref, g_ref, wg_ref, wu_ref, wd_ref, out_ref):
    x = x_ref[...]
    out_ref[...] = x + _swiglu(_rms(x, g_ref[...]).astype(BF16), wg_ref, wu_ref, wd_ref)


def _ffn(x, g, w_gu, w_down, tm):
    m, d = x.shape
    resident = pl.Buffered(1)
    return pl.pallas_call(
        _ffn_kernel,
        grid=(m // tm,),
        in_specs=[
            pl.BlockSpec((tm, d), lambda i: (i, 0)),
            pl.BlockSpec((1, d), lambda i: (0, 0)),
            pl.BlockSpec((d, D_FF), lambda i: (0, 0), pipeline_mode=resident),
            pl.BlockSpec((d, D_FF), lambda i: (0, 1), pipeline_mode=resident),
            pl.BlockSpec((D_FF, d), lambda i: (0, 0), pipeline_mode=resident),
        ],
        out_specs=pl.BlockSpec((tm, d), lambda i: (i, 0)),
        out_shape=jax.ShapeDtypeStruct((m, d), F32),
        compiler_params=_cparams(("parallel",)),
        name="ffn",
    )(x, g.reshape(1, d), w_gu, w_gu, w_down)


ROUTE_E1, ROUTE_E2, ROUTE_R1, ROUTE_R2, ROUTE_G1, ROUTE_G2 = range(6)
ROUTE_FIELDS = 8


def _route_kernel(x_ref, g_ref, wr_ref, tril_ref, route_ref, routet_ref, cnt_ref, carry_scr):
    @pl.when(pl.program_id(0) == 0)
    def _():
        carry_scr[...] = jnp.zeros_like(carry_scr)

    n = _rms(x_ref[...], g_ref[...])
    logits = jnp.dot(n, wr_ref[...], preferred_element_type=F32, precision=lax.Precision.HIGHEST)
    lane = lax.broadcasted_iota(jnp.int32, logits.shape, 1)
    neg = jnp.float32(-jnp.inf)
    lg = jnp.where(lane < N_EXPERTS, logits, neg)
    m1 = jnp.max(lg, axis=-1, keepdims=True)
    i1 = jnp.min(jnp.where(lg == m1, lane, LANES), axis=-1, keepdims=True)
    lg2 = jnp.where(lane == i1, neg, lg)
    m2 = jnp.max(lg2, axis=-1, keepdims=True)
    i2 = jnp.min(jnp.where(lg2 == m2, lane, LANES), axis=-1, keepdims=True)
    e2 = jnp.exp(m2 - m1)
    g1 = 1.0 / (1.0 + e2)
    g2 = e2 / (1.0 + e2)
    chosen = jnp.where(lane == i1, 1.0, jnp.where(lane == i2, 1.0, 0.0))
    before = _dot(tril_ref[...], chosen.astype(BF16)) + carry_scr[...]
    r1 = jnp.sum(jnp.where(lane == i1, before, 0.0), axis=-1, keepdims=True)
    r2 = jnp.sum(jnp.where(lane == i2, before, 0.0), axis=-1, keepdims=True)
    carry_scr[...] += jnp.sum(chosen, axis=0, keepdims=True)
    cnt_ref[...] = carry_scr[...]
    fields = (i1.astype(F32), i2.astype(F32), r1, r2, g1, g2)
    route = jnp.zeros(logits.shape, F32)
    for idx, val in enumerate(fields):
        route = jnp.where(lane == idx, val, route)
    route_ref[...] = route
    routet_ref[...] = jnp.transpose(route)[:ROUTE_FIELDS]


def _route(x, g, wr_pad, tm):
    m, d = x.shape
    idx = jnp.arange(tm)
    tril = (idx[:, None] > idx[None, :]).astype(BF16)
    return pl.pallas_call(
        _route_kernel,
        grid=(m // tm,),
        in_specs=[
            pl.BlockSpec((tm, d), lambda i: (i, 0)),
            pl.BlockSpec((1, d), lambda i: (0, 0)),
            pl.BlockSpec((d, LANES), lambda i: (0, 0)),
            pl.BlockSpec((tm, tm), lambda i: (0, 0)),
        ],
        out_specs=[pl.BlockSpec((tm, LANES), lambda i: (i, 0)),
                   pl.BlockSpec((ROUTE_FIELDS, tm), lambda i: (0, i)),
                   pl.BlockSpec((1, LANES), lambda i: (0, 0))],
        out_shape=[jax.ShapeDtypeStruct((m, LANES), F32),
                   jax.ShapeDtypeStruct((ROUTE_FIELDS, m), F32),
                   jax.ShapeDtypeStruct((1, LANES), F32)],
        scratch_shapes=[pltpu.VMEM((1, LANES), F32)],
        compiler_params=_cparams(("arbitrary",)),
        name="route",
    )(x, g.reshape(1, d), wr_pad, tril)


def _row_copy(src, src_row, dst, dst_row, sem):
    return pltpu.make_async_copy(src.at[pl.ds(src_row, 1)], dst.at[pl.ds(dst_row, 1)], sem)


ROW_LOOP_UNROLL = 8


def _row_loop(tm, body):
    def step(i, carry):
        body(i)
        return carry
    lax.fori_loop(0, tm, step, 0, unroll=ROW_LOOP_UNROLL)


def _dispatch_kernel(p1_ref, p2_ref, x_ref, g_ref, init_ref, xs_ref, n_scr, sem):
    del init_ref
    tm = x_ref.shape[0]
    step = pl.program_id(0)

    def copies(s, i):
        slot = s % 2
        tok = s * tm + i
        return (_row_copy(n_scr.at[slot], i, xs_ref, p1_ref[tok], sem.at[slot]),
                _row_copy(n_scr.at[slot], i, xs_ref, p2_ref[tok], sem.at[slot]))

    def start(s):
        def body(i):
            first, second = copies(s, i)
            first.start()
            second.start(priority=1)
        _row_loop(tm, body)

    def wait(s):
        def body(i):
            first, second = copies(s, i)
            first.wait()
            second.wait()
        _row_loop(tm, body)

    n_scr[step % 2] = _rms(x_ref[...], g_ref[...])
    start(step)

    @pl.when(step > 0)
    def _():
        wait(step - 1)

    @pl.when(step == pl.num_programs(0) - 1)
    def _():
        wait(step)


def _dispatch(x, g, pos1, pos2, rows, tm):
    m, d = x.shape
    return pl.pallas_call(
        _dispatch_kernel,
        grid_spec=pltpu.PrefetchScalarGridSpec(
            num_scalar_prefetch=2,
            grid=(m // tm,),
            in_specs=[
                pl.BlockSpec((tm, d), lambda i, p1, p2: (i, 0)),
                pl.BlockSpec((1, d), lambda i, p1, p2: (0, 0)),
                pl.BlockSpec(memory_space=pl.ANY),
            ],
            out_specs=pl.BlockSpec(memory_space=pl.ANY),
            scratch_shapes=[pltpu.VMEM((2, tm, d), F32), pltpu.SemaphoreType.DMA((2,))],
        ),
        out_shape=jax.ShapeDtypeStruct((rows, d), F32),
        input_output_aliases={4: 0},
        compiler_params=_cparams(("arbitrary",)),
        name="dispatch",
    )(pos1, pos2, x, g.reshape(1, d), jnp.zeros((rows, d), F32))


def _experts_kernel(te_ref, nv_ref, x_ref, wg_ref, wu_ref, wd_ref, y_ref):
    del te_ref
    valid = pl.program_id(0) < nv_ref[0]

    @pl.when(valid)
    def _():
        y_ref[...] = _swiglu(x_ref[...].astype(BF16), wg_ref, wu_ref, wd_ref)

    @pl.when(jnp.logical_not(valid))
    def _():
        y_ref[...] = jnp.zeros_like(y_ref)


def _experts(xs, tile_expert, n_valid, w_gu, w_down, tm):
    rows, d = xs.shape
    return pl.pallas_call(
        _experts_kernel,
        grid_spec=pltpu.PrefetchScalarGridSpec(
            num_scalar_prefetch=2,
            grid=(rows // tm,),
            in_specs=[
                pl.BlockSpec((tm, d), lambda t, te, nv: (t, 0)),
                pl.BlockSpec((None, d, D_FF), lambda t, te, nv: (te[t], 0, 0)),
                pl.BlockSpec((None, d, D_FF), lambda t, te, nv: (te[t], 0, 1)),
                pl.BlockSpec((None, D_FF, d), lambda t, te, nv: (te[t], 0, 0)),
            ],
            out_specs=pl.BlockSpec((tm, d), lambda t, te, nv: (t, 0)),
        ),
        out_shape=jax.ShapeDtypeStruct((rows, d), F32),
        compiler_params=_cparams(("arbitrary",)),
        name="experts",
    )(tile_expert, n_valid, xs, w_gu, w_gu, w_down)


def _combine_kernel(p1_ref, p2_ref, x_ref, route_ref, fg_ref, ys_ref, out_ref, buf, sem, *, final):
    tm = x_ref.shape[0]
    step = pl.program_id(0)

    def copies(s, i):
        slot = s % 2
        tok = s * tm + i
        return (_row_copy(ys_ref, p1_ref[tok], buf.at[slot, 0], i, sem.at[slot]),
                _row_copy(ys_ref, p2_ref[tok], buf.at[slot, 1], i, sem.at[slot]))

    def start(s):
        def body(i):
            first, second = copies(s, i)
            first.start()
            second.start(priority=1)
        _row_loop(tm, body)

    def wait(s):
        def body(i):
            first, second = copies(s, i)
            first.wait()
            second.wait()
        _row_loop(tm, body)

    @pl.when(step == 0)
    def _():
        start(step)

    @pl.when(step + 1 < pl.num_programs(0))
    def _():
        start(step + 1)

    wait(step)
    rows = buf[step % 2]
    route = route_ref[...]
    g1 = route[:, ROUTE_G1:ROUTE_G1 + 1]
    g2 = route[:, ROUTE_G2:ROUTE_G2 + 1]
    out = x_ref[...] + g1 * rows[0] + g2 * rows[1]
    out_ref[...] = _rms(out, fg_ref[...]) if final else out


def _combine(x, route, ys, pos1, pos2, final_g, tm):
    m, d = x.shape
    final = final_g is not None
    fg = final_g.reshape(1, d) if final else jnp.ones((1, d), F32)
    return pl.pallas_call(
        functools.partial(_combine_kernel, final=final),
        grid_spec=pltpu.PrefetchScalarGridSpec(
            num_scalar_prefetch=2,
            grid=(m // tm,),
            in_specs=[
                pl.BlockSpec((tm, d), lambda i, p1, p2: (i, 0)),
                pl.BlockSpec((tm, LANES), lambda i, p1, p2: (i, 0)),
                pl.BlockSpec((1, d), lambda i, p1, p2: (0, 0)),
                pl.BlockSpec(memory_space=pl.ANY),
            ],
            out_specs=pl.BlockSpec((tm, d), lambda i, p1, p2: (i, 0)),
            scratch_shapes=[pltpu.VMEM((2, 2, tm, d), F32), pltpu.SemaphoreType.DMA((2,))],
        ),
        out_shape=jax.ShapeDtypeStruct((m, d), F32),
        compiler_params=_cparams(("arbitrary",)),
        name="combine",
    )(pos1, pos2, x, route, fg, ys)


def _moe(x, g, w_router, w_gu, w_down, final_g, tm_route, tm_rows, tm_expert):
    m, d = x.shape
    wr_pad = jnp.zeros((d, LANES), F32).at[:, :N_EXPERTS].set(w_router)
    route, route_t, counts = _route(x, g, wr_pad, tm_route)
    counts = counts[0, :N_EXPERTS].astype(jnp.int32)
    tiles = (counts + tm_expert - 1) // tm_expert
    experts = jnp.arange(N_EXPERTS, dtype=jnp.int32)
    tile_end = jnp.sum(jnp.where(experts[None, :] <= experts[:, None], tiles[None, :], 0), axis=1)
    row_start = (tile_end - tiles) * tm_expert

    def slot(e_row, r_row):
        e = route_t[e_row].astype(jnp.int32)
        start = jnp.sum(jnp.where(e[:, None] == experts[None, :], row_start[None, :], 0), axis=1)
        return start + route_t[r_row].astype(jnp.int32)

    pos1 = slot(ROUTE_E1, ROUTE_R1)
    pos2 = slot(ROUTE_E2, ROUTE_R2)
    n_tiles = 2 * m // tm_expert + N_EXPERTS
    tile_ids = jnp.arange(n_tiles, dtype=jnp.int32)
    tile_expert = jnp.minimum(
        jnp.sum((tile_ids[:, None] >= tile_end[None, :]).astype(jnp.int32), axis=1), N_EXPERTS - 1)
    n_valid = tile_end[-1:]
    xs = _dispatch(x, g, pos1, pos2, n_tiles * tm_expert, tm_rows)
    ys = _experts(xs, tile_expert, n_valid, w_gu, w_down, tm_expert)
    return _combine(x, route, ys, pos1, pos2, final_g, tm_rows)


def _final_norm_kernel(x_ref, g_ref, o_ref):
    o_ref[...] = _rms(x_ref[...], g_ref[...])


def _final_norm(x, g, tm):
    m, d = x.shape
    return pl.pallas_call(
        _final_norm_kernel,
        grid=(m // tm,),
        in_specs=[pl.BlockSpec((tm, d), lambda i: (i, 0)), pl.BlockSpec((1, d), lambda i: (0, 0))],
        out_specs=pl.BlockSpec((tm, d), lambda i: (i, 0)),
        out_shape=jax.ShapeDtypeStruct((m, d), F32),
        compiler_params=_cparams(("parallel",)),
        name="final_norm",
    )(x, g.reshape(1, d))


def _gelu(x):
    return 0.5 * x * (1.0 + lax.erf(x * (1.0 / math.sqrt(2.0))))


def _gmlp_kernel(u_ref, v_ref, lng_ref, lnb_ref, ws_ref, bias_ref, y_ref):
    tt = u_ref.shape[1]
    v = _gelu(v_ref[0])
    mu = jnp.mean(v, axis=-1, keepdims=True)
    d = v - mu
    var = jnp.mean(d * d, axis=-1, keepdims=True)
    vn = d * lax.rsqrt(var + LN_EPS) * lng_ref[...] + lnb_ref[...]
    row = lax.broadcasted_iota(jnp.int32, (GCHUNK, GCHUNK), 0)
    col = lax.broadcasted_iota(jnp.int32, (GCHUNK, GCHUNK), 1)
    causal = row >= col
    first_head = lax.broadcasted_iota(jnp.int32, (1, LANES), 1) < HEAD
    ws = [jnp.where(causal, ws_ref[gi], 0.0).astype(BF16) for gi in range(MIX_HEADS)]
    for c in range(tt // GCHUNK):
        rows = slice(c * GCHUNK, (c + 1) * GCHUNK)
        outs = []
        for j in range(PAIRS):
            vp = vn[rows, j * LANES:(j + 1) * LANES].astype(BF16)
            outs.append(jnp.where(first_head, _dot(ws[2 * j], vp), _dot(ws[2 * j + 1], vp)))
        mixed = jnp.concatenate(outs, axis=1) + bias_ref[...]
        y_ref[0, rows, :] = _gelu(u_ref[0, rows, :]) * mixed


def _gmlp(p3, ln_g, ln_b, w_s, bias, tt):
    b, t, _ = p3.shape
    return pl.pallas_call(
        _gmlp_kernel,
        grid=(b, t // tt),
        in_specs=[
            pl.BlockSpec((1, tt, MIX), lambda i, j: (i, j, 0)),
            pl.BlockSpec((1, tt, MIX), lambda i, j: (i, j, 1)),
            pl.BlockSpec((1, MIX), lambda i, j: (0, 0)),
            pl.BlockSpec((1, MIX), lambda i, j: (0, 0)),
            pl.BlockSpec((MIX_HEADS, GCHUNK, GCHUNK), lambda i, j: (0, 0, 0)),
            pl.BlockSpec((GCHUNK, MIX), lambda i, j: (0, 0)),
        ],
        out_specs=pl.BlockSpec((1, tt, MIX), lambda i, j: (i, j, 0)),
        out_shape=jax.ShapeDtypeStruct((b, t, MIX), F32),
        compiler_params=_cparams(("parallel", "parallel")),
        name="gmlp",
    )(p3, p3, ln_g.reshape(1, MIX), ln_b.reshape(1, MIX), w_s, bias)


def _token_shift(cur, prev_scr, mu):
    rows = cur.shape[0]
    row = lax.broadcasted_iota(jnp.int32, cur.shape, 0)
    prev = jnp.where(row == 0, prev_scr[...], pltpu.roll(cur, 1, axis=0))
    prev_scr[...] = cur[rows - 1:rows, :]
    return cur + (prev - cur) * mu


def _rwkv_kernel(r_ref, k_ref, v_ref, l_ref,
                 mur_ref, muk_ref, muv_ref, mul_ref, wcat_ref,
                 w0_ref, a0_ref, kk_ref, ka_ref, rk_ref, lnw_ref, lnb_ref,
                 jmat_ref, tril_ref, same_ref,
                 y_ref,
                 pr_scr, pk_scr, pv_scr, pl_scr, state_scr,
                 at_scr, bt_scr, kt_scr, rt_scr, bp_scr, kp_scr, v_scr, pe_scr,
                 rk_scr, g_scr, y_scr):
    tt = r_ref.shape[1]
    L = SCAN_CHUNK

    @pl.when(pl.program_id(1) == 0)
    def _():
        pr_scr[...] = jnp.zeros_like(pr_scr)
        pk_scr[...] = jnp.zeros_like(pk_scr)
        pv_scr[...] = jnp.zeros_like(pv_scr)
        pl_scr[...] = jnp.zeros_like(pl_scr)
        state_scr[...] = jnp.zeros_like(state_scr)

    r = _token_shift(r_ref[0], pr_scr, mur_ref[...])
    k = _token_shift(k_ref[0], pk_scr, muk_ref[...])
    v = _token_shift(v_ref[0], pv_scr, muv_ref[...])
    lo = _token_shift(l_ref[0], pl_scr, mul_ref[...])
    llane = lax.broadcasted_iota(jnp.int32, lo.shape, 1)
    z = jnp.where(llane < 64, jnp.tanh(lo), jnp.where(llane < 128, lo, jax.nn.sigmoid(lo)))
    proj = _dot(z.astype(BF16), wcat_ref[...])
    w = -jax.nn.softplus(-(w0_ref[...] + proj[:, :MIX])) - 0.5
    a = jax.nn.sigmoid(a0_ref[...] + proj[:, MIX:2 * MIX])
    jmat = jmat_ref[...]
    kk = k * kk_ref[...]
    kk_sq = kk * kk
    ss = jnp.concatenate(
        [_dot_split(kk_sq[:, j * LANES:(j + 1) * LANES], jmat) for j in range(PAIRS)], axis=1)
    kk = kk / jnp.maximum(jnp.sqrt(ss), 1e-12)
    k = k * (1.0 + (a - 1.0) * ka_ref[...])
    b = kk * a
    ld = -jnp.exp(w)
    cum = _split_dot(tril_ref[...], ld)
    cum_end = _split_dot(same_ref[...], ld)
    p_inv = jnp.exp(-cum)
    to_end = jnp.exp(cum_end - cum)
    at_scr[...] = -(kk * jnp.exp(cum - ld))
    bt_scr[...] = b * p_inv
    kt_scr[...] = k * p_inv
    rt_scr[...] = r * jnp.exp(cum)
    bp_scr[...] = b * to_end
    kp_scr[...] = k * to_end
    v_scr[...] = v
    pe_scr[...] = jnp.exp(cum_end)
    rk_scr[...] = r * k * rk_ref[...]
    g_scr[...] = proj[:, 2 * MIX:]

    lane = lax.broadcasted_iota(jnp.int32, (1, LANES), 1)
    h0 = lane < HEAD
    row = lax.broadcasted_iota(jnp.int32, (L, LANES), 0)
    col = lax.broadcasted_iota(jnp.int32, (L, LANES), 1) % HEAD
    strict = row > col
    incl = row >= col
    srow = lax.broadcasted_iota(jnp.int32, (LANES, LANES), 0) < HEAD
    scol = lax.broadcasted_iota(jnp.int32, (LANES, LANES), 1) < HEAD
    same_head = srow == scol
    pairs = range(PAIRS)

    def cat(x, y):
        return jnp.concatenate([x, y], axis=0)

    def first(x):
        return jnp.where(h0, x, 0.0)

    def second(x):
        return jnp.where(h0, 0.0, x)

    def chunk_body(c, carry):
        start = pl.multiple_of(c * L, L)
        rows = pl.ds(start, L)
        ld_tile = lambda ref: [ref[rows, j * LANES:(j + 1) * LANES] for j in pairs]
        at, bt, kt, rt = ld_tile(at_scr), ld_tile(bt_scr), ld_tile(kt_scr), ld_tile(rt_scr)
        bp, kp, vc = ld_tile(bp_scr), ld_tile(kp_scr), ld_tile(v_scr)
        pe = [pe_scr[pl.ds(start, 1), j * LANES:(j + 1) * LANES] for j in pairs]
        s = [state_scr[j] for j in pairs]
        g_0 = [_dot_nt(cat(first(at[j]), first(rt[j])).astype(BF16), cat(bt[j], kt[j]).astype(BF16))
               for j in pairs]
        g_1 = [_dot_nt(cat(second(at[j]), second(rt[j])).astype(BF16), cat(kt[j], bt[j]).astype(BF16))
               for j in pairs]
        ars = [_dot_nt(cat(at[j], rt[j]).astype(BF16), s[j].astype(BF16)) for j in pairs]
        top0 = [jnp.where(strict, g_0[j][:L], 0.0) for j in pairs]
        top1 = [jnp.where(strict, g_1[j][:L], 0.0) for j in pairs]
        vv = [cat(vc[j], vc[j]).astype(BF16) for j in pairs]
        akv0 = [_dot(second(top0[j]).astype(BF16), vv[j]) for j in pairs]
        akv1 = [_dot(first(top1[j]).astype(BF16), vv[j]) for j in pairs]
        rhs = [ars[j][:L] + jnp.where(h0, akv0[j], akv1[j]) for j in pairs]
        x = [cat(first(rhs[j]), second(rhs[j])) for j in pairs]
        apow = [cat(first(top0[j]), second(top1[j])).astype(BF16) for j in pairs]
        n_steps = 6
        for i in range(n_steps):
            x = [x[j] + _dot(apow[j], x[j].astype(BF16)) for j in pairs]
            if i + 1 < n_steps:
                apow = [_dot(apow[j], apow[j]).astype(BF16) for j in pairs]
        u = [x[j][:L] + x[j][L:] for j in pairs]
        uv = [cat(u[j], vc[j]).astype(BF16) for j in pairs]
        vu = [cat(vc[j], u[j]).astype(BF16) for j in pairs]
        y0 = [_dot(jnp.where(incl, g_0[j][L:], 0.0).astype(BF16), uv[j]) for j in pairs]
        y1 = [_dot(jnp.where(incl, g_1[j][L:], 0.0).astype(BF16), vu[j]) for j in pairs]
        upd = [_dot_tn(uv[j], cat(bp[j], kp[j]).astype(BF16)) for j in pairs]
        for j in pairs:
            y_scr[rows, j * LANES:(j + 1) * LANES] = ars[j][L:] + jnp.where(h0, y0[j], y1[j])
            state_scr[j] = s[j] * pe[j] + jnp.where(same_head, upd[j], 0.0)
        return carry

    lax.fori_loop(0, tt // L, chunk_body, 0)

    outs = []
    for j in pairs:
        cols = slice(j * LANES, (j + 1) * LANES)
        y = y_scr[:, cols]
        mean = _dot_split(y, jmat) * (1.0 / HEAD)
        d = y - mean
        var = _dot_split(d * d, jmat) * (1.0 / HEAD)
        yn = d * lax.rsqrt(var + GN_EPS) * lnw_ref[:, cols] + lnb_ref[:, cols]
        rk = _dot_split(rk_scr[:, cols], jmat)
        outs.append((yn + rk * v_scr[:, cols]) * g_scr[:, cols])
    y_ref[0] = jnp.concatenate(outs, axis=1)


def _rwkv(p3, mu, wcat, w0, a0, k_k, k_a, r_k, lnx_w, lnx_b, tt):
    b, t, _ = p3.shape
    row = lambda x: x.reshape(1, -1)
    head_of = jnp.arange(LANES) // HEAD
    jmat = (head_of[:, None] == head_of[None, :]).astype(BF16)
    idx = jnp.arange(tt)
    same = (idx[:, None] // SCAN_CHUNK) == (idx[None, :] // SCAN_CHUNK)
    tril = (same & (idx[:, None] >= idx[None, :])).astype(BF16)
    const = lambda shape: pl.BlockSpec(shape, lambda i, j: (0,) * len(shape))
    tile = pltpu.VMEM((tt, MIX), F32)
    return pl.pallas_call(
        _rwkv_kernel,
        grid=(b, t // tt),
        in_specs=[
            pl.BlockSpec((1, tt, MIX), lambda i, j: (i, j, 0)),
            pl.BlockSpec((1, tt, MIX), lambda i, j: (i, j, 1)),
            pl.BlockSpec((1, tt, MIX), lambda i, j: (i, j, 2)),
            pl.BlockSpec((1, tt, LORA_W), lambda i, j: (i, j, 3 * MIX // LORA_W)),
            const((1, MIX)), const((1, MIX)), const((1, MIX)), const((1, LORA_W)),
            const((LORA_W, 3 * MIX)),
            const((1, MIX)), const((1, MIX)), const((1, MIX)), const((1, MIX)),
            const((1, MIX)), const((1, MIX)), const((1, MIX)),
            const((LANES, LANES)), const((tt, tt)), const((tt, tt)),
        ],
        out_specs=pl.BlockSpec((1, tt, MIX), lambda i, j: (i, j, 0)),
        out_shape=jax.ShapeDtypeStruct((b, t, MIX), F32),
        scratch_shapes=[
            pltpu.VMEM((1, MIX), F32), pltpu.VMEM((1, MIX), F32), pltpu.VMEM((1, MIX), F32),
            pltpu.VMEM((1, LORA_W), F32),
            pltpu.VMEM((PAIRS, LANES, LANES), F32),
        ] + [tile] * 11,
        compiler_params=_cparams(("parallel", "arbitrary")),
        name="rwkv7",
    )(p3, p3, p3, p3,
      row(mu[:MIX]), row(mu[MIX:2 * MIX]), row(mu[2 * MIX:3 * MIX]), row(mu[3 * MIX:]),
      wcat, row(w0), row(a0), row(k_k), row(k_a), row(r_k), row(lnx_w), row(lnx_b),
      jmat, tril, same.astype(BF16))


def kernel(x, mem, mem_norm_g, norm1_g, w_kv_mem, w_out, norm2_g, rwkv_w_in, rwkv_mu, rwkv_w0, rwkv_w2, rwkv_a0, rwkv_a2, rwkv_g2, rwkv_k_k, rwkv_k_a, rwkv_r_k, rwkv_lnx_w, rwkv_lnx_b, ffn_w_gu, ffn_w_down, gmlp_w_in, gmlp_v_ln_g, gmlp_v_ln_b, gmlp_w_s, gmlp_b_s, moe_router, moe_w_gu, moe_w_down, final_norm_g):
    b, t, d = x.shape
    n_tok = b * t
    depth = norm1_g.shape[0]
    xs = x.reshape(n_tok, d)
    mem2 = mem.reshape(b * N_MEM, d)
    for i in range(depth):
        j = i // 2
        kv = _norm_matmul(mem2, mem_norm_g, w_kv_mem[i].astype(BF16), 512).reshape(b, N_MEM, 2 * MEMW)
        if i % 2 == 0:
            p = _norm_matmul(xs, norm1_g[i], rwkv_w_in[j].astype(BF16), 512)
            p3 = p.reshape(b, t, RWKV_COLS + MEMW)
            wcat = jnp.zeros((LORA_W, 3 * MIX), F32)
            wcat = wcat.at[0:64, 0:MIX].set(rwkv_w2[j])
            wcat = wcat.at[64:128, MIX:2 * MIX].set(rwkv_a2[j])
            wcat = wcat.at[128:256, 2 * MIX:].set(rwkv_g2[j])
            y = _rwkv(p3, rwkv_mu[j], wcat.astype(BF16), rwkv_w0[j], rwkv_a0[j], rwkv_k_k[j],
                      rwkv_k_a[j], rwkv_r_k[j].reshape(MIX), rwkv_lnx_w[j], rwkv_lnx_b[j], 256)
            q_block = RWKV_COLS // MEMW
        else:
            p = _norm_matmul(xs, norm1_g[i], gmlp_w_in[j].astype(BF16), 512)
            p3 = p.reshape(b, t, 2 * MIX + MEMW)
            bias = jnp.repeat(jnp.transpose(gmlp_b_s[j]), HEAD, axis=1)
            y = _gmlp(p3, gmlp_v_ln_g[j], gmlp_v_ln_b[j], gmlp_w_s[j], bias, 256)
            q_block = 2 * MIX // MEMW
        o = _mem_attn(p3, q_block, kv, 512)
        wo = w_out[i].astype(BF16)
        xs = _mixer_out(xs, y.reshape(n_tok, MIX), o.reshape(n_tok, MEMW), wo[:MIX], wo[MIX:], 512)
        if i % 2 == 0:
            xs = _ffn(xs, norm2_g[i], ffn_w_gu[j].astype(BF16), ffn_w_down[j].astype(BF16), 512)
        else:
            last = i == depth - 1
            xs = _moe(xs, norm2_g[i], moe_router[j], moe_w_gu[j].astype(BF16),
                      moe_w_down[j].astype(BF16), final_norm_g if last else None,
                      512, 256, 512)
    if depth % 2 == 1:
        xs = _final_norm(xs, final_norm_g, 512)
    return xs.reshape(b, t, d)
```

```python
import functools
import math

import jax
import jax.numpy as jnp
from jax import lax
from jax.experimental import pallas as pl
from jax.experimental.pallas import tpu as pltpu

F32 = jnp.float32
BF16 = jnp.bfloat16

D_MODEL = 1024
HEAD = 64
MIX = 768
MIX_HEADS = MIX // HEAD
MEMW = 256
MEM_HEADS = 4
N_MEM = 256
LORA_W = 256
RWKV_COLS = 3 * MIX + LORA_W
GCHUNK = 128
D_FF = 2816
N_EXPERTS = 8
RMS_EPS = 1e-6
GN_EPS = 64e-5
LN_EPS = 1e-5

LANES = 128
PAIRS = MIX // LANES
SCAN_CHUNK = 64
PREPARE_CHUNKS = 2
VMEM_LIMIT = 56 * 1024 * 1024

TOKEN_TILE = 512
RWKV_TILE = 512
GMLP_TILE = 256
ROW_COPY_TILE = 256
EXPERT_TILE = 512


def _cparams(sem):
    return pltpu.CompilerParams(dimension_semantics=sem, vmem_limit_bytes=VMEM_LIMIT)


def _rms(x, g):
    ms = jnp.mean(x * x, axis=-1, keepdims=True)
    return x * lax.rsqrt(ms + RMS_EPS) * g


def _dot(a, b):
    return jnp.dot(a, b, preferred_element_type=F32)


def _dot_nt(a, b):
    return lax.dot_general(a, b, (((1,), (1,)), ((), ())), preferred_element_type=F32)


def _dot_tn(a, b):
    return lax.dot_general(a, b, (((0,), (0,)), ((), ())), preferred_element_type=F32)


def _dot_split(x, m2_bf16):
    hi = x.astype(BF16)
    lo = (x - hi.astype(F32)).astype(BF16)
    return _dot(jnp.concatenate([hi, lo], axis=1), m2_bf16)


def _split_dot(m2_bf16, x):
    hi = x.astype(BF16)
    lo = (x - hi.astype(F32)).astype(BF16)
    return _dot(m2_bf16, jnp.concatenate([hi, lo], axis=0))


def _norm_matmul_kernel(x_ref, g_ref, w_ref, o_ref):
    n = _rms(x_ref[...], g_ref[...]).astype(BF16)
    o_ref[...] = _dot(n, w_ref[...])


def _norm_matmul(x, g, w, tm):
    m, k = x.shape
    n = w.shape[1]
    return pl.pallas_call(
        _norm_matmul_kernel,
        grid=(m // tm,),
        in_specs=[
            pl.BlockSpec((tm, k), lambda i: (i, 0)),
            pl.BlockSpec((1, k), lambda i: (0, 0)),
            pl.BlockSpec((k, n), lambda i: (0, 0)),
        ],
        out_specs=pl.BlockSpec((tm, n), lambda i: (i, 0)),
        out_shape=jax.ShapeDtypeStruct((m, n), F32),
        compiler_params=_cparams(("parallel",)),
        name="norm_matmul",
    )(x, g.reshape(1, k), w)


def _mem_attn_kernel(q_ref, kv_ref, o_ref):
    q = q_ref[0]
    kv = kv_ref[0]
    k = kv[:, :MEMW].astype(BF16)
    v = kv[:, MEMW:].astype(BF16)
    lane = lax.broadcasted_iota(jnp.int32, (1, MEMW), 1)
    acc = jnp.zeros(q.shape, F32)
    for h in range(MEM_HEADS):
        m = (lane >= h * HEAD) & (lane < (h + 1) * HEAD)
        qh = jnp.where(m, q, 0.0).astype(BF16)
        s = _dot_nt(qh, k) * (1.0 / math.sqrt(HEAD))
        s = s - jnp.max(s, axis=-1, keepdims=True)
        p = jnp.exp(s)
        l = jnp.sum(p, axis=-1, keepdims=True)
        pv = _dot(p.astype(BF16), v)
        acc = acc + jnp.where(m, pv / l, 0.0)
    o_ref[0] = acc


def _mem_attn(p3, q_block, kv3, tt):
    b, t, _ = p3.shape
    return pl.pallas_call(
        _mem_attn_kernel,
        grid=(b, t // tt),
        in_specs=[
            pl.BlockSpec((1, tt, MEMW), lambda i, j: (i, j, q_block)),
            pl.BlockSpec((1, N_MEM, 2 * MEMW), lambda i, j: (i, 0, 0)),
        ],
        out_specs=pl.BlockSpec((1, tt, MEMW), lambda i, j: (i, j, 0)),
        out_shape=jax.ShapeDtypeStruct((b, t, MEMW), F32),
        compiler_params=_cparams(("parallel", "parallel")),
        name="mem_attn",
    )(p3, kv3)


def _mixer_out_kernel(x_ref, y_ref, o_ref, wy_ref, wo_ref, out_ref):
    out_ref[...] = (x_ref[...]
                    + _dot(y_ref[...].astype(BF16), wy_ref[...])
                    + _dot(o_ref[...].astype(BF16), wo_ref[...]))


def _mixer_out(x, y, o, wy, wo, tm):
    m, d = x.shape
    return pl.pallas_call(
        _mixer_out_kernel,
        grid=(m // tm,),
        in_specs=[
            pl.BlockSpec((tm, d), lambda i: (i, 0)),
            pl.BlockSpec((tm, MIX), lambda i: (i, 0)),
            pl.BlockSpec((tm, MEMW), lambda i: (i, 0)),
            pl.BlockSpec((MIX, d), lambda i: (0, 0)),
            pl.BlockSpec((MEMW, d), lambda i: (0, 0)),
        ],
        out_specs=pl.BlockSpec((tm, d), lambda i: (i, 0)),
        out_shape=jax.ShapeDtypeStruct((m, d), F32),
        compiler_params=_cparams(("parallel",)),
        name="mixer_out",
    )(x, y, o, wy, wo)


def _swiglu(n, wg_ref, wu_ref, wd_ref):
    gate = _dot(n, wg_ref[...])
    up = _dot(n, wu_ref[...])
    h = gate * jax.nn.sigmoid(gate) * up
    return _dot(h.astype(BF16), wd_ref[...])


def _ffn_kernel(x_ref, g_ref, wg_ref, wu_ref, wd_ref, out_ref):
    x = x_ref[...]
    out_ref[...] = x + _swiglu(_rms(x, g_ref[...]).astype(BF16), wg_ref, wu_ref, wd_ref)


def _ffn(x, g, w_gu, w_down, tm):
    m, d = x.shape
    resident = pl.Buffered(1)
    return pl.pallas_call(
        _ffn_kernel,
        grid=(m // tm,),
        in_specs=[
            pl.BlockSpec((tm, d), lambda i: (i, 0)),
            pl.BlockSpec((1, d), lambda i: (0, 0)),
            pl.BlockSpec((d, D_FF), lambda i: (0, 0), pipeline_mode=resident),
            pl.BlockSpec((d, D_FF), lambda i: (0, 1), pipeline_mode=resident),
            pl.BlockSpec((D_FF, d), lambda i: (0, 0), pipeline_mode=resident),
        ],
        out_specs=pl.BlockSpec((tm, d), lambda i: (i, 0)),
        out_shape=jax.ShapeDtypeStruct((m, d), F32),
        compiler_params=_cparams(("parallel",)),
        name="ffn",
    )(x, g.reshape(1, d), w_gu, w_gu, w_down)


ROUTE_E1, ROUTE_E2, ROUTE_R1, ROUTE_R2, ROUTE_G1, ROUTE_G2 = range(6)
ROUTE_FIELDS = 8


def _route_kernel(x_ref, g_ref, wr_ref, tril_ref, route_ref, routet_ref, cnt_ref, carry_scr):
    @pl.when(pl.program_id(0) == 0)
    def _():
        carry_scr[...] = jnp.zeros_like(carry_scr)

    n = _rms(x_ref[...], g_ref[...])
    logits = jnp.dot(n, wr_ref[...], preferred_element_type=F32, precision=lax.Precision.HIGHEST)
    lane = lax.broadcasted_iota(jnp.int32, logits.shape, 1)
    neg = jnp.float32(-jnp.inf)
    lg = jnp.where(lane < N_EXPERTS, logits, neg)
    m1 = jnp.max(lg, axis=-1, keepdims=True)
    i1 = jnp.min(jnp.where(lg == m1, lane, LANES), axis=-1, keepdims=True)
    lg2 = jnp.where(lane == i1, neg, lg)
    m2 = jnp.max(lg2, axis=-1, keepdims=True)
    i2 = jnp.min(jnp.where(lg2 == m2, lane, LANES), axis=-1, keepdims=True)
    e2 = jnp.exp(m2 - m1)
    g1 = 1.0 / (1.0 + e2)
    g2 = e2 / (1.0 + e2)
    chosen = jnp.where(lane == i1, 1.0, jnp.where(lane == i2, 1.0, 0.0))
    before = _dot(tril_ref[...], chosen.astype(BF16)) + carry_scr[...]
    r1 = jnp.sum(jnp.where(lane == i1, before, 0.0), axis=-1, keepdims=True)
    r2 = jnp.sum(jnp.where(lane == i2, before, 0.0), axis=-1, keepdims=True)
    carry_scr[...] += jnp.sum(chosen, axis=0, keepdims=True)
    cnt_ref[...] = carry_scr[...]
    fields = (i1.astype(F32), i2.astype(F32), r1, r2, g1, g2)
    route = jnp.zeros(logits.shape, F32)
    for idx, val in enumerate(fields):
        route = jnp.where(lane == idx, val, route)
    route_ref[...] = route
    routet_ref[...] = jnp.transpose(route)[:ROUTE_FIELDS]


def _route(x, g, wr_pad, tm):
    m, d = x.shape
    idx = jnp.arange(tm)
    tril = (idx[:, None] > idx[None, :]).astype(BF16)
    return pl.pallas_call(
        _route_kernel,
        grid=(m // tm,),
        in_specs=[
            pl.BlockSpec((tm, d), lambda i: (i, 0)),
            pl.BlockSpec((1, d), lambda i: (0, 0)),
            pl.BlockSpec((d, LANES), lambda i: (0, 0)),
            pl.BlockSpec((tm, tm), lambda i: (0, 0)),
        ],
        out_specs=[pl.BlockSpec((tm, LANES), lambda i: (i, 0)),
                   pl.BlockSpec((ROUTE_FIELDS, tm), lambda i: (0, i)),
                   pl.BlockSpec((1, LANES), lambda i: (0, 0))],
        out_shape=[jax.ShapeDtypeStruct((m, LANES), F32),
                   jax.ShapeDtypeStruct((ROUTE_FIELDS, m), F32),
                   jax.ShapeDtypeStruct((1, LANES), F32)],
        scratch_shapes=[pltpu.VMEM((1, LANES), F32)],
        compiler_params=_cparams(("arbitrary",)),
        name="route",
    )(x, g.reshape(1, d), wr_pad, tril)


def _row_copy(src, src_row, dst, dst_row, sem):
    return pltpu.make_async_copy(src.at[pl.ds(src_row, 1)], dst.at[pl.ds(dst_row, 1)], sem)


ROW_LOOP_UNROLL = 8


def _row_loop(tm, body):
    def step(i, carry):
        body(i)
        return carry
    lax.fori_loop(0, tm, step, 0, unroll=ROW_LOOP_UNROLL)


def _dispatch_kernel(p1_ref, p2_ref, x_ref, g_ref, init_ref, xs_ref, n_scr, sem):
    del init_ref
    tm = x_ref.shape[0]
    step = pl.program_id(0)

    def copies(s, i):
        slot = s % 2
        tok = s * tm + i
        return (_row_copy(n_scr.at[slot], i, xs_ref, p1_ref[tok], sem.at[slot]),
                _row_copy(n_scr.at[slot], i, xs_ref, p2_ref[tok], sem.at[slot]))

    def start(s):
        def body(i):
            first, second = copies(s, i)
            first.start()
            second.start(priority=1)
        _row_loop(tm, body)

    def wait(s):
        def body(i):
            first, second = copies(s, i)
            first.wait()
            second.wait()
        _row_loop(tm, body)

    n_scr[step % 2] = _rms(x_ref[...], g_ref[...])
    start(step)

    @pl.when(step > 0)
    def _():
        wait(step - 1)

    @pl.when(step == pl.num_programs(0) - 1)
    def _():
        wait(step)


def _dispatch(x, g, pos1, pos2, rows, tm):
    m, d = x.shape
    return pl.pallas_call(
        _dispatch_kernel,
        grid_spec=pltpu.PrefetchScalarGridSpec(
            num_scalar_prefetch=2,
            grid=(m // tm,),
            in_specs=[
                pl.BlockSpec((tm, d), lambda i, p1, p2: (i, 0)),
                pl.BlockSpec((1, d), lambda i, p1, p2: (0, 0)),
                pl.BlockSpec(memory_space=pl.ANY),
            ],
            out_specs=pl.BlockSpec(memory_space=pl.ANY),
            scratch_shapes=[pltpu.VMEM((2, tm, d), F32), pltpu.SemaphoreType.DMA((2,))],
        ),
        out_shape=jax.ShapeDtypeStruct((rows, d), F32),
        input_output_aliases={4: 0},
        compiler_params=_cparams(("arbitrary",)),
        name="dispatch",
    )(pos1, pos2, x, g.reshape(1, d), jnp.zeros((rows, d), F32))


def _experts_kernel(te_ref, nv_ref, x_ref, wg_ref, wu_ref, wd_ref, y_ref):
    del te_ref
    valid = pl.program_id(0) < nv_ref[0]

    @pl.when(valid)
    def _():
        y_ref[...] = _swiglu(x_ref[...].astype(BF16), wg_ref, wu_ref, wd_ref)

    @pl.when(jnp.logical_not(valid))
    def _():
        y_ref[...] = jnp.zeros_like(y_ref)


def _experts(xs, tile_expert, n_valid, w_gu, w_down, tm):
    rows, d = xs.shape
    return pl.pallas_call(
        _experts_kernel,
        grid_spec=pltpu.PrefetchScalarGridSpec(
            num_scalar_prefetch=2,
            grid=(rows // tm,),
            in_specs=[
                pl.BlockSpec((tm, d), lambda t, te, nv: (t, 0)),
                pl.BlockSpec((None, d, D_FF), lambda t, te, nv: (te[t], 0, 0)),
                pl.BlockSpec((None, d, D_FF), lambda t, te, nv: (te[t], 0, 1)),
                pl.BlockSpec((None, D_FF, d), lambda t, te, nv: (te[t], 0, 0)),
            ],
            out_specs=pl.BlockSpec((tm, d), lambda t, te, nv: (t, 0)),
        ),
        out_shape=jax.ShapeDtypeStruct((rows, d), F32),
        compiler_params=_cparams(("arbitrary",)),
        name="experts",
    )(tile_expert, n_valid, xs, w_gu, w_gu, w_down)


def _combine_kernel(p1_ref, p2_ref, x_ref, route_ref, fg_ref, ys_ref, out_ref, buf, sem, *, final):
    tm = x_ref.shape[0]
    step = pl.program_id(0)

    def copies(s, i):
        slot = s % 2
        tok = s * tm + i
        return (_row_copy(ys_ref, p1_ref[tok], buf.at[slot, 0], i, sem.at[slot]),
                _row_copy(ys_ref, p2_ref[tok], buf.at[slot, 1], i, sem.at[slot]))

    def start(s):
        def body(i):
            first, second = copies(s, i)
            first.start()
            second.start(priority=1)
        _row_loop(tm, body)

    def wait(s):
        def body(i):
            first, second = copies(s, i)
            first.wait()
            second.wait()
        _row_loop(tm, body)

    @pl.when(step == 0)
    def _():
        start(step)

    @pl.when(step + 1 < pl.num_programs(0))
    def _():
        start(step + 1)

    wait(step)
    rows = buf[step % 2]
    route = route_ref[...]
    g1 = route[:, ROUTE_G1:ROUTE_G1 + 1]
    g2 = route[:, ROUTE_G2:ROUTE_G2 + 1]
    out = x_ref[...] + g1 * rows[0] + g2 * rows[1]
    out_ref[...] = _rms(out, fg_ref[...]) if final else out


def _combine(x, route, ys, pos1, pos2, final_g, tm):
    m, d = x.shape
    final = final_g is not None
    fg = final_g.reshape(1, d) if final else jnp.ones((1, d), F32)
    return pl.pallas_call(
        functools.partial(_combine_kernel, final=final),
        grid_spec=pltpu.PrefetchScalarGridSpec(
            num_scalar_prefetch=2,
            grid=(m // tm,),
            in_specs=[
                pl.BlockSpec((tm, d), lambda i, p1, p2: (i, 0)),
                pl.BlockSpec((tm, LANES), lambda i, p1, p2: (i, 0)),
                pl.BlockSpec((1, d), lambda i, p1, p2: (0, 0)),
                pl.BlockSpec(memory_space=pl.ANY),
            ],
            out_specs=pl.BlockSpec((tm, d), lambda i, p1, p2: (i, 0)),
            scratch_shapes=[pltpu.VMEM((2, 2, tm, d), F32), pltpu.SemaphoreType.DMA((2,))],
        ),
        out_shape=jax.ShapeDtypeStruct((m, d), F32),
        compiler_params=_cparams(("arbitrary",)),
        name="combine",
    )(pos1, pos2, x, route, fg, ys)


def _moe(x, g, w_router, w_gu, w_down, final_g, tm_route, tm_rows, tm_expert):
    m, d = x.shape
    wr_pad = jnp.zeros((d, LANES), F32).at[:, :N_EXPERTS].set(w_router)
    route, route_t, counts = _route(x, g, wr_pad, tm_route)
    counts = counts[0, :N_EXPERTS].astype(jnp.int32)
    tiles = (counts + tm_expert - 1) // tm_expert
    experts = jnp.arange(N_EXPERTS, dtype=jnp.int32)
    tile_end = jnp.sum(jnp.where(experts[None, :] <= experts[:, None], tiles[None, :], 0), axis=1)
    row_start = (tile_end - tiles) * tm_expert

    def slot(e_row, r_row):
        e = route_t[e_row].astype(jnp.int32)
        start = jnp.sum(jnp.where(e[:, None] == experts[None, :], row_start[None, :], 0), axis=1)
        return start + route_t[r_row].astype(jnp.int32)

    pos1 = slot(ROUTE_E1, ROUTE_R1)
    pos2 = slot(ROUTE_E2, ROUTE_R2)
    n_tiles = 2 * m // tm_expert + N_EXPERTS
    tile_ids = jnp.arange(n_tiles, dtype=jnp.int32)
    tile_expert = jnp.minimum(
        jnp.sum((tile_ids[:, None] >= tile_end[None, :]).astype(jnp.int32), axis=1), N_EXPERTS - 1)
    n_valid = tile_end[-1:]
    xs = _dispatch(x, g, pos1, pos2, n_tiles * tm_expert, tm_rows)
    ys = _experts(xs, tile_expert, n_valid, w_gu, w_down, tm_expert)
    return _combine(x, route, ys, pos1, pos2, final_g, tm_rows)


def _final_norm_kernel(x_ref, g_ref, o_ref):
    o_ref[...] = _rms(x_ref[...], g_ref[...])


def _final_norm(x, g, tm):
    m, d = x.shape
    return pl.pallas_call(
        _final_norm_kernel,
        grid=(m // tm,),
        in_specs=[pl.BlockSpec((tm, d), lambda i: (i, 0)), pl.BlockSpec((1, d), lambda i: (0, 0))],
        out_specs=pl.BlockSpec((tm, d), lambda i: (i, 0)),
        out_shape=jax.ShapeDtypeStruct((m, d), F32),
        compiler_params=_cparams(("parallel",)),
        name="final_norm",
    )(x, g.reshape(1, d))


def _gelu(x):
    return 0.5 * x * (1.0 + lax.erf(x * (1.0 / math.sqrt(2.0))))


def _gmlp_kernel(u_ref, v_ref, lng_ref, lnb_ref, ws_ref, bias_ref, y_ref):
    tt = u_ref.shape[1]
    v = _gelu(v_ref[0])
    mu = jnp.mean(v, axis=-1, keepdims=True)
    d = v - mu
    var = jnp.mean(d * d, axis=-1, keepdims=True)
    vn = d * lax.rsqrt(var + LN_EPS) * lng_ref[...] + lnb_ref[...]
    row = lax.broadcasted_iota(jnp.int32, (GCHUNK, GCHUNK), 0)
    col = lax.broadcasted_iota(jnp.int32, (GCHUNK, GCHUNK), 1)
    causal = row >= col
    first_head = lax.broadcasted_iota(jnp.int32, (1, LANES), 1) < HEAD
    ws = [jnp.where(causal, ws_ref[gi], 0.0).astype(BF16) for gi in range(MIX_HEADS)]
    for c in range(tt // GCHUNK):
        rows = slice(c * GCHUNK, (c + 1) * GCHUNK)
        outs = []
        for j in range(PAIRS):
            vp = vn[rows, j * LANES:(j + 1) * LANES].astype(BF16)
            outs.append(jnp.where(first_head, _dot(ws[2 * j], vp), _dot(ws[2 * j + 1], vp)))
        mixed = jnp.concatenate(outs, axis=1) + bias_ref[...]
        y_ref[0, rows, :] = _gelu(u_ref[0, rows, :]) * mixed


def _gmlp(p3, ln_g, ln_b, w_s, bias, tt):
    b, t, _ = p3.shape
    return pl.pallas_call(
        _gmlp_kernel,
        grid=(b, t // tt),
        in_specs=[
            pl.BlockSpec((1, tt, MIX), lambda i, j: (i, j, 0)),
            pl.BlockSpec((1, tt, MIX), lambda i, j: (i, j, 1)),
            pl.BlockSpec((1, MIX), lambda i, j: (0, 0)),
            pl.BlockSpec((1, MIX), lambda i, j: (0, 0)),
            pl.BlockSpec((MIX_HEADS, GCHUNK, GCHUNK), lambda i, j: (0, 0, 0)),
            pl.BlockSpec((GCHUNK, MIX), lambda i, j: (0, 0)),
        ],
        out_specs=pl.BlockSpec((1, tt, MIX), lambda i, j: (i, j, 0)),
        out_shape=jax.ShapeDtypeStruct((b, t, MIX), F32),
        compiler_params=_cparams(("parallel", "parallel")),
        name="gmlp",
    )(p3, p3, ln_g.reshape(1, MIX), ln_b.reshape(1, MIX), w_s, bias)


def _token_shift(cur, prev_scr, mu):
    rows = cur.shape[0]
    row = lax.broadcasted_iota(jnp.int32, cur.shape, 0)
    prev = jnp.where(row == 0, prev_scr[...], pltpu.roll(cur, 1, axis=0))
    prev_scr[...] = cur[rows - 1:rows, :]
    return cur + (prev - cur) * mu


def _rwkv_kernel(r_ref, k_ref, v_ref, l_ref,
                 mur_ref, muk_ref, muv_ref, mul_ref, wcat_ref,
                 w0_ref, a0_ref, kk_ref, ka_ref, rk_ref, lnw_ref, lnb_ref,
                 jmat_ref, tril_ref,
                 y_ref,
                 pr_scr, pk_scr, pv_scr, pl_scr, state_scr,
                 at_scr, bt_scr, kt_scr, rt_scr, bp_scr, kp_scr, v_scr, pe_scr,
                 rk_scr, g_scr, y_scr,
                 wq_scr, bk_scr, pet_scr, u0_scr, y0_scr):
    tt = r_ref.shape[1]
    L = SCAN_CHUNK

    @pl.when(pl.program_id(1) == 0)
    def _():
        pr_scr[...] = jnp.zeros_like(pr_scr)
        pk_scr[...] = jnp.zeros_like(pk_scr)
        pv_scr[...] = jnp.zeros_like(pv_scr)
        pl_scr[...] = jnp.zeros_like(pl_scr)
        state_scr[...] = jnp.zeros_like(state_scr)

    r = _token_shift(r_ref[0], pr_scr, mur_ref[...])
    k = _token_shift(k_ref[0], pk_scr, muk_ref[...])
    v = _token_shift(v_ref[0], pv_scr, muv_ref[...])
    lo = _token_shift(l_ref[0], pl_scr, mul_ref[...])
    llane = lax.broadcasted_iota(jnp.int32, lo.shape, 1)
    z = jnp.where(llane < 64, jnp.tanh(lo), jnp.where(llane < 128, lo, jax.nn.sigmoid(lo)))
    proj = _dot(z.astype(BF16), wcat_ref[...])
    ld = -math.exp(-0.5) * jax.nn.sigmoid(w0_ref[...] + proj[:, :MIX])
    a = jax.nn.sigmoid(a0_ref[...] + proj[:, MIX:2 * MIX])
    jmat = jmat_ref[...]
    kk = k * kk_ref[...]
    kk_sq = kk * kk
    ss = jnp.concatenate(
        [_dot_split(kk_sq[:, j * LANES:(j + 1) * LANES], jmat) for j in range(PAIRS)], axis=1)
    kk = kk * lax.rsqrt(jnp.maximum(ss, 1e-24))
    k = k * (1.0 + (a - 1.0) * ka_ref[...])
    b = kk * a
    chunk_cum = [_split_dot(tril_ref[...], ld[c * L:(c + 1) * L]) for c in range(tt // L)]
    cum = jnp.concatenate(chunk_cum, axis=0)
    cum_end = jnp.concatenate(
        [jnp.broadcast_to(cc[L - 1:L], (L, MIX)) for cc in chunk_cum], axis=0)
    p_inv = jnp.exp(-cum)
    to_end = jnp.exp(cum_end - cum)
    at_scr[...] = -(kk * jnp.exp(cum - ld))
    bt_scr[...] = b * p_inv
    kt_scr[...] = k * p_inv
    rt_scr[...] = r * jnp.exp(cum)
    bp_scr[...] = b * to_end
    kp_scr[...] = k * to_end
    v_scr[...] = v
    pe_scr[...] = jnp.exp(cum_end)
    rk_scr[...] = r * k * rk_ref[...]
    g_scr[...] = proj[:, 2 * MIX:]

    lane = lax.broadcasted_iota(jnp.int32, (1, LANES), 1)
    h0 = lane < HEAD
    row = lax.broadcasted_iota(jnp.int32, (L, LANES), 0)
    col = lax.broadcasted_iota(jnp.int32, (L, LANES), 1) % HEAD
    strict = row > col
    incl = row >= col
    srow = lax.broadcasted_iota(jnp.int32, (LANES, LANES), 0) < HEAD
    scol = lax.broadcasted_iota(jnp.int32, (LANES, LANES), 1) < HEAD
    same_head = srow == scol
    eye = jnp.where(lax.broadcasted_iota(jnp.int32, (LANES, LANES), 0)
                    == lax.broadcasted_iota(jnp.int32, (LANES, LANES), 1), 1.0, 0.0)
    pairs = range(PAIRS)

    def cat(x, y):
        return jnp.concatenate([x, y], axis=0)

    def first(x):
        return jnp.where(h0, x, 0.0)

    def second(x):
        return jnp.where(h0, 0.0, x)

    def lanes(x, y):
        return jnp.concatenate([x, y], axis=1)

    def tile(ref, c, j):
        return ref[c * L:(c + 1) * L, j * LANES:(j + 1) * LANES]


    def prepare(chunks):
        items = [(c, j) for c in chunks for j in pairs]
        idx = range(len(items))
        at = [tile(at_scr, c, j) for c, j in items]
        bt = [tile(bt_scr, c, j) for c, j in items]
        kt = [tile(kt_scr, c, j) for c, j in items]
        rt = [tile(rt_scr, c, j) for c, j in items]
        vc = [tile(v_scr, c, j) for c, j in items]
        g_0 = [_dot_nt(cat(first(at[i]), first(rt[i])).astype(BF16), cat(bt[i], kt[i]).astype(BF16))
               for i in idx]
        g_1 = [_dot_nt(cat(second(at[i]), second(rt[i])).astype(BF16), cat(kt[i], bt[i]).astype(BF16))
               for i in idx]
        yield
        top0 = [jnp.where(strict, g_0[i][:L], 0.0) for i in idx]
        top1 = [jnp.where(strict, g_1[i][:L], 0.0) for i in idx]
        vv = [cat(vc[i], vc[i]).astype(BF16) for i in idx]
        akv0 = [_dot(second(top0[i]).astype(BF16), vv[i]) for i in idx]
        akv1 = [_dot(first(top1[i]).astype(BF16), vv[i]) for i in idx]
        yield
        akv = [jnp.where(h0, akv0[i], akv1[i]) for i in idx]
        z = [lanes(cat(first(at[i]), second(at[i])), cat(first(akv[i]), second(akv[i])))
             .astype(BF16) for i in idx]
        apow = [cat(first(top0[i]), second(top1[i])) for i in idx]
        tinv = [eye + apow[i] for i in idx]
        apow = [apow[i].astype(BF16) for i in idx]
        for _ in range(5):
            apow = [_dot(apow[i], apow[i]).astype(BF16) for i in idx]
            yield
            tinv = [tinv[i] + _dot(tinv[i].astype(BF16), apow[i]) for i in idx]
        z = [_dot(tinv[i].astype(BF16), z[i]) for i in idx]
        yield
        w = [z[i][:L, :LANES] + z[i][L:, :LANES] for i in idx]
        u0 = [z[i][:L, LANES:] + z[i][L:, LANES:] for i in idx]
        zero = jnp.zeros((L, LANES), F32)
        wu = [lanes(w[i], u0[i]) for i in idx]
        zv = [lanes(zero, vc[i]) for i in idx]
        r_0 = [_dot(jnp.where(incl, g_0[i][L:], 0.0).astype(BF16), cat(wu[i], zv[i]).astype(BF16))
               for i in idx]
        r_1 = [_dot(jnp.where(incl, g_1[i][L:], 0.0).astype(BF16), cat(zv[i], wu[i]).astype(BF16))
               for i in idx]
        yield
        for i, (c, j) in enumerate(items):
            q = rt[i] + jnp.where(h0, r_0[i][:, :LANES], r_1[i][:, :LANES])
            wq_scr[c, j] = cat(w[i], q).astype(BF16)
            u0_scr[c, j] = u0[i]
            y0_scr[c, j] = jnp.where(h0, r_0[i][:, LANES:], r_1[i][:, LANES:])
            bk_scr[c, j] = jnp.transpose(cat(tile(bp_scr, c, j), tile(kp_scr, c, j))).astype(BF16)
            pe_row = pe_scr[c * L:c * L + 1, j * LANES:(j + 1) * LANES]
            pet_scr[c, j] = jnp.transpose(jnp.broadcast_to(pe_row, (LANES, LANES)))
        yield

    def advance(c):
        h = [state_scr[j] for j in pairs]
        wqh = [_dot(wq_scr[c, j], h[j].astype(BF16)) for j in pairs]
        yield
        uv = [cat(wqh[j][:L] + u0_scr[c, j], tile(v_scr, c, j)).astype(BF16) for j in pairs]
        upd = [_dot(bk_scr[c, j], uv[j]) for j in pairs]
        for j in pairs:
            y_scr[c * L:(c + 1) * L, j * LANES:(j + 1) * LANES] = wqh[j][L:] + y0_scr[c, j]
        yield
        for j in pairs:
            state_scr[j] = h[j] * pet_scr[c, j] + jnp.where(same_head, upd[j], 0.0)
        yield

    def chain(gens):
        for gen in gens:
            yield from gen

    waiting = iter(())
    for start in range(0, tt // L, PREPARE_CHUNKS):
        group = range(start, start + PREPARE_CHUNKS)
        for _ in prepare(group):
            next(waiting, None)
        for _ in waiting:
            pass
        waiting = chain([advance(c) for c in group])
    for _ in waiting:
        pass

    outs = []
    for j in pairs:
        cols = slice(j * LANES, (j + 1) * LANES)
        y = y_scr[:, cols]
        mean = _dot_split(y, jmat) * (1.0 / HEAD)
        d = y - mean
        var = _dot_split(d * d, jmat) * (1.0 / HEAD)
        yn = d * lax.rsqrt(var + GN_EPS) * lnw_ref[:, cols] + lnb_ref[:, cols]
        rk = _dot_split(rk_scr[:, cols], jmat)
        outs.append((yn + rk * v_scr[:, cols]) * g_scr[:, cols])
    y_ref[0] = jnp.concatenate(outs, axis=1)


def _rwkv(p3, mu, wcat, w0, a0, k_k, k_a, r_k, lnx_w, lnx_b, tt):
    b, t, _ = p3.shape
    row = lambda x: x.reshape(1, -1)
    head_of = jnp.arange(LANES) // HEAD
    jmat = (head_of[:, None] == head_of[None, :]).astype(BF16)
    jmat = jnp.concatenate([jmat, jmat], axis=0)
    idx = jnp.arange(SCAN_CHUNK)
    tril = (idx[:, None] >= idx[None, :]).astype(BF16)
    tril = jnp.concatenate([tril, tril], axis=1)
    const = lambda shape: pl.BlockSpec(shape, lambda i, j: (0,) * len(shape))
    tile = pltpu.VMEM((tt, MIX), F32)
    return pl.pallas_call(
        _rwkv_kernel,
        grid=(b, t // tt),
        in_specs=[
            pl.BlockSpec((1, tt, MIX), lambda i, j: (i, j, 0)),
            pl.BlockSpec((1, tt, MIX), lambda i, j: (i, j, 1)),
            pl.BlockSpec((1, tt, MIX), lambda i, j: (i, j, 2)),
            pl.BlockSpec((1, tt, LORA_W), lambda i, j: (i, j, 3 * MIX // LORA_W)),
            const((1, MIX)), const((1, MIX)), const((1, MIX)), const((1, LORA_W)),
            const((LORA_W, 3 * MIX)),
            const((1, MIX)), const((1, MIX)), const((1, MIX)), const((1, MIX)),
            const((1, MIX)), const((1, MIX)), const((1, MIX)),
            const((2 * LANES, LANES)), const((SCAN_CHUNK, 2 * SCAN_CHUNK)),
        ],
        out_specs=pl.BlockSpec((1, tt, MIX), lambda i, j: (i, j, 0)),
        out_shape=jax.ShapeDtypeStruct((b, t, MIX), F32),
        scratch_shapes=[
            pltpu.VMEM((1, MIX), F32), pltpu.VMEM((1, MIX), F32), pltpu.VMEM((1, MIX), F32),
            pltpu.VMEM((1, LORA_W), F32),
            pltpu.VMEM((PAIRS, LANES, LANES), F32),
        ] + [tile] * 11 + [
            pltpu.VMEM((tt // SCAN_CHUNK, PAIRS, LANES, LANES), BF16),
            pltpu.VMEM((tt // SCAN_CHUNK, PAIRS, LANES, LANES), BF16),
            pltpu.VMEM((tt // SCAN_CHUNK, PAIRS, LANES, LANES), F32),
            pltpu.VMEM((tt // SCAN_CHUNK, PAIRS, SCAN_CHUNK, LANES), F32),
            pltpu.VMEM((tt // SCAN_CHUNK, PAIRS, SCAN_CHUNK, LANES), F32),
        ],
        compiler_params=_cparams(("parallel", "arbitrary")),
        name="rwkv7",
    )(p3, p3, p3, p3,
      row(mu[:MIX]), row(mu[MIX:2 * MIX]), row(mu[2 * MIX:3 * MIX]), row(mu[3 * MIX:]),
      wcat, row(w0), row(a0), row(k_k), row(k_a), row(r_k), row(lnx_w), row(lnx_b),
      jmat, tril)


def kernel(x, mem, mem_norm_g, norm1_g, w_kv_mem, w_out, norm2_g, rwkv_w_in, rwkv_mu, rwkv_w0, rwkv_w2, rwkv_a0, rwkv_a2, rwkv_g2, rwkv_k_k, rwkv_k_a, rwkv_r_k, rwkv_lnx_w, rwkv_lnx_b, ffn_w_gu, ffn_w_down, gmlp_w_in, gmlp_v_ln_g, gmlp_v_ln_b, gmlp_w_s, gmlp_b_s, moe_router, moe_w_gu, moe_w_down, final_norm_g):
    b, t, d = x.shape
    n_tok = b * t
    depth = norm1_g.shape[0]
    xs = x.reshape(n_tok, d)
    mem2 = mem.reshape(b * N_MEM, d)
    for i in range(depth):
        j = i // 2
        kv = _norm_matmul(mem2, mem_norm_g, w_kv_mem[i].astype(BF16), TOKEN_TILE).reshape(b, N_MEM, 2 * MEMW)
        if i % 2 == 0:
            p = _norm_matmul(xs, norm1_g[i], rwkv_w_in[j].astype(BF16), TOKEN_TILE)
            p3 = p.reshape(b, t, RWKV_COLS + MEMW)
            wcat = jnp.zeros((LORA_W, 3 * MIX), F32)
            wcat = wcat.at[0:64, 0:MIX].set(rwkv_w2[j])
            wcat = wcat.at[64:128, MIX:2 * MIX].set(rwkv_a2[j])
            wcat = wcat.at[128:256, 2 * MIX:].set(rwkv_g2[j])
            y = _rwkv(p3, rwkv_mu[j], wcat.astype(BF16), rwkv_w0[j], rwkv_a0[j], rwkv_k_k[j],
                      rwkv_k_a[j], rwkv_r_k[j].reshape(MIX), rwkv_lnx_w[j], rwkv_lnx_b[j], RWKV_TILE)
            q_block = RWKV_COLS // MEMW
        else:
            p = _norm_matmul(xs, norm1_g[i], gmlp_w_in[j].astype(BF16), TOKEN_TILE)
            p3 = p.reshape(b, t, 2 * MIX + MEMW)
            bias = jnp.repeat(jnp.transpose(gmlp_b_s[j]), HEAD, axis=1)
            y = _gmlp(p3, gmlp_v_ln_g[j], gmlp_v_ln_b[j], gmlp_w_s[j], bias, GMLP_TILE)
            q_block = 2 * MIX // MEMW
        o = _mem_attn(p3, q_block, kv, TOKEN_TILE)
        wo = w_out[i].astype(BF16)
        xs = _mixer_out(xs, y.reshape(n_tok, MIX), o.reshape(n_tok, MEMW), wo[:MIX], wo[MIX:], TOKEN_TILE)
        if i % 2 == 0:
            xs = _ffn(xs, norm2_g[i], ffn_w_gu[j].astype(BF16), ffn_w_down[j].astype(BF16), TOKEN_TILE)
        else:
            last = i == depth - 1
            xs = _moe(xs, norm2_g[i], moe_router[j], moe_w_gu[j].astype(BF16),
                      moe_w_down[j].astype(BF16), final_norm_g if last else None,
                      TOKEN_TILE, ROW_COPY_TILE, EXPERT_TILE)
    if depth % 2 == 1:
        xs = _final_norm(xs, final_norm_g, TOKEN_TILE)
    return xs.reshape(b, t, d)
```

```python
import functools
import math

import jax
import jax.numpy as jnp
from jax import lax
from jax.experimental import pallas as pl
from jax.experimental.pallas import tpu as pltpu

F32 = jnp.float32
BF16 = jnp.bfloat16

D_MODEL = 1024
HEAD = 64
MIX = 768
MIX_HEADS = MIX // HEAD
MEMW = 256
MEM_HEADS = 4
N_MEM = 256
LORA_W = 256
RWKV_COLS = 3 * MIX + LORA_W
GCHUNK = 128
D_FF = 2816
N_EXPERTS = 8
RMS_EPS = 1e-6
GN_EPS = 64e-5
LN_EPS = 1e-5

LANES = 128
PAIRS = MIX // LANES
SCAN_CHUNK = 64
PREPARE_CHUNKS = 2
VMEM_LIMIT = 56 * 1024 * 1024

TOKEN_TILE = 512
RWKV_TILE = 512
GMLP_TILE = 256
ROW_COPY_TILE = 256
EXPERT_TILE = 512


def _cparams(sem):
    return pltpu.CompilerParams(dimension_semantics=sem, vmem_limit_bytes=VMEM_LIMIT)


def _rms(x, g):
    ms = jnp.mean(x * x, axis=-1, keepdims=True)
    return x * lax.rsqrt(ms + RMS_EPS) * g


def _dot(a, b):
    return jnp.dot(a, b, preferred_element_type=F32)


def _dot_nt(a, b):
    return lax.dot_general(a, b, (((1,), (1,)), ((), ())), preferred_element_type=F32)


def _dot_tn(a, b):
    return lax.dot_general(a, b, (((0,), (0,)), ((), ())), preferred_element_type=F32)


def _dot_split(x, m2_bf16):
    hi = x.astype(BF16)
    lo = (x - hi.astype(F32)).astype(BF16)
    return _dot(jnp.concatenate([hi, lo], axis=1), m2_bf16)


def _split_dot(m2_bf16, x):
    hi = x.astype(BF16)
    lo = (x - hi.astype(F32)).astype(BF16)
    return _dot(m2_bf16, jnp.concatenate([hi, lo], axis=0))


def _norm_matmul_kernel(x_ref, g_ref, w_ref, o_ref):
    n = _rms(x_ref[...], g_ref[...]).astype(BF16)
    o_ref[...] = _dot(n, w_ref[...]).astype(o_ref.dtype)


def _norm_matmul(x, g, w, tm):
    m, k = x.shape
    n = w.shape[1]
    return pl.pallas_call(
        _norm_matmul_kernel,
        grid=(m // tm,),
        in_specs=[
            pl.BlockSpec((tm, k), lambda i: (i, 0)),
            pl.BlockSpec((1, k), lambda i: (0, 0)),
            pl.BlockSpec((k, n), lambda i: (0, 0)),
        ],
        out_specs=pl.BlockSpec((tm, n), lambda i: (i, 0)),
        out_shape=jax.ShapeDtypeStruct((m, n), BF16),
        compiler_params=_cparams(("parallel",)),
        name="norm_matmul",
    )(x, g.reshape(1, k), w)


def _mem_attn_kernel(q_ref, kv_ref, o_ref):
    q = q_ref[0]
    kv = kv_ref[0]
    k = kv[:, :MEMW].astype(BF16)
    v = kv[:, MEMW:].astype(BF16)
    lane = lax.broadcasted_iota(jnp.int32, (1, MEMW), 1)
    acc = jnp.zeros(q.shape, F32)
    for h in range(MEM_HEADS):
        m = (lane >= h * HEAD) & (lane < (h + 1) * HEAD)
        qh = jnp.where(m, q, 0.0).astype(BF16)
        s = _dot_nt(qh, k) * (1.0 / math.sqrt(HEAD))
        s = s - jnp.max(s, axis=-1, keepdims=True)
        p = jnp.exp(s)
        l = jnp.sum(p, axis=-1, keepdims=True)
        pv = _dot(p.astype(BF16), v)
        acc = acc + jnp.where(m, pv / l, 0.0)
    o_ref[0] = acc.astype(o_ref.dtype)


def _mem_attn(p3, q_block, kv3, tt):
    b, t, _ = p3.shape
    return pl.pallas_call(
        _mem_attn_kernel,
        grid=(b, t // tt),
        in_specs=[
            pl.BlockSpec((1, tt, MEMW), lambda i, j: (i, j, q_block)),
            pl.BlockSpec((1, N_MEM, 2 * MEMW), lambda i, j: (i, 0, 0)),
        ],
        out_specs=pl.BlockSpec((1, tt, MEMW), lambda i, j: (i, j, 0)),
        out_shape=jax.ShapeDtypeStruct((b, t, MEMW), BF16),
        compiler_params=_cparams(("parallel", "parallel")),
        name="mem_attn",
    )(p3, kv3)


def _mixer_out_kernel(x_ref, y_ref, o_ref, wy_ref, wo_ref, out_ref):
    out_ref[...] = (x_ref[...]
                    + _dot(y_ref[...], wy_ref[...])
                    + _dot(o_ref[...], wo_ref[...]))


def _mixer_out(x, y, o, wy, wo, tm):
    m, d = x.shape
    return pl.pallas_call(
        _mixer_out_kernel,
        grid=(m // tm,),
        in_specs=[
            pl.BlockSpec((tm, d), lambda i: (i, 0)),
            pl.BlockSpec((tm, MIX), lambda i: (i, 0)),
            pl.BlockSpec((tm, MEMW), lambda i: (i, 0)),
            pl.BlockSpec((MIX, d), lambda i: (0, 0)),
            pl.BlockSpec((MEMW, d), lambda i: (0, 0)),
        ],
        out_specs=pl.BlockSpec((tm, d), lambda i: (i, 0)),
        out_shape=jax.ShapeDtypeStruct((m, d), F32),
        compiler_params=_cparams(("parallel",)),
        name="mixer_out",
    )(x, y, o, wy, wo)


def _swiglu(n, wg_ref, wu_ref, wd_ref):
    gate = _dot(n, wg_ref[...])
    up = _dot(n, wu_ref[...])
    h = gate * jax.nn.sigmoid(gate) * up
    return _dot(h.astype(BF16), wd_ref[...])


def _ffn_kernel(x_ref, g_ref, wg_ref, wu_ref, wd_ref, out_ref):
    x = x_ref[...]
    out_ref[...] = x + _swiglu(_rms(x, g_ref[...]).astype(BF16), wg_ref, wu_ref, wd_ref)


def _ffn(x, g, w_gu, w_down, tm):
    m, d = x.shape
    resident = pl.Buffered(1)
    return pl.pallas_call(
        _ffn_kernel,
        grid=(m // tm,),
        in_specs=[
            pl.BlockSpec((tm, d), lambda i: (i, 0)),
            pl.BlockSpec((1, d), lambda i: (0, 0)),
            pl.BlockSpec((d, D_FF), lambda i: (0, 0), pipeline_mode=resident),
            pl.BlockSpec((d, D_FF), lambda i: (0, 1), pipeline_mode=resident),
            pl.BlockSpec((D_FF, d), lambda i: (0, 0), pipeline_mode=resident),
        ],
        out_specs=pl.BlockSpec((tm, d), lambda i: (i, 0)),
        out_shape=jax.ShapeDtypeStruct((m, d), F32),
        compiler_params=_cparams(("parallel",)),
        name="ffn",
    )(x, g.reshape(1, d), w_gu, w_gu, w_down)


ROUTE_E1, ROUTE_E2, ROUTE_R1, ROUTE_R2, ROUTE_G1, ROUTE_G2 = range(6)
ROUTE_FIELDS = 8


def _route_kernel(x_ref, g_ref, wr_ref, tril_ref, route_ref, routet_ref, cnt_ref, carry_scr):
    @pl.when(pl.program_id(0) == 0)
    def _():
        carry_scr[...] = jnp.zeros_like(carry_scr)

    n = _rms(x_ref[...], g_ref[...])
    n_hi = n.astype(BF16)
    n_lo = (n - n_hi.astype(F32)).astype(BF16)
    logits = _dot(jnp.concatenate([n_hi, n_hi, n_lo], axis=1), wr_ref[...])
    lane = lax.broadcasted_iota(jnp.int32, logits.shape, 1)
    neg = jnp.float32(-jnp.inf)
    lg = jnp.where(lane < N_EXPERTS, logits, neg)
    m1 = jnp.max(lg, axis=-1, keepdims=True)
    i1 = jnp.min(jnp.where(lg == m1, lane, LANES), axis=-1, keepdims=True)
    lg2 = jnp.where(lane == i1, neg, lg)
    m2 = jnp.max(lg2, axis=-1, keepdims=True)
    i2 = jnp.min(jnp.where(lg2 == m2, lane, LANES), axis=-1, keepdims=True)
    e2 = jnp.exp(m2 - m1)
    g1 = 1.0 / (1.0 + e2)
    g2 = e2 / (1.0 + e2)
    chosen = jnp.where(lane == i1, 1.0, jnp.where(lane == i2, 1.0, 0.0))
    before = _dot(tril_ref[...], chosen.astype(BF16)) + carry_scr[...]
    r1 = jnp.sum(jnp.where(lane == i1, before, 0.0), axis=-1, keepdims=True)
    r2 = jnp.sum(jnp.where(lane == i2, before, 0.0), axis=-1, keepdims=True)
    carry_scr[...] += jnp.sum(chosen, axis=0, keepdims=True)
    cnt_ref[...] = carry_scr[...]
    fields = (i1.astype(F32), i2.astype(F32), r1, r2, g1, g2)
    route = jnp.zeros(logits.shape, F32)
    for idx, val in enumerate(fields):
        route = jnp.where(lane == idx, val, route)
    route_ref[...] = route
    routet_ref[...] = jnp.transpose(route)[:ROUTE_FIELDS]


def _route(x, g, wr_split, tm):
    m, d = x.shape
    idx = jnp.arange(tm)
    tril = (idx[:, None] > idx[None, :]).astype(BF16)
    return pl.pallas_call(
        _route_kernel,
        grid=(m // tm,),
        in_specs=[
            pl.BlockSpec((tm, d), lambda i: (i, 0)),
            pl.BlockSpec((1, d), lambda i: (0, 0)),
            pl.BlockSpec((3 * d, LANES), lambda i: (0, 0)),
            pl.BlockSpec((tm, tm), lambda i: (0, 0)),
        ],
        out_specs=[pl.BlockSpec((tm, LANES), lambda i: (i, 0)),
                   pl.BlockSpec((ROUTE_FIELDS, tm), lambda i: (0, i)),
                   pl.BlockSpec((1, LANES), lambda i: (0, 0))],
        out_shape=[jax.ShapeDtypeStruct((m, LANES), F32),
                   jax.ShapeDtypeStruct((ROUTE_FIELDS, m), F32),
                   jax.ShapeDtypeStruct((1, LANES), F32)],
        scratch_shapes=[pltpu.VMEM((1, LANES), F32)],
        compiler_params=_cparams(("arbitrary",)),
        name="route",
    )(x, g.reshape(1, d), wr_split, tril)


def _row_copy(src, src_row, dst, dst_row, sem):
    return pltpu.make_async_copy(src.at[pl.ds(src_row, 1)], dst.at[pl.ds(dst_row, 1)], sem)


ROW_LOOP_UNROLL = 8


def _row_loop(tm, body):
    def step(i, carry):
        body(i)
        return carry
    lax.fori_loop(0, tm, step, 0, unroll=ROW_LOOP_UNROLL)


def _dispatch_kernel(p1_ref, p2_ref, fill_ref, x_ref, g_ref, xs_ref, n_scr, zero_scr, sem, fill_sem):
    tm = x_ref.shape[0]
    tile_rows = zero_scr.shape[0]
    step = pl.program_id(0)

    def fill_copy(f):
        return pltpu.make_async_copy(
            zero_scr, xs_ref.at[pl.ds(pl.multiple_of(fill_ref[f] * tile_rows, tile_rows), tile_rows)],
            fill_sem)

    @pl.when(step == 0)
    def _():
        zero_scr[...] = jnp.zeros_like(zero_scr)
        for f in range(fill_ref.shape[0]):
            @pl.when(fill_ref[f] >= 0)
            def _():
                fill_copy(f).start()
        for f in range(fill_ref.shape[0]):
            @pl.when(fill_ref[f] >= 0)
            def _():
                fill_copy(f).wait()

    def copies(s, i):
        slot = s % 2
        tok = s * tm + i
        return (_row_copy(n_scr.at[slot], i, xs_ref, p1_ref[tok], sem.at[slot]),
                _row_copy(n_scr.at[slot], i, xs_ref, p2_ref[tok], sem.at[slot]))

    def start(s):
        def body(i):
            first, second = copies(s, i)
            first.start()
            second.start(priority=1)
        _row_loop(tm, body)

    def wait(s):
        def body(i):
            first, second = copies(s, i)
            first.wait()
            second.wait()
        _row_loop(tm, body)

    n_scr[step % 2] = _rms(x_ref[...], g_ref[...])
    start(step)

    @pl.when(step > 0)
    def _():
        wait(step - 1)

    @pl.when(step == pl.num_programs(0) - 1)
    def _():
        wait(step)


def _dispatch(x, g, pos1, pos2, fill, rows, tm, tile_rows):
    m, d = x.shape
    return pl.pallas_call(
        _dispatch_kernel,
        grid_spec=pltpu.PrefetchScalarGridSpec(
            num_scalar_prefetch=3,
            grid=(m // tm,),
            in_specs=[
                pl.BlockSpec((tm, d), lambda i, p1, p2, fl: (i, 0)),
                pl.BlockSpec((1, d), lambda i, p1, p2, fl: (0, 0)),
            ],
            out_specs=pl.BlockSpec(memory_space=pl.ANY),
            scratch_shapes=[pltpu.VMEM((2, tm, d), F32), pltpu.VMEM((tile_rows, d), F32),
                            pltpu.SemaphoreType.DMA((2,)), pltpu.SemaphoreType.DMA],
        ),
        out_shape=jax.ShapeDtypeStruct((rows, d), F32),
        compiler_params=_cparams(("arbitrary",)),
        name="dispatch",
    )(pos1, pos2, fill, x, g.reshape(1, d))


def _experts_kernel(te_ref, nv_ref, x_ref, wg_ref, wu_ref, wd_ref, y_ref):
    del te_ref
    valid = pl.program_id(0) < nv_ref[0]

    @pl.when(valid)
    def _():
        y_ref[...] = _swiglu(x_ref[...].astype(BF16), wg_ref, wu_ref, wd_ref)

    @pl.when(jnp.logical_not(valid))
    def _():
        y_ref[...] = jnp.zeros_like(y_ref)


def _experts(xs, tile_expert, n_valid, w_gu, w_down, tm):
    rows, d = xs.shape
    return pl.pallas_call(
        _experts_kernel,
        grid_spec=pltpu.PrefetchScalarGridSpec(
            num_scalar_prefetch=2,
            grid=(rows // tm,),
            in_specs=[
                pl.BlockSpec((tm, d), lambda t, te, nv: (t, 0)),
                pl.BlockSpec((None, d, D_FF), lambda t, te, nv: (te[t], 0, 0)),
                pl.BlockSpec((None, d, D_FF), lambda t, te, nv: (te[t], 0, 1)),
                pl.BlockSpec((None, D_FF, d), lambda t, te, nv: (te[t], 0, 0)),
            ],
            out_specs=pl.BlockSpec((tm, d), lambda t, te, nv: (t, 0)),
        ),
        out_shape=jax.ShapeDtypeStruct((rows, d), F32),
        compiler_params=_cparams(("arbitrary",)),
        name="experts",
    )(tile_expert, n_valid, xs, w_gu, w_gu, w_down)


def _combine_kernel(p1_ref, p2_ref, x_ref, route_ref, fg_ref, ys_ref, out_ref, buf, sem, *, final):
    tm = x_ref.shape[0]
    step = pl.program_id(0)

    def copies(s, i):
        slot = s % 2
        tok = s * tm + i
        return (_row_copy(ys_ref, p1_ref[tok], buf.at[slot, 0], i, sem.at[slot]),
                _row_copy(ys_ref, p2_ref[tok], buf.at[slot, 1], i, sem.at[slot]))

    def start(s):
        def body(i):
            first, second = copies(s, i)
            first.start()
            second.start(priority=1)
        _row_loop(tm, body)

    def wait(s):
        def body(i):
            first, second = copies(s, i)
            first.wait()
            second.wait()
        _row_loop(tm, body)

    @pl.when(step == 0)
    def _():
        start(step)

    @pl.when(step + 1 < pl.num_programs(0))
    def _():
        start(step + 1)

    wait(step)
    rows = buf[step % 2]
    route = route_ref[...]
    g1 = route[:, ROUTE_G1:ROUTE_G1 + 1]
    g2 = route[:, ROUTE_G2:ROUTE_G2 + 1]
    out = x_ref[...] + g1 * rows[0] + g2 * rows[1]
    out_ref[...] = _rms(out, fg_ref[...]) if final else out


def _combine(x, route, ys, pos1, pos2, final_g, tm):
    m, d = x.shape
    final = final_g is not None
    fg = final_g.reshape(1, d) if final else jnp.ones((1, d), F32)
    return pl.pallas_call(
        functools.partial(_combine_kernel, final=final),
        grid_spec=pltpu.PrefetchScalarGridSpec(
            num_scalar_prefetch=2,
            grid=(m // tm,),
            in_specs=[
                pl.BlockSpec((tm, d), lambda i, p1, p2: (i, 0)),
                pl.BlockSpec((tm, LANES), lambda i, p1, p2: (i, 0)),
                pl.BlockSpec((1, d), lambda i, p1, p2: (0, 0)),
                pl.BlockSpec(memory_space=pl.ANY),
            ],
            out_specs=pl.BlockSpec((tm, d), lambda i, p1, p2: (i, 0)),
            scratch_shapes=[pltpu.VMEM((2, 2, tm, d), F32), pltpu.SemaphoreType.DMA((2,))],
        ),
        out_shape=jax.ShapeDtypeStruct((m, d), F32),
        compiler_params=_cparams(("arbitrary",)),
        name="combine",
    )(pos1, pos2, x, route, fg, ys)


def _moe(x, g, w_router, w_gu, w_down, final_g, tm_route, tm_rows, tm_expert):
    m, d = x.shape
    wr_pad = jnp.zeros((d, LANES), F32).at[:, :N_EXPERTS].set(w_router)
    wr_hi = wr_pad.astype(BF16)
    wr_lo = (wr_pad - wr_hi.astype(F32)).astype(BF16)
    wr_split = jnp.concatenate([wr_hi, wr_lo, wr_hi], axis=0)
    route, route_t, counts = _route(x, g, wr_split, tm_route)
    counts = counts[0, :N_EXPERTS].astype(jnp.int32)
    tiles = (counts + tm_expert - 1) // tm_expert
    experts = jnp.arange(N_EXPERTS, dtype=jnp.int32)
    tile_end = jnp.sum(jnp.where(experts[None, :] <= experts[:, None], tiles[None, :], 0), axis=1)
    row_start = (tile_end - tiles) * tm_expert

    def slot(e_row, r_row):
        e = route_t[e_row].astype(jnp.int32)
        start = jnp.sum(jnp.where(e[:, None] == experts[None, :], row_start[None, :], 0), axis=1)
        return start + route_t[r_row].astype(jnp.int32)

    pos1 = slot(ROUTE_E1, ROUTE_R1)
    pos2 = slot(ROUTE_E2, ROUTE_R2)
    n_tiles = 2 * m // tm_expert + N_EXPERTS
    tile_ids = jnp.arange(n_tiles, dtype=jnp.int32)
    tile_expert = jnp.minimum(
        jnp.sum((tile_ids[:, None] >= tile_end[None, :]).astype(jnp.int32), axis=1), N_EXPERTS - 1)
    n_valid = tile_end[-1:]
    tail = n_valid + experts
    fill = jnp.concatenate([jnp.where(tiles > 0, tile_end - 1, -1),
                            jnp.where(tail < n_tiles, tail, -1)])
    xs = _dispatch(x, g, pos1, pos2, fill, n_tiles * tm_expert, tm_rows, tm_expert)
    ys = _experts(xs, tile_expert, n_valid, w_gu, w_down, tm_expert)
    return _combine(x, route, ys, pos1, pos2, final_g, tm_rows)


def _final_norm_kernel(x_ref, g_ref, o_ref):
    o_ref[...] = _rms(x_ref[...], g_ref[...])


def _final_norm(x, g, tm):
    m, d = x.shape
    return pl.pallas_call(
        _final_norm_kernel,
        grid=(m // tm,),
        in_specs=[pl.BlockSpec((tm, d), lambda i: (i, 0)), pl.BlockSpec((1, d), lambda i: (0, 0))],
        out_specs=pl.BlockSpec((tm, d), lambda i: (i, 0)),
        out_shape=jax.ShapeDtypeStruct((m, d), F32),
        compiler_params=_cparams(("parallel",)),
        name="final_norm",
    )(x, g.reshape(1, d))


def _gelu(x):
    return 0.5 * x * (1.0 + lax.erf(x * (1.0 / math.sqrt(2.0))))


def _gmlp_kernel(u_ref, v_ref, lng_ref, lnb_ref, ws_ref, bias_ref, y_ref):
    tt = u_ref.shape[1]
    v = _gelu(v_ref[0].astype(F32))
    mu = jnp.mean(v, axis=-1, keepdims=True)
    d = v - mu
    var = jnp.mean(d * d, axis=-1, keepdims=True)
    vn = d * lax.rsqrt(var + LN_EPS) * lng_ref[...] + lnb_ref[...]
    row = lax.broadcasted_iota(jnp.int32, (GCHUNK, GCHUNK), 0)
    col = lax.broadcasted_iota(jnp.int32, (GCHUNK, GCHUNK), 1)
    causal = row >= col
    first_head = lax.broadcasted_iota(jnp.int32, (1, LANES), 1) < HEAD
    ws = [jnp.where(causal, ws_ref[gi], 0.0).astype(BF16) for gi in range(MIX_HEADS)]
    for c in range(tt // GCHUNK):
        rows = slice(c * GCHUNK, (c + 1) * GCHUNK)
        outs = []
        for j in range(PAIRS):
            vp = vn[rows, j * LANES:(j + 1) * LANES].astype(BF16)
            outs.append(jnp.where(first_head, _dot(ws[2 * j], vp), _dot(ws[2 * j + 1], vp)))
        mixed = jnp.concatenate(outs, axis=1) + bias_ref[...]
        u = _gelu(u_ref[0, rows, :].astype(F32))
        y_ref[0, rows, :] = (u * mixed).astype(y_ref.dtype)


def _gmlp(p3, ln_g, ln_b, w_s, bias, tt):
    b, t, _ = p3.shape
    return pl.pallas_call(
        _gmlp_kernel,
        grid=(b, t // tt),
        in_specs=[
            pl.BlockSpec((1, tt, MIX), lambda i, j: (i, j, 0)),
            pl.BlockSpec((1, tt, MIX), lambda i, j: (i, j, 1)),
            pl.BlockSpec((1, MIX), lambda i, j: (0, 0)),
            pl.BlockSpec((1, MIX), lambda i, j: (0, 0)),
            pl.BlockSpec((MIX_HEADS, GCHUNK, GCHUNK), lambda i, j: (0, 0, 0)),
            pl.BlockSpec((GCHUNK, MIX), lambda i, j: (0, 0)),
        ],
        out_specs=pl.BlockSpec((1, tt, MIX), lambda i, j: (i, j, 0)),
        out_shape=jax.ShapeDtypeStruct((b, t, MIX), BF16),
        compiler_params=_cparams(("parallel", "parallel")),
        name="gmlp",
    )(p3, p3, ln_g.reshape(1, MIX), ln_b.reshape(1, MIX), w_s, bias)


def _token_shift(cur, prev_scr, mu):
    rows = cur.shape[0]
    row = lax.broadcasted_iota(jnp.int32, cur.shape, 0)
    prev = jnp.where(row == 0, prev_scr[...], pltpu.roll(cur, 1, axis=0))
    prev_scr[...] = cur[rows - 1:rows, :]
    return cur + (prev - cur) * mu


def _rwkv_kernel(r_ref, k_ref, v_ref, l_ref,
                 mur_ref, muk_ref, muv_ref, mul_ref, wcat_ref,
                 w0_ref, a0_ref, kk_ref, ka_ref, rk_ref, lnw_ref, lnb_ref,
                 jmat_ref, tril_ref,
                 y_ref,
                 pr_scr, pk_scr, pv_scr, pl_scr, state_scr,
                 at_scr, bt_scr, kt_scr, rt_scr, bp_scr, kp_scr, v_scr, pe_scr,
                 rk_scr, g_scr, y_scr,
                 wq_scr, bk_scr, pet_scr, u0_scr, y0_scr):
    tt = r_ref.shape[1]
    L = SCAN_CHUNK

    @pl.when(pl.program_id(1) == 0)
    def _():
        pr_scr[...] = jnp.zeros_like(pr_scr)
        pk_scr[...] = jnp.zeros_like(pk_scr)
        pv_scr[...] = jnp.zeros_like(pv_scr)
        pl_scr[...] = jnp.zeros_like(pl_scr)
        state_scr[...] = jnp.zeros_like(state_scr)

    r = _token_shift(r_ref[0].astype(F32), pr_scr, mur_ref[...])
    k = _token_shift(k_ref[0].astype(F32), pk_scr, muk_ref[...])
    v = _token_shift(v_ref[0].astype(F32), pv_scr, muv_ref[...])
    lo = _token_shift(l_ref[0].astype(F32), pl_scr, mul_ref[...])
    llane = lax.broadcasted_iota(jnp.int32, lo.shape, 1)
    z = jnp.where(llane < 64, jnp.tanh(lo), jnp.where(llane < 128, lo, jax.nn.sigmoid(lo)))
    proj = _dot(z.astype(BF16), wcat_ref[...])
    ld = -math.exp(-0.5) * jax.nn.sigmoid(w0_ref[...] + proj[:, :MIX])
    a = jax.nn.sigmoid(a0_ref[...] + proj[:, MIX:2 * MIX])
    jmat = jmat_ref[...]
    kk = k * kk_ref[...]
    kk_sq = kk * kk
    ss = jnp.concatenate(
        [_dot_split(kk_sq[:, j * LANES:(j + 1) * LANES], jmat) for j in range(PAIRS)], axis=1)
    kk = kk * lax.rsqrt(jnp.maximum(ss, 1e-24))
    k = k * (1.0 + (a - 1.0) * ka_ref[...])
    b = kk * a
    chunk_cum = [_split_dot(tril_ref[...], ld[c * L:(c + 1) * L]) for c in range(tt // L)]
    cum = jnp.concatenate(chunk_cum, axis=0)
    cum_end = jnp.concatenate(
        [jnp.broadcast_to(cc[L - 1:L], (L, MIX)) for cc in chunk_cum], axis=0)
    p_inv = jnp.exp(-cum)
    to_end = jnp.exp(cum_end - cum)
    at_scr[...] = -(kk * jnp.exp(cum - ld))
    bt_scr[...] = b * p_inv
    kt_scr[...] = k * p_inv
    rt_scr[...] = r * jnp.exp(cum)
    bp_scr[...] = b * to_end
    kp_scr[...] = k * to_end
    v_scr[...] = v
    pe_scr[...] = jnp.exp(cum_end)
    rk_scr[...] = r * k * rk_ref[...]
    g_scr[...] = proj[:, 2 * MIX:]

    lane = lax.broadcasted_iota(jnp.int32, (1, LANES), 1)
    h0 = lane < HEAD
    row = lax.broadcasted_iota(jnp.int32, (L, LANES), 0)
    col = lax.broadcasted_iota(jnp.int32, (L, LANES), 1) % HEAD
    strict = row > col
    incl = row >= col
    srow = lax.broadcasted_iota(jnp.int32, (LANES, LANES), 0) < HEAD
    scol = lax.broadcasted_iota(jnp.int32, (LANES, LANES), 1) < HEAD
    same_head = srow == scol
    eye = jnp.where(lax.broadcasted_iota(jnp.int32, (LANES, LANES), 0)
                    == lax.broadcasted_iota(jnp.int32, (LANES, LANES), 1), 1.0, 0.0)
    pairs = range(PAIRS)

    def cat(x, y):
        return jnp.concatenate([x, y], axis=0)

    def first(x):
        return jnp.where(h0, x, 0.0)

    def second(x):
        return jnp.where(h0, 0.0, x)

    def lanes(x, y):
        return jnp.concatenate([x, y], axis=1)

    def tile(ref, c, j):
        return ref[c * L:(c + 1) * L, j * LANES:(j + 1) * LANES]


    def prepare(chunks):
        items = [(c, j) for c in chunks for j in pairs]
        idx = range(len(items))
        at = [tile(at_scr, c, j) for c, j in items]
        bt = [tile(bt_scr, c, j) for c, j in items]
        kt = [tile(kt_scr, c, j) for c, j in items]
        rt = [tile(rt_scr, c, j) for c, j in items]
        vc = [tile(v_scr, c, j) for c, j in items]
        g_0 = [_dot_nt(cat(first(at[i]), first(rt[i])).astype(BF16), cat(bt[i], kt[i]).astype(BF16))
               for i in idx]
        g_1 = [_dot_nt(cat(second(at[i]), second(rt[i])).astype(BF16), cat(kt[i], bt[i]).astype(BF16))
               for i in idx]
        yield
        top0 = [jnp.where(strict, g_0[i][:L], 0.0) for i in idx]
        top1 = [jnp.where(strict, g_1[i][:L], 0.0) for i in idx]
        vv = [cat(vc[i], vc[i]).astype(BF16) for i in idx]
        akv0 = [_dot(second(top0[i]).astype(BF16), vv[i]) for i in idx]
        akv1 = [_dot(first(top1[i]).astype(BF16), vv[i]) for i in idx]
        yield
        akv = [jnp.where(h0, akv0[i], akv1[i]) for i in idx]
        z = [lanes(cat(first(at[i]), second(at[i])), cat(first(akv[i]), second(akv[i])))
             .astype(BF16) for i in idx]
        apow = [cat(first(top0[i]), second(top1[i])) for i in idx]
        tinv = [eye + apow[i] for i in idx]
        apow = [apow[i].astype(BF16) for i in idx]
        for _ in range(5):
            apow = [_dot(apow[i], apow[i]).astype(BF16) for i in idx]
            yield
            tinv = [tinv[i] + _dot(tinv[i].astype(BF16), apow[i]) for i in idx]
        z = [_dot(tinv[i].astype(BF16), z[i]) for i in idx]
        yield
        w = [z[i][:L, :LANES] + z[i][L:, :LANES] for i in idx]
        u0 = [z[i][:L, LANES:] + z[i][L:, LANES:] for i in idx]
        zero = jnp.zeros((L, LANES), F32)
        wu = [lanes(w[i], u0[i]) for i in idx]
        zv = [lanes(zero, vc[i]) for i in idx]
        r_0 = [_dot(jnp.where(incl, g_0[i][L:], 0.0).astype(BF16), cat(wu[i], zv[i]).astype(BF16))
               for i in idx]
        r_1 = [_dot(jnp.where(incl, g_1[i][L:], 0.0).astype(BF16), cat(zv[i], wu[i]).astype(BF16))
               for i in idx]
        yield
        for i, (c, j) in enumerate(items):
            q = rt[i] + jnp.where(h0, r_0[i][:, :LANES], r_1[i][:, :LANES])
            wq_scr[c, j] = cat(w[i], q).astype(BF16)
            u0_scr[c, j] = u0[i]
            y0_scr[c, j] = jnp.where(h0, r_0[i][:, LANES:], r_1[i][:, LANES:])
            bk_scr[c, j] = jnp.transpose(cat(tile(bp_scr, c, j), tile(kp_scr, c, j))).astype(BF16)
            pe_row = pe_scr[c * L:c * L + 1, j * LANES:(j + 1) * LANES]
            pet_scr[c, j] = jnp.transpose(jnp.broadcast_to(pe_row, (LANES, LANES)))
        yield

    def advance(c):
        h = [state_scr[j] for j in pairs]
        wqh = [_dot(wq_scr[c, j], h[j].astype(BF16)) for j in pairs]
        yield
        uv = [cat(wqh[j][:L] + u0_scr[c, j], tile(v_scr, c, j)).astype(BF16) for j in pairs]
        upd = [_dot(bk_scr[c, j], uv[j]) for j in pairs]
        for j in pairs:
            y_scr[c * L:(c + 1) * L, j * LANES:(j + 1) * LANES] = wqh[j][L:] + y0_scr[c, j]
        yield
        for j in pairs:
            state_scr[j] = h[j] * pet_scr[c, j] + jnp.where(same_head, upd[j], 0.0)
        yield

    def chain(gens):
        for gen in gens:
            yield from gen

    waiting = iter(())
    for start in range(0, tt // L, PREPARE_CHUNKS):
        group = range(start, start + PREPARE_CHUNKS)
        for _ in prepare(group):
            next(waiting, None)
        for _ in waiting:
            pass
        waiting = chain([advance(c) for c in group])
    for _ in waiting:
        pass

    outs = []
    for j in pairs:
        cols = slice(j * LANES, (j + 1) * LANES)
        y = y_scr[:, cols]
        mean = _dot_split(y, jmat) * (1.0 / HEAD)
        d = y - mean
        var = _dot_split(d * d, jmat) * (1.0 / HEAD)
        yn = d * lax.rsqrt(var + GN_EPS) * lnw_ref[:, cols] + lnb_ref[:, cols]
        rk = _dot_split(rk_scr[:, cols], jmat)
        outs.append((yn + rk * v_scr[:, cols]) * g_scr[:, cols])
    y_ref[0] = jnp.concatenate(outs, axis=1).astype(y_ref.dtype)


def _rwkv(p3, mu, wcat, w0, a0, k_k, k_a, r_k, lnx_w, lnx_b, tt):
    b, t, _ = p3.shape
    row = lambda x: x.reshape(1, -1)
    head_of = jnp.arange(LANES) // HEAD
    jmat = (head_of[:, None] == head_of[None, :]).astype(BF16)
    jmat = jnp.concatenate([jmat, jmat], axis=0)
    idx = jnp.arange(SCAN_CHUNK)
    tril = (idx[:, None] >= idx[None, :]).astype(BF16)
    tril = jnp.concatenate([tril, tril], axis=1)
    const = lambda shape: pl.BlockSpec(shape, lambda i, j: (0,) * len(shape))
    tile = pltpu.VMEM((tt, MIX), F32)
    return pl.pallas_call(
        _rwkv_kernel,
        grid=(b, t // tt),
        in_specs=[
            pl.BlockSpec((1, tt, MIX), lambda i, j: (i, j, 0)),
            pl.BlockSpec((1, tt, MIX), lambda i, j: (i, j, 1)),
            pl.BlockSpec((1, tt, MIX), lambda i, j: (i, j, 2)),
            pl.BlockSpec((1, tt, LORA_W), lambda i, j: (i, j, 3 * MIX // LORA_W)),
            const((1, MIX)), const((1, MIX)), const((1, MIX)), const((1, LORA_W)),
            const((LORA_W, 3 * MIX)),
            const((1, MIX)), const((1, MIX)), const((1, MIX)), const((1, MIX)),
            const((1, MIX)), const((1, MIX)), const((1, MIX)),
            const((2 * LANES, LANES)), const((SCAN_CHUNK, 2 * SCAN_CHUNK)),
        ],
        out_specs=pl.BlockSpec((1, tt, MIX), lambda i, j: (i, j, 0)),
        out_shape=jax.ShapeDtypeStruct((b, t, MIX), BF16),
        scratch_shapes=[
            pltpu.VMEM((1, MIX), F32), pltpu.VMEM((1, MIX), F32), pltpu.VMEM((1, MIX), F32),
            pltpu.VMEM((1, LORA_W), F32),
            pltpu.VMEM((PAIRS, LANES, LANES), F32),
        ] + [tile] * 11 + [
            pltpu.VMEM((tt // SCAN_CHUNK, PAIRS, LANES, LANES), BF16),
            pltpu.VMEM((tt // SCAN_CHUNK, PAIRS, LANES, LANES), BF16),
            pltpu.VMEM((tt // SCAN_CHUNK, PAIRS, LANES, LANES), F32),
            pltpu.VMEM((tt // SCAN_CHUNK, PAIRS, SCAN_CHUNK, LANES), F32),
            pltpu.VMEM((tt // SCAN_CHUNK, PAIRS, SCAN_CHUNK, LANES), F32),
        ],
        compiler_params=_cparams(("parallel", "arbitrary")),
        name="rwkv7",
    )(p3, p3, p3, p3,
      row(mu[:MIX]), row(mu[MIX:2 * MIX]), row(mu[2 * MIX:3 * MIX]), row(mu[3 * MIX:]),
      wcat, row(w0), row(a0), row(k_k), row(k_a), row(r_k), row(lnx_w), row(lnx_b),
      jmat, tril)


def kernel(x, mem, mem_norm_g, norm1_g, w_kv_mem, w_out, norm2_g, rwkv_w_in, rwkv_mu, rwkv_w0, rwkv_w2, rwkv_a0, rwkv_a2, rwkv_g2, rwkv_k_k, rwkv_k_a, rwkv_r_k, rwkv_lnx_w, rwkv_lnx_b, ffn_w_gu, ffn_w_down, gmlp_w_in, gmlp_v_ln_g, gmlp_v_ln_b, gmlp_w_s, gmlp_b_s, moe_router, moe_w_gu, moe_w_down, final_norm_g):
    b, t, d = x.shape
    n_tok = b * t
    depth = norm1_g.shape[0]
    xs = x.reshape(n_tok, d)
    mem2 = mem.reshape(b * N_MEM, d)
    for i in range(depth):
        j = i // 2
        kv = _norm_matmul(mem2, mem_norm_g, w_kv_mem[i].astype(BF16), TOKEN_TILE).reshape(b, N_MEM, 2 * MEMW)
        if i % 2 == 0:
            p = _norm_matmul(xs, norm1_g[i], rwkv_w_in[j].astype(BF16), TOKEN_TILE)
            p3 = p.reshape(b, t, RWKV_COLS + MEMW)
            wcat = jnp.zeros((LORA_W, 3 * MIX), F32)
            wcat = wcat.at[0:64, 0:MIX].set(rwkv_w2[j])
            wcat = wcat.at[64:128, MIX:2 * MIX].set(rwkv_a2[j])
            wcat = wcat.at[128:256, 2 * MIX:].set(rwkv_g2[j])
            y = _rwkv(p3, rwkv_mu[j], wcat.astype(BF16), rwkv_w0[j], rwkv_a0[j], rwkv_k_k[j],
                      rwkv_k_a[j], rwkv_r_k[j].reshape(MIX), rwkv_lnx_w[j], rwkv_lnx_b[j], RWKV_TILE)
            q_block = RWKV_COLS // MEMW
        else:
            p = _norm_matmul(xs, norm1_g[i], gmlp_w_in[j].astype(BF16), TOKEN_TILE)
            p3 = p.reshape(b, t, 2 * MIX + MEMW)
            bias = jnp.repeat(jnp.transpose(gmlp_b_s[j]), HEAD, axis=1)
            y = _gmlp(p3, gmlp_v_ln_g[j], gmlp_v_ln_b[j], gmlp_w_s[j], bias, GMLP_TILE)
            q_block = 2 * MIX // MEMW
        o = _mem_attn(p3, q_block, kv, TOKEN_TILE)
        wo = w_out[i].astype(BF16)
        xs = _mixer_out(xs, y.reshape(n_tok, MIX), o.reshape(n_tok, MEMW), wo[:MIX], wo[MIX:], TOKEN_TILE)
        if i % 2 == 0:
            xs = _ffn(xs, norm2_g[i], ffn_w_gu[j].astype(BF16), ffn_w_down[j].astype(BF16), TOKEN_TILE)
        else:
            last = i == depth - 1
            xs = _moe(xs, norm2_g[i], moe_router[j], moe_w_gu[j].astype(BF16),
                      moe_w_down[j].astype(BF16), final_norm_g if last else None,
                      TOKEN_TILE, ROW_COPY_TILE, EXPERT_TILE)
    if depth % 2 == 1:
        xs = _final_norm(xs, final_norm_g, TOKEN_TILE)
    return xs.reshape(b, t, d)
```

```python
import functools
import math

import jax
import jax.numpy as jnp
from jax import lax
from jax.experimental import pallas as pl
from jax.experimental.pallas import tpu as pltpu

F32 = jnp.float32
BF16 = jnp.bfloat16

D_MODEL = 1024
HEAD = 64
MIX = 768
MIX_HEADS = MIX // HEAD
MEMW = 256
MEM_HEADS = 4
N_MEM = 256
LORA_W = 256
RWKV_COLS = 3 * MIX + LORA_W
GCHUNK = 128
D_FF = 2816
N_EXPERTS = 8
RMS_EPS = 1e-6
GN_EPS = 64e-5
LN_EPS = 1e-5

LANES = 128
PAIRS = MIX // LANES
SCAN_CHUNK = 64
PREPARE_CHUNKS = 2
VMEM_LIMIT = 56 * 1024 * 1024

TOKEN_TILE = 512
RWKV_TILE = 512
GMLP_TILE = 256
FFN_TILE = 256
ROW_COPY_TILE = 512
EXPERT_TILE = 512


def _cparams(sem):
    return pltpu.CompilerParams(dimension_semantics=sem, vmem_limit_bytes=VMEM_LIMIT)


def _rms(x, g):
    ms = jnp.mean(x * x, axis=-1, keepdims=True)
    return x * lax.rsqrt(ms + RMS_EPS) * g


def _dot(a, b):
    return jnp.dot(a, b, preferred_element_type=F32)


def _dot_nt(a, b):
    return lax.dot_general(a, b, (((1,), (1,)), ((), ())), preferred_element_type=F32)


def _dot_tn(a, b):
    return lax.dot_general(a, b, (((0,), (0,)), ((), ())), preferred_element_type=F32)


def _dot_split(x, m2_bf16):
    hi = x.astype(BF16)
    lo = (x - hi.astype(F32)).astype(BF16)
    return _dot(jnp.concatenate([hi, lo], axis=1), m2_bf16)


def _split_dot(m2_bf16, x):
    hi = x.astype(BF16)
    lo = (x - hi.astype(F32)).astype(BF16)
    return _dot(m2_bf16, jnp.concatenate([hi, lo], axis=0))


def _norm_matmul_kernel(x_ref, g_ref, w_ref, o_ref):
    n = _rms(x_ref[...], g_ref[...]).astype(BF16)
    o_ref[...] = _dot(n, w_ref[...]).astype(o_ref.dtype)


def _norm_matmul(x, g, w, tm):
    m, k = x.shape
    n = w.shape[1]
    return pl.pallas_call(
        _norm_matmul_kernel,
        grid=(m // tm,),
        in_specs=[
            pl.BlockSpec((tm, k), lambda i: (i, 0)),
            pl.BlockSpec((1, k), lambda i: (0, 0)),
            pl.BlockSpec((k, n), lambda i: (0, 0)),
        ],
        out_specs=pl.BlockSpec((tm, n), lambda i: (i, 0)),
        out_shape=jax.ShapeDtypeStruct((m, n), BF16),
        compiler_params=_cparams(("parallel",)),
        name="norm_matmul",
    )(x, g.reshape(1, k), w)


def _mem_attn_kernel(q_ref, kv_ref, o_ref):
    q = q_ref[0]
    kv = kv_ref[0]
    k = kv[:, :MEMW].astype(BF16)
    v = kv[:, MEMW:].astype(BF16)
    lane = lax.broadcasted_iota(jnp.int32, (1, MEMW), 1)
    acc = jnp.zeros(q.shape, F32)
    for h in range(MEM_HEADS):
        m = (lane >= h * HEAD) & (lane < (h + 1) * HEAD)
        qh = jnp.where(m, q, 0.0).astype(BF16)
        s = _dot_nt(qh, k) * (1.0 / math.sqrt(HEAD))
        s = s - jnp.max(s, axis=-1, keepdims=True)
        p = jnp.exp(s)
        l = jnp.sum(p, axis=-1, keepdims=True)
        pv = _dot(p.astype(BF16), v)
        acc = acc + jnp.where(m, pv / l, 0.0)
    o_ref[0] = acc.astype(o_ref.dtype)


def _mem_attn(p3, q_block, kv3, tt):
    b, t, _ = p3.shape
    return pl.pallas_call(
        _mem_attn_kernel,
        grid=(b, t // tt),
        in_specs=[
            pl.BlockSpec((1, tt, MEMW), lambda i, j: (i, j, q_block)),
            pl.BlockSpec((1, N_MEM, 2 * MEMW), lambda i, j: (i, 0, 0)),
        ],
        out_specs=pl.BlockSpec((1, tt, MEMW), lambda i, j: (i, j, 0)),
        out_shape=jax.ShapeDtypeStruct((b, t, MEMW), BF16),
        compiler_params=_cparams(("parallel", "parallel")),
        name="mem_attn",
    )(p3, kv3)


def _mixer_out_kernel(x_ref, y_ref, o_ref, wy_ref, wo_ref, out_ref):
    out_ref[...] = (x_ref[...]
                    + _dot(y_ref[...], wy_ref[...])
                    + _dot(o_ref[...], wo_ref[...]))


def _mixer_out(x, y, o, wy, wo, tm):
    m, d = x.shape
    return pl.pallas_call(
        _mixer_out_kernel,
        grid=(m // tm,),
        in_specs=[
            pl.BlockSpec((tm, d), lambda i: (i, 0)),
            pl.BlockSpec((tm, MIX), lambda i: (i, 0)),
            pl.BlockSpec((tm, MEMW), lambda i: (i, 0)),
            pl.BlockSpec((MIX, d), lambda i: (0, 0)),
            pl.BlockSpec((MEMW, d), lambda i: (0, 0)),
        ],
        out_specs=pl.BlockSpec((tm, d), lambda i: (i, 0)),
        out_shape=jax.ShapeDtypeStruct((m, d), F32),
        compiler_params=_cparams(("parallel",)),
        name="mixer_out",
    )(x, y, o, wy, wo)


def _swiglu(n, wg_ref, wu_ref, wd_ref):
    gate = _dot(n, wg_ref[...])
    up = _dot(n, wu_ref[...])
    h = gate * jax.nn.sigmoid(gate) * up
    return _dot(h.astype(BF16), wd_ref[...])


def _ffn_kernel(x_ref, g_ref, wg_ref, wu_ref, wd_ref, *refs):
    narrow = len(refs) > 1
    out_ref = refs[2] if narrow else refs[0]
    if narrow:
        refs[3][...] = refs[0][...].astype(BF16)
    x = x_ref[...]
    out_ref[...] = x + _swiglu(_rms(x, g_ref[...]).astype(BF16), wg_ref, wu_ref, wd_ref)
    if narrow:
        refs[4][...] = refs[1][...].astype(BF16)


def _ffn(x, g, w_gu, w_down, tm, narrow=()):
    m, d = x.shape
    steps = m // tm
    resident = pl.Buffered(1)
    row_blocks = [pl.BlockSpec((a.shape[0] // steps, a.shape[1]), lambda i: (i, 0)) for a in narrow]
    out = pl.pallas_call(
        _ffn_kernel,
        grid=(steps,),
        in_specs=[
            pl.BlockSpec((tm, d), lambda i: (i, 0)),
            pl.BlockSpec((1, d), lambda i: (0, 0)),
            pl.BlockSpec((d, D_FF), lambda i: (0, 0), pipeline_mode=resident),
            pl.BlockSpec((d, D_FF), lambda i: (0, 1), pipeline_mode=resident),
            pl.BlockSpec((D_FF, d), lambda i: (0, 0), pipeline_mode=resident),
        ] + row_blocks,
        out_specs=[pl.BlockSpec((tm, d), lambda i: (i, 0))] + row_blocks,
        out_shape=[jax.ShapeDtypeStruct((m, d), F32)]
        + [jax.ShapeDtypeStruct(a.shape, BF16) for a in narrow],
        compiler_params=_cparams(("parallel",)),
        name="ffn",
    )(x, g.reshape(1, d), w_gu, w_gu, w_down, *narrow)
    return out[0], tuple(out[1:])


ROUTE_E1, ROUTE_E2, ROUTE_R1, ROUTE_R2, ROUTE_G1, ROUTE_G2 = range(6)
ROUTE_FIELDS = 8


def _route_kernel(x_ref, g_ref, wr_ref, tril_ref, route_ref, routet_ref, cnt_ref, carry_scr):
    @pl.when(pl.program_id(0) == 0)
    def _():
        carry_scr[...] = jnp.zeros_like(carry_scr)

    n = _rms(x_ref[...], g_ref[...])
    n_hi = n.astype(BF16)
    n_lo = (n - n_hi.astype(F32)).astype(BF16)
    logits = _dot(jnp.concatenate([n_hi, n_hi, n_lo], axis=1), wr_ref[...])
    lane = lax.broadcasted_iota(jnp.int32, logits.shape, 1)
    neg = jnp.float32(-jnp.inf)
    lg = jnp.where(lane < N_EXPERTS, logits, neg)
    m1 = jnp.max(lg, axis=-1, keepdims=True)
    i1 = jnp.min(jnp.where(lg == m1, lane, LANES), axis=-1, keepdims=True)
    lg2 = jnp.where(lane == i1, neg, lg)
    m2 = jnp.max(lg2, axis=-1, keepdims=True)
    i2 = jnp.min(jnp.where(lg2 == m2, lane, LANES), axis=-1, keepdims=True)
    e2 = jnp.exp(m2 - m1)
    g1 = 1.0 / (1.0 + e2)
    g2 = e2 / (1.0 + e2)
    chosen = jnp.where(lane == i1, 1.0, jnp.where(lane == i2, 1.0, 0.0))
    before = _dot(tril_ref[...], chosen.astype(BF16)) + carry_scr[...]
    r1 = jnp.sum(jnp.where(lane == i1, before, 0.0), axis=-1, keepdims=True)
    r2 = jnp.sum(jnp.where(lane == i2, before, 0.0), axis=-1, keepdims=True)
    carry_scr[...] += jnp.sum(chosen, axis=0, keepdims=True)
    cnt_ref[...] = carry_scr[...]
    fields = (i1.astype(F32), i2.astype(F32), r1, r2, g1, g2)
    route = jnp.zeros(logits.shape, F32)
    for idx, val in enumerate(fields):
        route = jnp.where(lane == idx, val, route)
    route_ref[...] = route
    routet_ref[...] = jnp.transpose(route)[:ROUTE_FIELDS]


def _route(x, g, wr_split, tm):
    m, d = x.shape
    idx = jnp.arange(tm)
    tril = (idx[:, None] > idx[None, :]).astype(BF16)
    return pl.pallas_call(
        _route_kernel,
        grid=(m // tm,),
        in_specs=[
            pl.BlockSpec((tm, d), lambda i: (i, 0)),
            pl.BlockSpec((1, d), lambda i: (0, 0)),
            pl.BlockSpec((3 * d, LANES), lambda i: (0, 0)),
            pl.BlockSpec((tm, tm), lambda i: (0, 0)),
        ],
        out_specs=[pl.BlockSpec((tm, LANES), lambda i: (i, 0)),
                   pl.BlockSpec((ROUTE_FIELDS, tm), lambda i: (0, i)),
                   pl.BlockSpec((1, LANES), lambda i: (0, 0))],
        out_shape=[jax.ShapeDtypeStruct((m, LANES), F32),
                   jax.ShapeDtypeStruct((ROUTE_FIELDS, m), F32),
                   jax.ShapeDtypeStruct((1, LANES), F32)],
        scratch_shapes=[pltpu.VMEM((1, LANES), F32)],
        compiler_params=_cparams(("arbitrary",)),
        name="route",
    )(x, g.reshape(1, d), wr_split, tril)


def _row_copy(src, src_row, dst, dst_row, sem):
    return pltpu.make_async_copy(src.at[pl.ds(src_row, 1)], dst.at[pl.ds(dst_row, 1)], sem)


ROW_LOOP_UNROLL = 8


def _row_loop(tm, body):
    def step(i, carry):
        body(i)
        return carry
    lax.fori_loop(0, tm, step, 0, unroll=ROW_LOOP_UNROLL)


def _dispatch_kernel(p1_ref, p2_ref, fill_ref, x_ref, g_ref, xs_ref, n_scr, zero_scr, sem, fill_sem):
    tm = x_ref.shape[0]
    tile_rows = zero_scr.shape[0]
    step = pl.program_id(0)

    def fill_copy(f):
        return pltpu.make_async_copy(
            zero_scr, xs_ref.at[pl.ds(pl.multiple_of(fill_ref[f] * tile_rows, tile_rows), tile_rows)],
            fill_sem)

    @pl.when(step == 0)
    def _():
        zero_scr[...] = jnp.zeros_like(zero_scr)
        for f in range(fill_ref.shape[0]):
            @pl.when(fill_ref[f] >= 0)
            def _():
                fill_copy(f).start()
        for f in range(fill_ref.shape[0]):
            @pl.when(fill_ref[f] >= 0)
            def _():
                fill_copy(f).wait()

    def copies(s, i):
        slot = s % 2
        tok = s * tm + i
        return (_row_copy(n_scr.at[slot], i, xs_ref, p1_ref[tok], sem.at[slot]),
                _row_copy(n_scr.at[slot], i, xs_ref, p2_ref[tok], sem.at[slot]))

    def start(s):
        def body(i):
            first, second = copies(s, i)
            first.start()
            second.start(priority=1)
        _row_loop(tm, body)

    def wait(s):
        def body(i):
            first, second = copies(s, i)
            first.wait()
            second.wait()
        _row_loop(tm, body)

    n_scr[step % 2] = _rms(x_ref[...], g_ref[...])
    start(step)

    @pl.when(step > 0)
    def _():
        wait(step - 1)

    @pl.when(step == pl.num_programs(0) - 1)
    def _():
        wait(step)


def _dispatch(x, g, pos1, pos2, fill, rows, tm, tile_rows):
    m, d = x.shape
    return pl.pallas_call(
        _dispatch_kernel,
        grid_spec=pltpu.PrefetchScalarGridSpec(
            num_scalar_prefetch=3,
            grid=(m // tm,),
            in_specs=[
                pl.BlockSpec((tm, d), lambda i, p1, p2, fl: (i, 0)),
                pl.BlockSpec((1, d), lambda i, p1, p2, fl: (0, 0)),
            ],
            out_specs=pl.BlockSpec(memory_space=pl.ANY),
            scratch_shapes=[pltpu.VMEM((2, tm, d), F32), pltpu.VMEM((tile_rows, d), F32),
                            pltpu.SemaphoreType.DMA((2,)), pltpu.SemaphoreType.DMA],
        ),
        out_shape=jax.ShapeDtypeStruct((rows, d), F32),
        compiler_params=_cparams(("arbitrary",)),
        name="dispatch",
    )(pos1, pos2, fill, x, g.reshape(1, d))


def _experts_kernel(te_ref, nv_ref, x_ref, wg_ref, wu_ref, wd_ref, y_ref):
    del te_ref
    valid = pl.program_id(0) < nv_ref[0]

    @pl.when(valid)
    def _():
        y_ref[...] = _swiglu(x_ref[...].astype(BF16), wg_ref, wu_ref, wd_ref)

    @pl.when(jnp.logical_not(valid))
    def _():
        y_ref[...] = jnp.zeros_like(y_ref)


def _experts(xs, tile_expert, n_valid, w_gu, w_down, tm):
    rows, d = xs.shape
    return pl.pallas_call(
        _experts_kernel,
        grid_spec=pltpu.PrefetchScalarGridSpec(
            num_scalar_prefetch=2,
            grid=(rows // tm,),
            in_specs=[
                pl.BlockSpec((tm, d), lambda t, te, nv: (t, 0)),
                pl.BlockSpec((None, d, D_FF), lambda t, te, nv: (te[t], 0, 0)),
                pl.BlockSpec((None, d, D_FF), lambda t, te, nv: (te[t], 0, 1)),
                pl.BlockSpec((None, D_FF, d), lambda t, te, nv: (te[t], 0, 0)),
            ],
            out_specs=pl.BlockSpec((tm, d), lambda t, te, nv: (t, 0)),
        ),
        out_shape=jax.ShapeDtypeStruct((rows, d), F32),
        compiler_params=_cparams(("arbitrary",)),
        name="experts",
    )(tile_expert, n_valid, xs, w_gu, w_gu, w_down)


def _combine_kernel(p1_ref, p2_ref, x_ref, route_ref, fg_ref, ys_ref, out_ref, buf, sem, *, final):
    tm = x_ref.shape[0]
    step = pl.program_id(0)

    def copies(s, i):
        slot = s % 2
        tok = s * tm + i
        return (_row_copy(ys_ref, p1_ref[tok], buf.at[slot, 0], i, sem.at[slot]),
                _row_copy(ys_ref, p2_ref[tok], buf.at[slot, 1], i, sem.at[slot]))

    def start(s):
        def body(i):
            first, second = copies(s, i)
            first.start()
            second.start(priority=1)
        _row_loop(tm, body)

    def wait(s):
        def body(i):
            first, second = copies(s, i)
            first.wait()
            second.wait()
        _row_loop(tm, body)

    @pl.when(step == 0)
    def _():
        start(step)

    @pl.when(step + 1 < pl.num_programs(0))
    def _():
        start(step + 1)

    wait(step)
    rows = buf[step % 2]
    route = route_ref[...]
    g1 = route[:, ROUTE_G1:ROUTE_G1 + 1]
    g2 = route[:, ROUTE_G2:ROUTE_G2 + 1]
    out = x_ref[...] + g1 * rows[0] + g2 * rows[1]
    out_ref[...] = _rms(out, fg_ref[...]) if final else out


def _combine(x, route, ys, pos1, pos2, final_g, tm):
    m, d = x.shape
    final = final_g is not None
    fg = final_g.reshape(1, d) if final else jnp.ones((1, d), F32)
    return pl.pallas_call(
        functools.partial(_combine_kernel, final=final),
        grid_spec=pltpu.PrefetchScalarGridSpec(
            num_scalar_prefetch=2,
            grid=(m // tm,),
            in_specs=[
                pl.BlockSpec((tm, d), lambda i, p1, p2: (i, 0)),
                pl.BlockSpec((tm, LANES), lambda i, p1, p2: (i, 0)),
                pl.BlockSpec((1, d), lambda i, p1, p2: (0, 0)),
                pl.BlockSpec(memory_space=pl.ANY),
            ],
            out_specs=pl.BlockSpec((tm, d), lambda i, p1, p2: (i, 0)),
            scratch_shapes=[pltpu.VMEM((2, 2, tm, d), F32), pltpu.SemaphoreType.DMA((2,))],
        ),
        out_shape=jax.ShapeDtypeStruct((m, d), F32),
        compiler_params=_cparams(("arbitrary",)),
        name="combine",
    )(pos1, pos2, x, route, fg, ys)


def _moe(x, g, w_router, w_gu, w_down, final_g, tm_route, tm_rows, tm_expert):
    m, d = x.shape
    wr_pad = jnp.zeros((d, LANES), F32).at[:, :N_EXPERTS].set(w_router)
    wr_hi = wr_pad.astype(BF16)
    wr_lo = (wr_pad - wr_hi.astype(F32)).astype(BF16)
    wr_split = jnp.concatenate([wr_hi, wr_lo, wr_hi], axis=0)
    route, route_t, counts = _route(x, g, wr_split, tm_route)
    counts = counts[0, :N_EXPERTS].astype(jnp.int32)
    tiles = (counts + tm_expert - 1) // tm_expert
    experts = jnp.arange(N_EXPERTS, dtype=jnp.int32)
    tile_end = jnp.sum(jnp.where(experts[None, :] <= experts[:, None], tiles[None, :], 0), axis=1)
    row_start = (tile_end - tiles) * tm_expert

    def slot(e_row, r_row):
        e = route_t[e_row].astype(jnp.int32)
        start = jnp.sum(jnp.where(e[:, None] == experts[None, :], row_start[None, :], 0), axis=1)
        return start + route_t[r_row].astype(jnp.int32)

    pos1 = slot(ROUTE_E1, ROUTE_R1)
    pos2 = slot(ROUTE_E2, ROUTE_R2)
    n_tiles = 2 * m // tm_expert + N_EXPERTS
    tile_ids = jnp.arange(n_tiles, dtype=jnp.int32)
    tile_expert = jnp.minimum(
        jnp.sum((tile_ids[:, None] >= tile_end[None, :]).astype(jnp.int32), axis=1), N_EXPERTS - 1)
    n_valid = tile_end[-1:]
    tail = n_valid + experts
    fill = jnp.concatenate([jnp.where(tiles > 0, tile_end - 1, -1),
                            jnp.where(tail < n_tiles, tail, -1)])
    xs = _dispatch(x, g, pos1, pos2, fill, n_tiles * tm_expert, tm_rows, tm_expert)
    ys = _experts(xs, tile_expert, n_valid, w_gu, w_down, tm_expert)
    return _combine(x, route, ys, pos1, pos2, final_g, tm_rows)


def _final_norm_kernel(x_ref, g_ref, o_ref):
    o_ref[...] = _rms(x_ref[...], g_ref[...])


def _final_norm(x, g, tm):
    m, d = x.shape
    return pl.pallas_call(
        _final_norm_kernel,
        grid=(m // tm,),
        in_specs=[pl.BlockSpec((tm, d), lambda i: (i, 0)), pl.BlockSpec((1, d), lambda i: (0, 0))],
        out_specs=pl.BlockSpec((tm, d), lambda i: (i, 0)),
        out_shape=jax.ShapeDtypeStruct((m, d), F32),
        compiler_params=_cparams(("parallel",)),
        name="final_norm",
    )(x, g.reshape(1, d))


def _gelu(x):
    return 0.5 * x * (1.0 + lax.erf(x * (1.0 / math.sqrt(2.0))))


def _gmlp_kernel(u_ref, v_ref, lng_ref, lnb_ref, ws_ref, bias_ref, y_ref):
    tt = u_ref.shape[1]
    v = _gelu(v_ref[0].astype(F32))
    mu = jnp.mean(v, axis=-1, keepdims=True)
    d = v - mu
    var = jnp.mean(d * d, axis=-1, keepdims=True)
    vn = d * lax.rsqrt(var + LN_EPS) * lng_ref[...] + lnb_ref[...]
    row = lax.broadcasted_iota(jnp.int32, (GCHUNK, GCHUNK), 0)
    col = lax.broadcasted_iota(jnp.int32, (GCHUNK, GCHUNK), 1)
    causal = row >= col
    first_head = lax.broadcasted_iota(jnp.int32, (1, LANES), 1) < HEAD
    ws = [jnp.where(causal, ws_ref[gi], 0.0).astype(BF16) for gi in range(MIX_HEADS)]
    for c in range(tt // GCHUNK):
        rows = slice(c * GCHUNK, (c + 1) * GCHUNK)
        outs = []
        for j in range(PAIRS):
            vp = vn[rows, j * LANES:(j + 1) * LANES].astype(BF16)
            outs.append(jnp.where(first_head, _dot(ws[2 * j], vp), _dot(ws[2 * j + 1], vp)))
        mixed = jnp.concatenate(outs, axis=1) + bias_ref[...]
        u = _gelu(u_ref[0, rows, :].astype(F32))
        y_ref[0, rows, :] = (u * mixed).astype(y_ref.dtype)


def _gmlp(p3, ln_g, ln_b, w_s, bias, tt):
    b, t, _ = p3.shape
    return pl.pallas_call(
        _gmlp_kernel,
        grid=(b, t // tt),
        in_specs=[
            pl.BlockSpec((1, tt, MIX), lambda i, j: (i, j, 0)),
            pl.BlockSpec((1, tt, MIX), lambda i, j: (i, j, 1)),
            pl.BlockSpec((1, MIX), lambda i, j: (0, 0)),
            pl.BlockSpec((1, MIX), lambda i, j: (0, 0)),
            pl.BlockSpec((MIX_HEADS, GCHUNK, GCHUNK), lambda i, j: (0, 0, 0)),
            pl.BlockSpec((GCHUNK, MIX), lambda i, j: (0, 0)),
        ],
        out_specs=pl.BlockSpec((1, tt, MIX), lambda i, j: (i, j, 0)),
        out_shape=jax.ShapeDtypeStruct((b, t, MIX), BF16),
        compiler_params=_cparams(("parallel", "parallel")),
        name="gmlp",
    )(p3, p3, ln_g.reshape(1, MIX), ln_b.reshape(1, MIX), w_s, bias)


def _token_shift(cur, prev_scr, mu):
    rows = cur.shape[0]
    row = lax.broadcasted_iota(jnp.int32, cur.shape, 0)
    prev = jnp.where(row == 0, prev_scr[...], pltpu.roll(cur, 1, axis=0))
    prev_scr[...] = cur[rows - 1:rows, :]
    return cur + (prev - cur) * mu


def _rwkv_kernel(r_ref, k_ref, v_ref, l_ref,
                 mur_ref, muk_ref, muv_ref, mul_ref, wcat_ref,
                 w0_ref, a0_ref, kk_ref, ka_ref, rk_ref, lnw_ref, lnb_ref,
                 jmat_ref, tril_ref,
                 y_ref,
                 pr_scr, pk_scr, pv_scr, pl_scr, state_scr,
                 at_scr, bt_scr, kt_scr, rt_scr, bp_scr, kp_scr, v_scr, pe_scr,
                 rk_scr, g_scr, y_scr,
                 wq_scr, bk_scr, pet_scr, u0_scr, y0_scr):
    tt = r_ref.shape[1]
    L = SCAN_CHUNK

    @pl.when(pl.program_id(1) == 0)
    def _():
        pr_scr[...] = jnp.zeros_like(pr_scr)
        pk_scr[...] = jnp.zeros_like(pk_scr)
        pv_scr[...] = jnp.zeros_like(pv_scr)
        pl_scr[...] = jnp.zeros_like(pl_scr)
        state_scr[...] = jnp.zeros_like(state_scr)

    r = _token_shift(r_ref[0].astype(F32), pr_scr, mur_ref[...])
    k = _token_shift(k_ref[0].astype(F32), pk_scr, muk_ref[...])
    v = _token_shift(v_ref[0].astype(F32), pv_scr, muv_ref[...])
    lo = _token_shift(l_ref[0].astype(F32), pl_scr, mul_ref[...])
    llane = lax.broadcasted_iota(jnp.int32, lo.shape, 1)
    z = jnp.where(llane < 64, jnp.tanh(lo), jnp.where(llane < 128, lo, jax.nn.sigmoid(lo)))
    proj = _dot(z.astype(BF16), wcat_ref[...])
    ld = -math.exp(-0.5) * jax.nn.sigmoid(w0_ref[...] + proj[:, :MIX])
    a = jax.nn.sigmoid(a0_ref[...] + proj[:, MIX:2 * MIX])
    jmat = jmat_ref[...]
    kk = k * kk_ref[...]
    kk_sq = kk * kk
    ss = jnp.concatenate(
        [_dot_split(kk_sq[:, j * LANES:(j + 1) * LANES], jmat) for j in range(PAIRS)], axis=1)
    kk = kk * lax.rsqrt(jnp.maximum(ss, 1e-24))
    k = k * (1.0 + (a - 1.0) * ka_ref[...])
    b = kk * a
    chunk_cum = [_split_dot(tril_ref[...], ld[c * L:(c + 1) * L]) for c in range(tt // L)]
    cum = jnp.concatenate(chunk_cum, axis=0)
    cum_end = jnp.concatenate(
        [jnp.broadcast_to(cc[L - 1:L], (L, MIX)) for cc in chunk_cum], axis=0)
    p_inv = jnp.exp(-cum)
    to_end = jnp.exp(cum_end - cum)
    at_scr[...] = -(kk * jnp.exp(cum - ld))
    bt_scr[...] = b * p_inv
    kt_scr[...] = k * p_inv
    rt_scr[...] = r * jnp.exp(cum)
    bp_scr[...] = b * to_end
    kp_scr[...] = k * to_end
    v_scr[...] = v
    pe_scr[...] = jnp.exp(cum_end)
    rk_scr[...] = r * k * rk_ref[...]
    g_scr[...] = proj[:, 2 * MIX:]

    lane = lax.broadcasted_iota(jnp.int32, (1, LANES), 1)
    h0 = lane < HEAD
    row = lax.broadcasted_iota(jnp.int32, (L, LANES), 0)
    col = lax.broadcasted_iota(jnp.int32, (L, LANES), 1) % HEAD
    strict = row > col
    incl = row >= col
    srow = lax.broadcasted_iota(jnp.int32, (LANES, LANES), 0) < HEAD
    scol = lax.broadcasted_iota(jnp.int32, (LANES, LANES), 1) < HEAD
    same_head = srow == scol
    eye = jnp.where(lax.broadcasted_iota(jnp.int32, (LANES, LANES), 0)
                    == lax.broadcasted_iota(jnp.int32, (LANES, LANES), 1), 1.0, 0.0)
    pairs = range(PAIRS)

    def cat(x, y):
        return jnp.concatenate([x, y], axis=0)

    def first(x):
        return jnp.where(h0, x, 0.0)

    def second(x):
        return jnp.where(h0, 0.0, x)

    def lanes(x, y):
        return jnp.concatenate([x, y], axis=1)

    def tile(ref, c, j):
        return ref[c * L:(c + 1) * L, j * LANES:(j + 1) * LANES]


    def prepare(chunks):
        items = [(c, j) for c in chunks for j in pairs]
        idx = range(len(items))
        at = [tile(at_scr, c, j) for c, j in items]
        bt = [tile(bt_scr, c, j) for c, j in items]
        kt = [tile(kt_scr, c, j) for c, j in items]
        rt = [tile(rt_scr, c, j) for c, j in items]
        vc = [tile(v_scr, c, j) for c, j in items]
        g_0 = [_dot_nt(cat(first(at[i]), first(rt[i])).astype(BF16), cat(bt[i], kt[i]).astype(BF16))
               for i in idx]
        g_1 = [_dot_nt(cat(second(at[i]), second(rt[i])).astype(BF16), cat(kt[i], bt[i]).astype(BF16))
               for i in idx]
        yield
        top0 = [jnp.where(strict, g_0[i][:L], 0.0) for i in idx]
        top1 = [jnp.where(strict, g_1[i][:L], 0.0) for i in idx]
        vv = [cat(vc[i], vc[i]).astype(BF16) for i in idx]
        akv0 = [_dot(second(top0[i]).astype(BF16), vv[i]) for i in idx]
        akv1 = [_dot(first(top1[i]).astype(BF16), vv[i]) for i in idx]
        yield
        akv = [jnp.where(h0, akv0[i], akv1[i]) for i in idx]
        z = [lanes(cat(first(at[i]), second(at[i])), cat(first(akv[i]), second(akv[i])))
             .astype(BF16) for i in idx]
        apow = [cat(first(top0[i]), second(top1[i])) for i in idx]
        tinv = [eye + apow[i] for i in idx]
        apow = [apow[i].astype(BF16) for i in idx]
        for _ in range(5):
            apow = [_dot(apow[i], apow[i]).astype(BF16) for i in idx]
            yield
            tinv = [tinv[i] + _dot(tinv[i].astype(BF16), apow[i]) for i in idx]
        z = [_dot(tinv[i].astype(BF16), z[i]) for i in idx]
        yield
        w = [z[i][:L, :LANES] + z[i][L:, :LANES] for i in idx]
        u0 = [z[i][:L, LANES:] + z[i][L:, LANES:] for i in idx]
        zero = jnp.zeros((L, LANES), F32)
        wu = [lanes(w[i], u0[i]) for i in idx]
        zv = [lanes(zero, vc[i]) for i in idx]
        r_0 = [_dot(jnp.where(incl, g_0[i][L:], 0.0).astype(BF16), cat(wu[i], zv[i]).astype(BF16))
               for i in idx]
        r_1 = [_dot(jnp.where(incl, g_1[i][L:], 0.0).astype(BF16), cat(zv[i], wu[i]).astype(BF16))
               for i in idx]
        yield
        for i, (c, j) in enumerate(items):
            q = rt[i] + jnp.where(h0, r_0[i][:, :LANES], r_1[i][:, :LANES])
            wq_scr[c, j] = cat(w[i], q).astype(BF16)
            u0_scr[c, j] = u0[i]
            y0_scr[c, j] = jnp.where(h0, r_0[i][:, LANES:], r_1[i][:, LANES:])
            bk_scr[c, j] = jnp.transpose(cat(tile(bp_scr, c, j), tile(kp_scr, c, j))).astype(BF16)
            pe_row = pe_scr[c * L:c * L + 1, j * LANES:(j + 1) * LANES]
            pet_scr[c, j] = jnp.transpose(jnp.broadcast_to(pe_row, (LANES, LANES)))
        yield

    def advance(c):
        h = [state_scr[j] for j in pairs]
        wqh = [_dot(wq_scr[c, j], h[j].astype(BF16)) for j in pairs]
        yield
        uv = [cat(wqh[j][:L] + u0_scr[c, j], tile(v_scr, c, j)).astype(BF16) for j in pairs]
        upd = [_dot(bk_scr[c, j], uv[j]) for j in pairs]
        for j in pairs:
            y_scr[c * L:(c + 1) * L, j * LANES:(j + 1) * LANES] = wqh[j][L:] + y0_scr[c, j]
        yield
        for j in pairs:
            state_scr[j] = h[j] * pet_scr[c, j] + jnp.where(same_head, upd[j], 0.0)
        yield

    def chain(gens):
        for gen in gens:
            yield from gen

    waiting = iter(())
    for start in range(0, tt // L, PREPARE_CHUNKS):
        group = range(start, start + PREPARE_CHUNKS)
        for _ in prepare(group):
            next(waiting, None)
        for _ in waiting:
            pass
        waiting = chain([advance(c) for c in group])
    for _ in waiting:
        pass

    outs = []
    for j in pairs:
        cols = slice(j * LANES, (j + 1) * LANES)
        y = y_scr[:, cols]
        mean = _dot_split(y, jmat) * (1.0 / HEAD)
        d = y - mean
        var = _dot_split(d * d, jmat) * (1.0 / HEAD)
        yn = d * lax.rsqrt(var + GN_EPS) * lnw_ref[:, cols] + lnb_ref[:, cols]
        rk = _dot_split(rk_scr[:, cols], jmat)
        outs.append((yn + rk * v_scr[:, cols]) * g_scr[:, cols])
    y_ref[0] = jnp.concatenate(outs, axis=1).astype(y_ref.dtype)


def _rwkv(p3, mu, wcat, w0, a0, k_k, k_a, r_k, lnx_w, lnx_b, tt):
    b, t, _ = p3.shape
    row = lambda x: x.reshape(1, -1)
    head_of = jnp.arange(LANES) // HEAD
    jmat = (head_of[:, None] == head_of[None, :]).astype(BF16)
    jmat = jnp.concatenate([jmat, jmat], axis=0)
    idx = jnp.arange(SCAN_CHUNK)
    tril = (idx[:, None] >= idx[None, :]).astype(BF16)
    tril = jnp.concatenate([tril, tril], axis=1)
    const = lambda shape: pl.BlockSpec(shape, lambda i, j: (0,) * len(shape))
    tile = pltpu.VMEM((tt, MIX), F32)
    return pl.pallas_call(
        _rwkv_kernel,
        grid=(b, t // tt),
        in_specs=[
            pl.BlockSpec((1, tt, MIX), lambda i, j: (i, j, 0)),
            pl.BlockSpec((1, tt, MIX), lambda i, j: (i, j, 1)),
            pl.BlockSpec((1, tt, MIX), lambda i, j: (i, j, 2)),
            pl.BlockSpec((1, tt, LORA_W), lambda i, j: (i, j, 3 * MIX // LORA_W)),
            const((1, MIX)), const((1, MIX)), const((1, MIX)), const((1, LORA_W)),
            const((LORA_W, 3 * MIX)),
            const((1, MIX)), const((1, MIX)), const((1, MIX)), const((1, MIX)),
            const((1, MIX)), const((1, MIX)), const((1, MIX)),
            const((2 * LANES, LANES)), const((SCAN_CHUNK, 2 * SCAN_CHUNK)),
        ],
        out_specs=pl.BlockSpec((1, tt, MIX), lambda i, j: (i, j, 0)),
        out_shape=jax.ShapeDtypeStruct((b, t, MIX), BF16),
        scratch_shapes=[
            pltpu.VMEM((1, MIX), F32), pltpu.VMEM((1, MIX), F32), pltpu.VMEM((1, MIX), F32),
            pltpu.VMEM((1, LORA_W), F32),
            pltpu.VMEM((PAIRS, LANES, LANES), F32),
        ] + [tile] * 11 + [
            pltpu.VMEM((tt // SCAN_CHUNK, PAIRS, LANES, LANES), BF16),
            pltpu.VMEM((tt // SCAN_CHUNK, PAIRS, LANES, LANES), BF16),
            pltpu.VMEM((tt // SCAN_CHUNK, PAIRS, LANES, LANES), F32),
            pltpu.VMEM((tt // SCAN_CHUNK, PAIRS, SCAN_CHUNK, LANES), F32),
            pltpu.VMEM((tt // SCAN_CHUNK, PAIRS, SCAN_CHUNK, LANES), F32),
        ],
        compiler_params=_cparams(("parallel", "arbitrary")),
        name="rwkv7",
    )(p3, p3, p3, p3,
      row(mu[:MIX]), row(mu[MIX:2 * MIX]), row(mu[2 * MIX:3 * MIX]), row(mu[3 * MIX:]),
      wcat, row(w0), row(a0), row(k_k), row(k_a), row(r_k), row(lnx_w), row(lnx_b),
      jmat, tril)


def kernel(x, mem, mem_norm_g, norm1_g, w_kv_mem, w_out, norm2_g, rwkv_w_in, rwkv_mu, rwkv_w0, rwkv_w2, rwkv_a0, rwkv_a2, rwkv_g2, rwkv_k_k, rwkv_k_a, rwkv_r_k, rwkv_lnx_w, rwkv_lnx_b, ffn_w_gu, ffn_w_down, gmlp_w_in, gmlp_v_ln_g, gmlp_v_ln_b, gmlp_w_s, gmlp_b_s, moe_router, moe_w_gu, moe_w_down, final_norm_g):
    b, t, d = x.shape
    n_tok = b * t
    depth = norm1_g.shape[0]
    xs = x.reshape(n_tok, d)
    mem2 = mem.reshape(b * N_MEM, d)
    for i in range(depth):
        j = i // 2
        kv = _norm_matmul(mem2, mem_norm_g, w_kv_mem[i].astype(BF16), TOKEN_TILE).reshape(b, N_MEM, 2 * MEMW)
        if i % 2 == 0:
            p = _norm_matmul(xs, norm1_g[i], rwkv_w_in[j].astype(BF16), TOKEN_TILE)
            p3 = p.reshape(b, t, RWKV_COLS + MEMW)
            wcat = jnp.zeros((LORA_W, 3 * MIX), F32)
            wcat = wcat.at[0:64, 0:MIX].set(rwkv_w2[j])
            wcat = wcat.at[64:128, MIX:2 * MIX].set(rwkv_a2[j])
            wcat = wcat.at[128:256, 2 * MIX:].set(rwkv_g2[j])
            y = _rwkv(p3, rwkv_mu[j], wcat.astype(BF16), rwkv_w0[j], rwkv_a0[j], rwkv_k_k[j],
                      rwkv_k_a[j], rwkv_r_k[j].reshape(MIX), rwkv_lnx_w[j], rwkv_lnx_b[j], RWKV_TILE)
            q_block = RWKV_COLS // MEMW
        else:
            p = _norm_matmul(xs, norm1_g[i], gmlp_w_in[j].astype(BF16), TOKEN_TILE)
            p3 = p.reshape(b, t, 2 * MIX + MEMW)
            bias = jnp.repeat(jnp.transpose(gmlp_b_s[j]), HEAD, axis=1)
            y = _gmlp(p3, gmlp_v_ln_g[j], gmlp_v_ln_b[j], gmlp_w_s[j], bias, GMLP_TILE)
            q_block = 2 * MIX // MEMW
        o = _mem_attn(p3, q_block, kv, TOKEN_TILE)
        wo = w_out[i].astype(BF16)
        xs = _mixer_out(xs, y.reshape(n_tok, MIX), o.reshape(n_tok, MEMW), wo[:MIX], wo[MIX:], TOKEN_TILE)
        if i % 2 == 0:
            pending = ()
            if j < moe_w_gu.shape[0]:
                pending = (moe_w_gu[j].reshape(-1, 2 * D_FF), moe_w_down[j].reshape(-1, d))
            xs, narrowed = _ffn(xs, norm2_g[i], ffn_w_gu[j].astype(BF16), ffn_w_down[j].astype(BF16),
                                FFN_TILE, pending)
        else:
            last = i == depth - 1
            w_gu, w_down = (narrowed[0].reshape(moe_w_gu.shape[1:]),
                            narrowed[1].reshape(moe_w_down.shape[1:]))
            xs = _moe(xs, norm2_g[i], moe_router[j], w_gu, w_down, final_norm_g if last else None,
                      TOKEN_TILE, ROW_COPY_TILE, EXPERT_TILE)
    if depth % 2 == 1:
        xs = _final_norm(xs, final_norm_g, TOKEN_TILE)
    return xs.reshape(b, t, d)
```

```python
import functools
import math

import jax
import jax.numpy as jnp
from jax import lax
from jax.experimental import pallas as pl
from jax.experimental.pallas import tpu as pltpu

F32 = jnp.float32
BF16 = jnp.bfloat16

D_MODEL = 1024
HEAD = 64
MIX = 768
MIX_HEADS = MIX // HEAD
MEMW = 256
MEM_HEADS = 4
N_MEM = 256
LORA_W = 256
RWKV_COLS = 3 * MIX + LORA_W
GCHUNK = 128
D_FF = 2816
N_EXPERTS = 8
RMS_EPS = 1e-6
GN_EPS = 64e-5
LN_EPS = 1e-5

LANES = 128
SUBLANES_BF16 = 16
PAIRS = MIX // LANES
SCAN_CHUNK = 64
PREPARE_CHUNKS = 2
VMEM_LIMIT = 56 * 1024 * 1024

TOKEN_TILE = 512
RWKV_TILE = 512
GMLP_TILE = 256
FFN_TILE = 256
ROW_COPY_TILE = 512
EXPERT_TILE = 512


def _cparams(sem):
    return pltpu.CompilerParams(dimension_semantics=sem, vmem_limit_bytes=VMEM_LIMIT)


def _rms(x, g):
    ms = jnp.mean(x * x, axis=-1, keepdims=True)
    return x * lax.rsqrt(ms + RMS_EPS) * g


def _dot(a, b):
    return jnp.dot(a, b, preferred_element_type=F32)


def _dot_nt(a, b):
    return lax.dot_general(a, b, (((1,), (1,)), ((), ())), preferred_element_type=F32)


def _dot_tn(a, b):
    return lax.dot_general(a, b, (((0,), (0,)), ((), ())), preferred_element_type=F32)


def _dot_split(x, m2_bf16):
    hi = x.astype(BF16)
    lo = (x - hi.astype(F32)).astype(BF16)
    return _dot(jnp.concatenate([hi, lo], axis=1), m2_bf16)


def _split_dot(m2_bf16, x):
    hi = x.astype(BF16)
    lo = (x - hi.astype(F32)).astype(BF16)
    return _dot(m2_bf16, jnp.concatenate([hi, lo], axis=0))


def _norm_matmul_kernel(x_ref, g_ref, w_ref, o_ref):
    n = _rms(x_ref[...], g_ref[...]).astype(BF16)
    o_ref[...] = _dot(n, w_ref[...]).astype(o_ref.dtype)


def _norm_matmul(x, g, w, tm):
    m, k = x.shape
    n = w.shape[1]
    return pl.pallas_call(
        _norm_matmul_kernel,
        grid=(m // tm,),
        in_specs=[
            pl.BlockSpec((tm, k), lambda i: (i, 0)),
            pl.BlockSpec((1, k), lambda i: (0, 0)),
            pl.BlockSpec((k, n), lambda i: (0, 0)),
        ],
        out_specs=pl.BlockSpec((tm, n), lambda i: (i, 0)),
        out_shape=jax.ShapeDtypeStruct((m, n), BF16),
        compiler_params=_cparams(("parallel",)),
        name="norm_matmul",
    )(x, g.reshape(1, k), w)


def _mem_attn(q, kv):
    k = kv[:, :MEMW].astype(BF16)
    v = kv[:, MEMW:].astype(BF16)
    lane = lax.broadcasted_iota(jnp.int32, (1, MEMW), 1)
    acc = jnp.zeros(q.shape, F32)
    for h in range(MEM_HEADS):
        m = (lane >= h * HEAD) & (lane < (h + 1) * HEAD)
        qh = jnp.where(m, q, 0.0).astype(BF16)
        s = _dot_nt(qh, k) * (1.0 / math.sqrt(HEAD))
        s = s - jnp.max(s, axis=-1, keepdims=True)
        p = jnp.exp(s)
        l = jnp.sum(p, axis=-1, keepdims=True)
        pv = _dot(p.astype(BF16), v)
        acc = acc + jnp.where(m, pv / l, 0.0)
    return acc


def _mixer_out_kernel(x_ref, y_ref, q_ref, kv_ref, wy_ref, wo_ref, out_ref):
    o = _mem_attn(q_ref[0], kv_ref[0])
    out_ref[0] = (x_ref[0]
                  + _dot(y_ref[0], wy_ref[...])
                  + _dot(o.astype(BF16), wo_ref[...]))


def _mixer_out(x3, y3, p3, q_block, kv3, wy, wo, tt):
    b, t, d = x3.shape
    return pl.pallas_call(
        _mixer_out_kernel,
        grid=(b, t // tt),
        in_specs=[
            pl.BlockSpec((1, tt, d), lambda i, j: (i, j, 0)),
            pl.BlockSpec((1, tt, MIX), lambda i, j: (i, j, 0)),
            pl.BlockSpec((1, tt, MEMW), lambda i, j: (i, j, q_block)),
            pl.BlockSpec((1, N_MEM, 2 * MEMW), lambda i, j: (i, 0, 0)),
            pl.BlockSpec((MIX, d), lambda i, j: (0, 0)),
            pl.BlockSpec((MEMW, d), lambda i, j: (0, 0)),
        ],
        out_specs=pl.BlockSpec((1, tt, d), lambda i, j: (i, j, 0)),
        out_shape=jax.ShapeDtypeStruct((b, t, d), F32),
        compiler_params=_cparams(("parallel", "parallel")),
        name="mixer_out",
    )(x3, y3, p3, kv3, wy, wo)


def _swiglu(n, wg_ref, wu_ref, wd_ref):
    gate = _dot(n, wg_ref[...])
    up = _dot(n, wu_ref[...])
    h = gate * jax.nn.sigmoid(gate) * up
    return _dot(h.astype(BF16), wd_ref[...])


def _ffn_kernel(x_ref, g_ref, wg_ref, wu_ref, wd_ref, *refs):
    narrow = len(refs) > 1
    out_ref = refs[2] if narrow else refs[0]
    if narrow:
        refs[3][...] = refs[0][...].astype(BF16)
    x = x_ref[...]
    out_ref[...] = x + _swiglu(_rms(x, g_ref[...]).astype(BF16), wg_ref, wu_ref, wd_ref)
    if narrow:
        refs[4][...] = refs[1][...].astype(BF16)


def _ffn(x, g, w_gu, w_down, tm, narrow=()):
    m, d = x.shape
    steps = m // tm
    resident = pl.Buffered(1)
    row_blocks = [pl.BlockSpec((a.shape[0] // steps, a.shape[1]), lambda i: (i, 0)) for a in narrow]
    out = pl.pallas_call(
        _ffn_kernel,
        grid=(steps,),
        in_specs=[
            pl.BlockSpec((tm, d), lambda i: (i, 0)),
            pl.BlockSpec((1, d), lambda i: (0, 0)),
            pl.BlockSpec((d, D_FF), lambda i: (0, 0), pipeline_mode=resident),
            pl.BlockSpec((d, D_FF), lambda i: (0, 1), pipeline_mode=resident),
            pl.BlockSpec((D_FF, d), lambda i: (0, 0), pipeline_mode=resident),
        ] + row_blocks,
        out_specs=[pl.BlockSpec((tm, d), lambda i: (i, 0))] + row_blocks,
        out_shape=[jax.ShapeDtypeStruct((m, d), F32)]
        + [jax.ShapeDtypeStruct(a.shape, BF16) for a in narrow],
        compiler_params=_cparams(("parallel",)),
        name="ffn",
    )(x, g.reshape(1, d), w_gu, w_gu, w_down, *narrow)
    return out[0], tuple(out[1:])


ROUTE_E1, ROUTE_E2, ROUTE_R1, ROUTE_R2, ROUTE_G1, ROUTE_G2 = range(6)
ROUTE_FIELDS = 8


def _route_kernel(x_ref, g_ref, wr_ref, tril_ref, route_ref, routet_ref, cnt_ref, carry_scr):
    @pl.when(pl.program_id(0) == 0)
    def _():
        carry_scr[...] = jnp.zeros_like(carry_scr)

    n = _rms(x_ref[...], g_ref[...])
    n_hi = n.astype(BF16)
    n_lo = (n - n_hi.astype(F32)).astype(BF16)
    logits = _dot(jnp.concatenate([n_hi, n_hi, n_lo], axis=1), wr_ref[...])
    lane = lax.broadcasted_iota(jnp.int32, logits.shape, 1)
    neg = jnp.float32(-jnp.inf)
    lg = jnp.where(lane < N_EXPERTS, logits, neg)
    m1 = jnp.max(lg, axis=-1, keepdims=True)
    i1 = jnp.min(jnp.where(lg == m1, lane, LANES), axis=-1, keepdims=True)
    lg2 = jnp.where(lane == i1, neg, lg)
    m2 = jnp.max(lg2, axis=-1, keepdims=True)
    i2 = jnp.min(jnp.where(lg2 == m2, lane, LANES), axis=-1, keepdims=True)
    e2 = jnp.exp(m2 - m1)
    g1 = 1.0 / (1.0 + e2)
    g2 = e2 / (1.0 + e2)
    chosen = jnp.where(lane == i1, 1.0, jnp.where(lane == i2, 1.0, 0.0))
    before = _dot(tril_ref[...], chosen.astype(BF16)) + carry_scr[...]
    r1 = jnp.sum(jnp.where(lane == i1, before, 0.0), axis=-1, keepdims=True)
    r2 = jnp.sum(jnp.where(lane == i2, before, 0.0), axis=-1, keepdims=True)
    carry_scr[...] += jnp.sum(chosen, axis=0, keepdims=True)
    cnt_ref[...] = carry_scr[...]
    fields = (i1.astype(F32), i2.astype(F32), r1, r2, g1, g2)
    route = jnp.zeros(logits.shape, F32)
    for idx, val in enumerate(fields):
        route = jnp.where(lane == idx, val, route)
    route_ref[...] = route
    routet_ref[...] = jnp.transpose(route)[:ROUTE_FIELDS]


def _route(x, g, wr_split, tm):
    m, d = x.shape
    idx = jnp.arange(tm)
    tril = (idx[:, None] > idx[None, :]).astype(BF16)
    return pl.pallas_call(
        _route_kernel,
        grid=(m // tm,),
        in_specs=[
            pl.BlockSpec((tm, d), lambda i: (i, 0)),
            pl.BlockSpec((1, d), lambda i: (0, 0)),
            pl.BlockSpec((3 * d, LANES), lambda i: (0, 0)),
            pl.BlockSpec((tm, tm), lambda i: (0, 0)),
        ],
        out_specs=[pl.BlockSpec((tm, LANES), lambda i: (i, 0)),
                   pl.BlockSpec((ROUTE_FIELDS, tm), lambda i: (0, i)),
                   pl.BlockSpec((1, LANES), lambda i: (0, 0))],
        out_shape=[jax.ShapeDtypeStruct((m, LANES), F32),
                   jax.ShapeDtypeStruct((ROUTE_FIELDS, m), F32),
                   jax.ShapeDtypeStruct((1, LANES), F32)],
        scratch_shapes=[pltpu.VMEM((1, LANES), F32)],
        compiler_params=_cparams(("arbitrary",)),
        name="route",
    )(x, g.reshape(1, d), wr_split, tril)


def _row_copy(src, src_row, dst, dst_row, sem):
    return pltpu.make_async_copy(src.at[pl.ds(src_row, 1)], dst.at[pl.ds(dst_row, 1)], sem)


ROW_LOOP_UNROLL = 8


def _row_loop(tm, body):
    def step(i, carry):
        body(i)
        return carry
    lax.fori_loop(0, tm, step, 0, unroll=ROW_LOOP_UNROLL)


def _dispatch_kernel(p1_ref, p2_ref, fill_ref, x_ref, g_ref, xs_ref, n_scr, zero_scr, sem, fill_sem):
    tm = x_ref.shape[0]
    tile_rows = zero_scr.shape[0]
    step = pl.program_id(0)

    def fill_copy(f):
        return pltpu.make_async_copy(
            zero_scr, xs_ref.at[pl.ds(pl.multiple_of(fill_ref[f] * tile_rows, tile_rows), tile_rows)],
            fill_sem)

    @pl.when(step == 0)
    def _():
        zero_scr[...] = jnp.zeros_like(zero_scr)
        for f in range(fill_ref.shape[0]):
            @pl.when(fill_ref[f] >= 0)
            def _():
                fill_copy(f).start()
        for f in range(fill_ref.shape[0]):
            @pl.when(fill_ref[f] >= 0)
            def _():
                fill_copy(f).wait()

    def copies(s, i):
        slot = s % 2
        tok = s * tm + i
        return (_row_copy(n_scr.at[slot], i, xs_ref, p1_ref[tok], sem.at[slot]),
                _row_copy(n_scr.at[slot], i, xs_ref, p2_ref[tok], sem.at[slot]))

    def start(s):
        def body(i):
            first, second = copies(s, i)
            first.start()
            second.start(priority=1)
        _row_loop(tm, body)

    def wait(s):
        def body(i):
            first, second = copies(s, i)
            first.wait()
            second.wait()
        _row_loop(tm, body)

    n_scr[step % 2] = _rms(x_ref[...], g_ref[...])
    start(step)

    @pl.when(step > 0)
    def _():
        wait(step - 1)

    @pl.when(step == pl.num_programs(0) - 1)
    def _():
        wait(step)


def _dispatch(x, g, pos1, pos2, fill, rows, tm, tile_rows):
    m, d = x.shape
    return pl.pallas_call(
        _dispatch_kernel,
        grid_spec=pltpu.PrefetchScalarGridSpec(
            num_scalar_prefetch=3,
            grid=(m // tm,),
            in_specs=[
                pl.BlockSpec((tm, d), lambda i, p1, p2, fl: (i, 0)),
                pl.BlockSpec((1, d), lambda i, p1, p2, fl: (0, 0)),
            ],
            out_specs=pl.BlockSpec(memory_space=pl.ANY),
            scratch_shapes=[pltpu.VMEM((2, tm, d), F32), pltpu.VMEM((tile_rows, d), F32),
                            pltpu.SemaphoreType.DMA((2,)), pltpu.SemaphoreType.DMA],
        ),
        out_shape=jax.ShapeDtypeStruct((rows, d), F32),
        compiler_params=_cparams(("arbitrary",)),
        name="dispatch",
    )(pos1, pos2, fill, x, g.reshape(1, d))


def _experts_kernel(te_ref, nv_ref, x_ref, wg_ref, wu_ref, wd_ref, y_ref):
    del te_ref
    valid = pl.program_id(0) < nv_ref[0]

    @pl.when(valid)
    def _():
        y_ref[...] = _swiglu(x_ref[...].astype(BF16), wg_ref, wu_ref, wd_ref)

    @pl.when(jnp.logical_not(valid))
    def _():
        y_ref[...] = jnp.zeros_like(y_ref)


def _experts(xs, tile_expert, n_valid, w_gu, w_down, tm):
    rows, d = xs.shape
    return pl.pallas_call(
        _experts_kernel,
        grid_spec=pltpu.PrefetchScalarGridSpec(
            num_scalar_prefetch=2,
            grid=(rows // tm,),
            in_specs=[
                pl.BlockSpec((tm, d), lambda t, te, nv: (t, 0)),
                pl.BlockSpec((None, d, D_FF), lambda t, te, nv: (te[t], 0, 0)),
                pl.BlockSpec((None, d, D_FF), lambda t, te, nv: (te[t], 0, 1)),
                pl.BlockSpec((None, D_FF, d), lambda t, te, nv: (te[t], 0, 0)),
            ],
            out_specs=pl.BlockSpec((tm, d), lambda t, te, nv: (t, 0)),
        ),
        out_shape=jax.ShapeDtypeStruct((rows, d), F32),
        compiler_params=_cparams(("arbitrary",)),
        name="experts",
    )(tile_expert, n_valid, xs, w_gu, w_gu, w_down)


def _combine_kernel(p1_ref, p2_ref, x_ref, route_ref, fg_ref, ys_ref, out_ref, buf, sem, *, final):
    tm = x_ref.shape[0]
    step = pl.program_id(0)

    def copies(s, i):
        slot = s % 2
        tok = s * tm + i
        return (_row_copy(ys_ref, p1_ref[tok], buf.at[slot, 0], i, sem.at[slot]),
                _row_copy(ys_ref, p2_ref[tok], buf.at[slot, 1], i, sem.at[slot]))

    def start(s):
        def body(i):
            first, second = copies(s, i)
            first.start()
            second.start(priority=1)
        _row_loop(tm, body)

    def wait(s):
        def body(i):
            first, second = copies(s, i)
            first.wait()
            second.wait()
        _row_loop(tm, body)

    @pl.when(step == 0)
    def _():
        start(step)

    @pl.when(step + 1 < pl.num_programs(0))
    def _():
        start(step + 1)

    wait(step)
    rows = buf[step % 2]
    route = route_ref[...]
    g1 = route[:, ROUTE_G1:ROUTE_G1 + 1]
    g2 = route[:, ROUTE_G2:ROUTE_G2 + 1]
    out = x_ref[...] + g1 * rows[0] + g2 * rows[1]
    out_ref[...] = _rms(out, fg_ref[...]) if final else out


def _combine(x, route, ys, pos1, pos2, final_g, tm):
    m, d = x.shape
    final = final_g is not None
    fg = final_g.reshape(1, d) if final else jnp.ones((1, d), F32)
    return pl.pallas_call(
        functools.partial(_combine_kernel, final=final),
        grid_spec=pltpu.PrefetchScalarGridSpec(
            num_scalar_prefetch=2,
            grid=(m // tm,),
            in_specs=[
                pl.BlockSpec((tm, d), lambda i, p1, p2: (i, 0)),
                pl.BlockSpec((tm, LANES), lambda i, p1, p2: (i, 0)),
                pl.BlockSpec((1, d), lambda i, p1, p2: (0, 0)),
                pl.BlockSpec(memory_space=pl.ANY),
            ],
            out_specs=pl.BlockSpec((tm, d), lambda i, p1, p2: (i, 0)),
            scratch_shapes=[pltpu.VMEM((2, 2, tm, d), F32), pltpu.SemaphoreType.DMA((2,))],
        ),
        out_shape=jax.ShapeDtypeStruct((m, d), F32),
        compiler_params=_cparams(("arbitrary",)),
        name="combine",
    )(pos1, pos2, x, route, fg, ys)


def _moe(x, g, w_router, w_gu, w_down, final_g, tm_route, tm_rows, tm_expert):
    m, d = x.shape
    wr_pad = jnp.zeros((d, LANES), F32).at[:, :N_EXPERTS].set(w_router)
    wr_hi = wr_pad.astype(BF16)
    wr_lo = (wr_pad - wr_hi.astype(F32)).astype(BF16)
    wr_split = jnp.concatenate([wr_hi, wr_lo, wr_hi], axis=0)
    route, route_t, counts = _route(x, g, wr_split, tm_route)
    counts = counts[0, :N_EXPERTS].astype(jnp.int32)
    tiles = (counts + tm_expert - 1) // tm_expert
    experts = jnp.arange(N_EXPERTS, dtype=jnp.int32)
    tile_end = jnp.sum(jnp.where(experts[None, :] <= experts[:, None], tiles[None, :], 0), axis=1)
    row_start = (tile_end - tiles) * tm_expert

    def slot(e_row, r_row):
        e = route_t[e_row].astype(jnp.int32)
        start = jnp.sum(jnp.where(e[:, None] == experts[None, :], row_start[None, :], 0), axis=1)
        return start + route_t[r_row].astype(jnp.int32)

    pos1 = slot(ROUTE_E1, ROUTE_R1)
    pos2 = slot(ROUTE_E2, ROUTE_R2)
    n_tiles = 2 * m // tm_expert + N_EXPERTS
    tile_ids = jnp.arange(n_tiles, dtype=jnp.int32)
    tile_expert = jnp.minimum(
        jnp.sum((tile_ids[:, None] >= tile_end[None, :]).astype(jnp.int32), axis=1), N_EXPERTS - 1)
    n_valid = tile_end[-1:]
    tail = n_valid + experts
    fill = jnp.concatenate([jnp.where(tiles > 0, tile_end - 1, -1),
                            jnp.where(tail < n_tiles, tail, -1)])
    xs = _dispatch(x, g, pos1, pos2, fill, n_tiles * tm_expert, tm_rows, tm_expert)
    ys = _experts(xs, tile_expert, n_valid, w_gu, w_down, tm_expert)
    return _combine(x, route, ys, pos1, pos2, final_g, tm_rows)


def _final_norm_kernel(x_ref, g_ref, o_ref):
    o_ref[...] = _rms(x_ref[...], g_ref[...])


def _final_norm(x, g, tm):
    m, d = x.shape
    return pl.pallas_call(
        _final_norm_kernel,
        grid=(m // tm,),
        in_specs=[pl.BlockSpec((tm, d), lambda i: (i, 0)), pl.BlockSpec((1, d), lambda i: (0, 0))],
        out_specs=pl.BlockSpec((tm, d), lambda i: (i, 0)),
        out_shape=jax.ShapeDtypeStruct((m, d), F32),
        compiler_params=_cparams(("parallel",)),
        name="final_norm",
    )(x, g.reshape(1, d))


def _gelu(x):
    return 0.5 * x * (1.0 + lax.erf(x * (1.0 / math.sqrt(2.0))))


def _gmlp_kernel(u_ref, v_ref, lng_ref, lnb_ref, ws_ref, bias_ref, y_ref):
    tt = u_ref.shape[1]
    v = _gelu(v_ref[0].astype(F32))
    mu = jnp.mean(v, axis=-1, keepdims=True)
    d = v - mu
    var = jnp.mean(d * d, axis=-1, keepdims=True)
    vn = d * lax.rsqrt(var + LN_EPS) * lng_ref[...] + lnb_ref[...]
    row = lax.broadcasted_iota(jnp.int32, (GCHUNK, GCHUNK), 0)
    col = lax.broadcasted_iota(jnp.int32, (GCHUNK, GCHUNK), 1)
    causal = row >= col
    first_head = lax.broadcasted_iota(jnp.int32, (1, LANES), 1) < HEAD
    ws = [jnp.where(causal, ws_ref[gi], 0.0).astype(BF16) for gi in range(MIX_HEADS)]
    for c in range(tt // GCHUNK):
        rows = slice(c * GCHUNK, (c + 1) * GCHUNK)
        outs = []
        for j in range(PAIRS):
            vp = vn[rows, j * LANES:(j + 1) * LANES].astype(BF16)
            outs.append(jnp.where(first_head, _dot(ws[2 * j], vp), _dot(ws[2 * j + 1], vp)))
        mixed = jnp.concatenate(outs, axis=1) + bias_ref[...]
        u = _gelu(u_ref[0, rows, :].astype(F32))
        y_ref[0, rows, :] = (u * mixed).astype(y_ref.dtype)


def _gmlp(p3, ln_g, ln_b, w_s, bias, tt):
    b, t, _ = p3.shape
    return pl.pallas_call(
        _gmlp_kernel,
        grid=(b, t // tt),
        in_specs=[
            pl.BlockSpec((1, tt, MIX), lambda i, j: (i, j, 0)),
            pl.BlockSpec((1, tt, MIX), lambda i, j: (i, j, 1)),
            pl.BlockSpec((1, MIX), lambda i, j: (0, 0)),
            pl.BlockSpec((1, MIX), lambda i, j: (0, 0)),
            pl.BlockSpec((MIX_HEADS, GCHUNK, GCHUNK), lambda i, j: (0, 0, 0)),
            pl.BlockSpec((GCHUNK, MIX), lambda i, j: (0, 0)),
        ],
        out_specs=pl.BlockSpec((1, tt, MIX), lambda i, j: (i, j, 0)),
        out_shape=jax.ShapeDtypeStruct((b, t, MIX), BF16),
        compiler_params=_cparams(("parallel", "parallel")),
        name="gmlp",
    )(p3, p3, ln_g.reshape(1, MIX), ln_b.reshape(1, MIX), w_s, bias)


def _rwkv_kernel(r_ref, k_ref, v_ref, l_ref,
                 mur_ref, muk_ref, muv_ref, mul_ref, wcat_ref,
                 w0_ref, a0_ref, kk_ref, ka_ref, rk_ref, lnw_ref, lnb_ref,
                 jmat_ref, tril_ref,
                 y_ref,
                 pr_scr, pk_scr, pv_scr, pl_scr, state_scr,
                 at_scr, bt_scr, kt_scr, rt_scr, bp_scr, kp_scr, v_scr, pe_scr,
                 rk_scr, g_scr, y_scr,
                 wq_scr, bk_scr, pet_scr, u0_scr, y0_scr):
    tt = r_ref.shape[1]
    L = SCAN_CHUNK

    @pl.when(pl.program_id(1) == 0)
    def _():
        pr_scr[...] = jnp.zeros_like(pr_scr)
        pk_scr[...] = jnp.zeros_like(pk_scr)
        pv_scr[...] = jnp.zeros_like(pv_scr)
        pl_scr[...] = jnp.zeros_like(pl_scr)
        state_scr[...] = jnp.zeros_like(state_scr)

    GL = PREPARE_CHUNKS * L
    jmat = jmat_ref[...]

    def shifted(ref, prev_scr, mu_ref, g):
        cur = ref[0, g * GL:(g + 1) * GL, :].astype(F32)
        if g == 0:
            before = prev_scr[...]
        else:
            before = ref[0, g * GL - SUBLANES_BF16:g * GL, :].astype(F32)[SUBLANES_BF16 - 1:]
        row = lax.broadcasted_iota(jnp.int32, cur.shape, 0)
        prev = jnp.where(row == 0, before, pltpu.roll(cur, 1, axis=0))
        return cur + (prev - cur) * mu_ref[...]

    def prologue(g):
        rows = slice(g * GL, (g + 1) * GL)
        r = shifted(r_ref, pr_scr, mur_ref, g)
        k = shifted(k_ref, pk_scr, muk_ref, g)
        v = shifted(v_ref, pv_scr, muv_ref, g)
        lo = shifted(l_ref, pl_scr, mul_ref, g)
        llane = lax.broadcasted_iota(jnp.int32, lo.shape, 1)
        z = jnp.where(llane < 64, jnp.tanh(lo), jnp.where(llane < 128, lo, jax.nn.sigmoid(lo)))
        proj = _dot(z.astype(BF16), wcat_ref[...])
        yield
        ld = -math.exp(-0.5) * jax.nn.sigmoid(w0_ref[...] + proj[:, :MIX])
        a = jax.nn.sigmoid(a0_ref[...] + proj[:, MIX:2 * MIX])
        kk = k * kk_ref[...]
        kk_sq = kk * kk
        ss = jnp.concatenate(
            [_dot_split(kk_sq[:, j * LANES:(j + 1) * LANES], jmat) for j in range(PAIRS)], axis=1)
        chunk_cum = [_split_dot(tril_ref[...], ld[c * L:(c + 1) * L]) for c in range(PREPARE_CHUNKS)]
        yield
        kk = kk * lax.rsqrt(jnp.maximum(ss, 1e-24))
        k = k * (1.0 + (a - 1.0) * ka_ref[...])
        b = kk * a
        cum = jnp.concatenate(chunk_cum, axis=0)
        cum_end = jnp.concatenate(
            [jnp.broadcast_to(cc[L - 1:L], (L, MIX)) for cc in chunk_cum], axis=0)
        p_inv = jnp.exp(-cum)
        to_end = jnp.exp(cum_end - cum)
        at_scr[rows] = -(kk * jnp.exp(cum - ld))
        bt_scr[rows] = b * p_inv
        kt_scr[rows] = k * p_inv
        rt_scr[rows] = r * jnp.exp(cum)
        yield
        bp_scr[rows] = b * to_end
        kp_scr[rows] = k * to_end
        v_scr[rows] = v
        pe_scr[rows] = jnp.exp(cum_end)
        rk_scr[rows] = r * k * rk_ref[...]
        g_scr[rows] = proj[:, 2 * MIX:]
        yield

    lane = lax.broadcasted_iota(jnp.int32, (1, LANES), 1)
    h0 = lane < HEAD
    row = lax.broadcasted_iota(jnp.int32, (L, LANES), 0)
    col = lax.broadcasted_iota(jnp.int32, (L, LANES), 1) % HEAD
    strict = row > col
    incl = row >= col
    srow = lax.broadcasted_iota(jnp.int32, (LANES, LANES), 0) < HEAD
    scol = lax.broadcasted_iota(jnp.int32, (LANES, LANES), 1) < HEAD
    same_head = srow == scol
    eye = jnp.where(lax.broadcasted_iota(jnp.int32, (LANES, LANES), 0)
                    == lax.broadcasted_iota(jnp.int32, (LANES, LANES), 1), 1.0, 0.0)
    pairs = range(PAIRS)

    def cat(x, y):
        return jnp.concatenate([x, y], axis=0)

    def first(x):
        return jnp.where(h0, x, 0.0)

    def second(x):
        return jnp.where(h0, 0.0, x)

    def lanes(x, y):
        return jnp.concatenate([x, y], axis=1)

    def tile(ref, c, j):
        return ref[c * L:(c + 1) * L, j * LANES:(j + 1) * LANES]


    def prepare(chunks):
        items = [(c, j) for c in chunks for j in pairs]
        idx = range(len(items))
        at = [tile(at_scr, c, j) for c, j in items]
        bt = [tile(bt_scr, c, j) for c, j in items]
        kt = [tile(kt_scr, c, j) for c, j in items]
        rt = [tile(rt_scr, c, j) for c, j in items]
        vc = [tile(v_scr, c, j) for c, j in items]
        g_0 = [_dot_nt(cat(first(at[i]), first(rt[i])).astype(BF16), cat(bt[i], kt[i]).astype(BF16))
               for i in idx]
        g_1 = [_dot_nt(cat(second(at[i]), second(rt[i])).astype(BF16), cat(kt[i], bt[i]).astype(BF16))
               for i in idx]
        yield
        top0 = [jnp.where(strict, g_0[i][:L], 0.0) for i in idx]
        top1 = [jnp.where(strict, g_1[i][:L], 0.0) for i in idx]
        vv = [cat(vc[i], vc[i]).astype(BF16) for i in idx]
        akv0 = [_dot(second(top0[i]).astype(BF16), vv[i]) for i in idx]
        akv1 = [_dot(first(top1[i]).astype(BF16), vv[i]) for i in idx]
        yield
        akv = [jnp.where(h0, akv0[i], akv1[i]) for i in idx]
        z = [lanes(cat(first(at[i]), second(at[i])), cat(first(akv[i]), second(akv[i])))
             .astype(BF16) for i in idx]
        apow = [cat(first(top0[i]), second(top1[i])) for i in idx]
        tinv = [eye + apow[i] for i in idx]
        apow = [_dot(apow[i].astype(BF16), apow[i].astype(BF16)) for i in idx]
        yield
        for step in range(5):
            ab = [apow[i].astype(BF16) for i in idx]
            if step < 4:
                both = [_dot(cat(apow[i], tinv[i]).astype(BF16), ab[i]) for i in idx]
                apow = [both[i][:LANES] for i in idx]
                tinv = [tinv[i] + both[i][LANES:] for i in idx]
            else:
                tinv = [tinv[i] + _dot(tinv[i].astype(BF16), ab[i]) for i in idx]
            yield
        z = [_dot(tinv[i].astype(BF16), z[i]) for i in idx]
        yield
        w = [z[i][:L, :LANES] + z[i][L:, :LANES] for i in idx]
        u0 = [z[i][:L, LANES:] + z[i][L:, LANES:] for i in idx]
        zero = jnp.zeros((L, LANES), F32)
        wu = [lanes(w[i], u0[i]) for i in idx]
        zv = [lanes(zero, vc[i]) for i in idx]
        r_0 = [_dot(jnp.where(incl, g_0[i][L:], 0.0).astype(BF16), cat(wu[i], zv[i]).astype(BF16))
               for i in idx]
        r_1 = [_dot(jnp.where(incl, g_1[i][L:], 0.0).astype(BF16), cat(zv[i], wu[i]).astype(BF16))
               for i in idx]
        yield
        for i, (c, j) in enumerate(items):
            q = rt[i] + jnp.where(h0, r_0[i][:, :LANES], r_1[i][:, :LANES])
            wq_scr[c, j] = cat(w[i], q).astype(BF16)
            u0_scr[c, j] = u0[i]
            y0_scr[c, j] = jnp.where(h0, r_0[i][:, LANES:], r_1[i][:, LANES:])
            bk_scr[c, j] = jnp.transpose(cat(tile(bp_scr, c, j), tile(kp_scr, c, j))).astype(BF16)
            pe_row = pe_scr[c * L:c * L + 1, j * LANES:(j + 1) * LANES]
            pet_scr[c, j] = jnp.transpose(jnp.broadcast_to(pe_row, (LANES, LANES)))
        yield

    def advance(c):
        h = [state_scr[j] for j in pairs]
        wqh = [_dot(wq_scr[c, j], h[j].astype(BF16)) for j in pairs]
        yield
        uv = [cat(wqh[j][:L] + u0_scr[c, j], tile(v_scr, c, j)).astype(BF16) for j in pairs]
        upd = [_dot(bk_scr[c, j], uv[j]) for j in pairs]
        for j in pairs:
            y_scr[c * L:(c + 1) * L, j * LANES:(j + 1) * LANES] = wqh[j][L:] + y0_scr[c, j]
        yield
        for j in pairs:
            state_scr[j] = h[j] * pet_scr[c, j] + jnp.where(same_head, upd[j], 0.0)
        yield

    def chain(gens):
        for gen in gens:
            yield from gen

    def epilogue(g):
        rows = slice(g * GL, (g + 1) * GL)
        tiles = [(rows, slice(j * LANES, (j + 1) * LANES)) for j in pairs]
        y = [y_scr[t] for t in tiles]
        mean = [_dot_split(y[j], jmat) * (1.0 / HEAD) for j in pairs]
        rk = [_dot_split(rk_scr[tiles[j]], jmat) for j in pairs]
        yield
        d = [y[j] - mean[j] for j in pairs]
        var = [_dot_split(d[j] * d[j], jmat) * (1.0 / HEAD) for j in pairs]
        yield
        outs = []
        for j in pairs:
            cols = tiles[j][1]
            yn = d[j] * lax.rsqrt(var[j] + GN_EPS) * lnw_ref[:, cols] + lnb_ref[:, cols]
            outs.append((yn + rk[j] * v_scr[tiles[j]]) * g_scr[tiles[j]])
        y_ref[0, rows, :] = jnp.concatenate(outs, axis=1).astype(y_ref.dtype)
        yield

    n_groups = tt // GL
    for _ in prologue(0):
        pass
    behind = iter(())
    for g in range(n_groups):
        ahead = prologue(g + 1) if g + 1 < n_groups else iter(())
        for _ in prepare(range(g * PREPARE_CHUNKS, (g + 1) * PREPARE_CHUNKS)):
            next(behind, None)
            next(ahead, None)
        for _ in ahead:
            pass
        for _ in behind:
            pass
        behind = chain([advance(c) for c in range(g * PREPARE_CHUNKS, (g + 1) * PREPARE_CHUNKS)]
                       + [epilogue(g)])
    for _ in behind:
        pass

    for ref, scr in ((r_ref, pr_scr), (k_ref, pk_scr), (v_ref, pv_scr), (l_ref, pl_scr)):
        scr[...] = ref[0, tt - SUBLANES_BF16:tt, :].astype(F32)[SUBLANES_BF16 - 1:]


def _rwkv(p3, mu, wcat, w0, a0, k_k, k_a, r_k, lnx_w, lnx_b, tt):
    b, t, _ = p3.shape
    row = lambda x: x.reshape(1, -1)
    head_of = jnp.arange(LANES) // HEAD
    jmat = (head_of[:, None] == head_of[None, :]).astype(BF16)
    jmat = jnp.concatenate([jmat, jmat], axis=0)
    idx = jnp.arange(SCAN_CHUNK)
    tril = (idx[:, None] >= idx[None, :]).astype(BF16)
    tril = jnp.concatenate([tril, tril], axis=1)
    const = lambda shape: pl.BlockSpec(shape, lambda i, j: (0,) * len(shape))
    tile = pltpu.VMEM((tt, MIX), F32)
    return pl.pallas_call(
        _rwkv_kernel,
        grid=(b, t // tt),
        in_specs=[
            pl.BlockSpec((1, tt, MIX), lambda i, j: (i, j, 0)),
            pl.BlockSpec((1, tt, MIX), lambda i, j: (i, j, 1)),
            pl.BlockSpec((1, tt, MIX), lambda i, j: (i, j, 2)),
            pl.BlockSpec((1, tt, LORA_W), lambda i, j: (i, j, 3 * MIX // LORA_W)),
            const((1, MIX)), const((1, MIX)), const((1, MIX)), const((1, LORA_W)),
            const((LORA_W, 3 * MIX)),
            const((1, MIX)), const((1, MIX)), const((1, MIX)), const((1, MIX)),
            const((1, MIX)), const((1, MIX)), const((1, MIX)),
            const((2 * LANES, LANES)), const((SCAN_CHUNK, 2 * SCAN_CHUNK)),
        ],
        out_specs=pl.BlockSpec((1, tt, MIX), lambda i, j: (i, j, 0)),
        out_shape=jax.ShapeDtypeStruct((b, t, MIX), BF16),
        scratch_shapes=[
            pltpu.VMEM((1, MIX), F32), pltpu.VMEM((1, MIX), F32), pltpu.VMEM((1, MIX), F32),
            pltpu.VMEM((1, LORA_W), F32),
            pltpu.VMEM((PAIRS, LANES, LANES), F32),
        ] + [tile] * 11 + [
            pltpu.VMEM((tt // SCAN_CHUNK, PAIRS, LANES, LANES), BF16),
            pltpu.VMEM((tt // SCAN_CHUNK, PAIRS, LANES, LANES), BF16),
            pltpu.VMEM((tt // SCAN_CHUNK, PAIRS, LANES, LANES), F32),
            pltpu.VMEM((tt // SCAN_CHUNK, PAIRS, SCAN_CHUNK, LANES), F32),
            pltpu.VMEM((tt // SCAN_CHUNK, PAIRS, SCAN_CHUNK, LANES), F32),
        ],
        compiler_params=_cparams(("parallel", "arbitrary")),
        name="rwkv7",
    )(p3, p3, p3, p3,
      row(mu[:MIX]), row(mu[MIX:2 * MIX]), row(mu[2 * MIX:3 * MIX]), row(mu[3 * MIX:]),
      wcat, row(w0), row(a0), row(k_k), row(k_a), row(r_k), row(lnx_w), row(lnx_b),
      jmat, tril)


def kernel(x, mem, mem_norm_g, norm1_g, w_kv_mem, w_out, norm2_g, rwkv_w_in, rwkv_mu, rwkv_w0, rwkv_w2, rwkv_a0, rwkv_a2, rwkv_g2, rwkv_k_k, rwkv_k_a, rwkv_r_k, rwkv_lnx_w, rwkv_lnx_b, ffn_w_gu, ffn_w_down, gmlp_w_in, gmlp_v_ln_g, gmlp_v_ln_b, gmlp_w_s, gmlp_b_s, moe_router, moe_w_gu, moe_w_down, final_norm_g):
    b, t, d = x.shape
    n_tok = b * t
    depth = norm1_g.shape[0]
    xs = x.reshape(n_tok, d)
    mem2 = mem.reshape(b * N_MEM, d)
    for i in range(depth):
        j = i // 2
        kv = _norm_matmul(mem2, mem_norm_g, w_kv_mem[i].astype(BF16), TOKEN_TILE).reshape(b, N_MEM, 2 * MEMW)
        if i % 2 == 0:
            p = _norm_matmul(xs, norm1_g[i], rwkv_w_in[j].astype(BF16), TOKEN_TILE)
            p3 = p.reshape(b, t, RWKV_COLS + MEMW)
            wcat = jnp.zeros((LORA_W, 3 * MIX), F32)
            wcat = wcat.at[0:64, 0:MIX].set(rwkv_w2[j])
            wcat = wcat.at[64:128, MIX:2 * MIX].set(rwkv_a2[j])
            wcat = wcat.at[128:256, 2 * MIX:].set(rwkv_g2[j])
            y = _rwkv(p3, rwkv_mu[j], wcat.astype(BF16), rwkv_w0[j], rwkv_a0[j], rwkv_k_k[j],
                      rwkv_k_a[j], rwkv_r_k[j].reshape(MIX), rwkv_lnx_w[j], rwkv_lnx_b[j], RWKV_TILE)
            q_block = RWKV_COLS // MEMW
        else:
            p = _norm_matmul(xs, norm1_g[i], gmlp_w_in[j].astype(BF16), TOKEN_TILE)
            p3 = p.reshape(b, t, 2 * MIX + MEMW)
            bias = jnp.repeat(jnp.transpose(gmlp_b_s[j]), HEAD, axis=1)
            y = _gmlp(p3, gmlp_v_ln_g[j], gmlp_v_ln_b[j], gmlp_w_s[j], bias, GMLP_TILE)
            q_block = 2 * MIX // MEMW
        wo = w_out[i].astype(BF16)
        xs = _mixer_out(xs.reshape(b, t, d), y, p3, q_block, kv, wo[:MIX], wo[MIX:],
                        TOKEN_TILE).reshape(n_tok, d)
        if i % 2 == 0:
            pending = ()
            if j < moe_w_gu.shape[0]:
                pending = (moe_w_gu[j].reshape(-1, 2 * D_FF), moe_w_down[j].reshape(-1, d))
            xs, narrowed = _ffn(xs, norm2_g[i], ffn_w_gu[j].astype(BF16), ffn_w_down[j].astype(BF16),
                                FFN_TILE, pending)
        else:
            last = i == depth - 1
            w_gu, w_down = (narrowed[0].reshape(moe_w_gu.shape[1:]),
                            narrowed[1].reshape(moe_w_down.shape[1:]))
            xs = _moe(xs, norm2_g[i], moe_router[j], w_gu, w_down, final_norm_g if last else None,
                      TOKEN_TILE, ROW_COPY_TILE, EXPERT_TILE)
    if depth % 2 == 1:
        xs = _final_norm(xs, final_norm_g, TOKEN_TILE)
    return xs.reshape(b, t, d)
```

```python
import functools
import math

import jax
import jax.numpy as jnp
from jax import lax
from jax.experimental import pallas as pl
from jax.experimental.pallas import tpu as pltpu

F32 = jnp.float32
BF16 = jnp.bfloat16

D_MODEL = 1024
HEAD = 64
MIX = 768
MIX_HEADS = MIX // HEAD
MEMW = 256
MEM_HEADS = 4
N_MEM = 256
LORA_W = 256
RWKV_COLS = 3 * MIX + LORA_W
GCHUNK = 128
D_FF = 2816
N_EXPERTS = 8
RMS_EPS = 1e-6
GN_EPS = 64e-5
LN_EPS = 1e-5

LANES = 128
SUBLANES_BF16 = 16
PAIRS = MIX // LANES
SCAN_CHUNK = 64
PREPARE_CHUNKS = 2
VMEM_LIMIT = 56 * 1024 * 1024

TOKEN_TILE = 512
RWKV_TILE = 512
GMLP_TILE = 512
FFN_TILE = 256
ROW_COPY_TILE = 512
EXPERT_TILE = 512


def _cparams(sem):
    return pltpu.CompilerParams(dimension_semantics=sem, vmem_limit_bytes=VMEM_LIMIT)


def _rms(x, g):
    ms = jnp.mean(x * x, axis=-1, keepdims=True)
    return x * lax.rsqrt(ms + RMS_EPS) * g


def _dot(a, b):
    return jnp.dot(a, b, preferred_element_type=F32)


def _dot_nt(a, b):
    return lax.dot_general(a, b, (((1,), (1,)), ((), ())), preferred_element_type=F32)


def _dot_tn(a, b):
    return lax.dot_general(a, b, (((0,), (0,)), ((), ())), preferred_element_type=F32)


def _split_dot(m2_bf16, x):
    hi = x.astype(BF16)
    lo = (x - hi.astype(F32)).astype(BF16)
    return _dot(m2_bf16, jnp.concatenate([hi, lo], axis=0))


def _norm_matmul_kernel(x_ref, g_ref, w_ref, o_ref):
    n = _rms(x_ref[...], g_ref[...]).astype(BF16)
    o_ref[...] = _dot(n, w_ref[...]).astype(o_ref.dtype)


def _norm_matmul(x, g, w, tm):
    m, k = x.shape
    n = w.shape[1]
    return pl.pallas_call(
        _norm_matmul_kernel,
        grid=(m // tm,),
        in_specs=[
            pl.BlockSpec((tm, k), lambda i: (i, 0)),
            pl.BlockSpec((1, k), lambda i: (0, 0)),
            pl.BlockSpec((k, n), lambda i: (0, 0)),
        ],
        out_specs=pl.BlockSpec((tm, n), lambda i: (i, 0)),
        out_shape=jax.ShapeDtypeStruct((m, n), BF16),
        compiler_params=_cparams(("parallel",)),
        name="norm_matmul",
    )(x, g.reshape(1, k), w)


def _mem_attn(q, kv):
    k = kv[:, :MEMW].astype(BF16)
    v = kv[:, MEMW:].astype(BF16)
    lane = lax.broadcasted_iota(jnp.int32, (1, MEMW), 1)
    acc = jnp.zeros(q.shape, F32)
    for h in range(MEM_HEADS):
        m = (lane >= h * HEAD) & (lane < (h + 1) * HEAD)
        qh = jnp.where(m, q, 0.0).astype(BF16)
        s = _dot_nt(qh, k) * (1.0 / math.sqrt(HEAD))
        s = s - jnp.max(s, axis=-1, keepdims=True)
        p = jnp.exp(s)
        l = jnp.sum(p, axis=-1, keepdims=True)
        pv = _dot(p.astype(BF16), v)
        acc = acc + jnp.where(m, pv / l, 0.0)
    return acc


def _mixer_tile(x, y, q, kv, wy_ref, wo_ref):
    o = _mem_attn(q, kv)
    return x + _dot(y, wy_ref[...]) + _dot(o.astype(BF16), wo_ref[...])


def _swiglu(n, wg_ref, wu_ref, wd_ref):
    gate = _dot(n, wg_ref[...])
    up = _dot(n, wu_ref[...])
    h = gate * jax.nn.sigmoid(gate) * up
    return _dot(h.astype(BF16), wd_ref[...])


def _mixer_ffn_kernel(x_ref, y_ref, q_ref, kv_ref, wy_ref, wo_ref,
                      g_ref, wg_ref, wu_ref, wd_ref, *refs):
    narrow = len(refs) > 1
    out_ref = refs[2] if narrow else refs[0]
    if narrow:
        refs[3][...] = refs[0][...].astype(BF16)
    x = _mixer_tile(x_ref[...], y_ref[...], q_ref[...], kv_ref[0], wy_ref, wo_ref)
    out_ref[...] = x + _swiglu(_rms(x, g_ref[...]).astype(BF16), wg_ref, wu_ref, wd_ref)
    if narrow:
        refs[4][...] = refs[1][...].astype(BF16)


def _mixer_ffn(x, y, p, q_block, kv3, wy, wo, g, w_gu, w_down, tm, narrow=()):
    m, d = x.shape
    steps = m // tm
    per_batch = m // kv3.shape[0] // tm
    resident = pl.Buffered(1)
    row_blocks = [pl.BlockSpec((a.shape[0] // steps, a.shape[1]), lambda i: (i, 0)) for a in narrow]
    out = pl.pallas_call(
        _mixer_ffn_kernel,
        grid=(steps,),
        in_specs=[
            pl.BlockSpec((tm, d), lambda i: (i, 0)),
            pl.BlockSpec((tm, MIX), lambda i: (i, 0)),
            pl.BlockSpec((tm, MEMW), lambda i: (i, q_block)),
            pl.BlockSpec((1, N_MEM, 2 * MEMW), lambda i: (i // per_batch, 0, 0)),
            pl.BlockSpec((MIX, d), lambda i: (0, 0), pipeline_mode=resident),
            pl.BlockSpec((MEMW, d), lambda i: (0, 0), pipeline_mode=resident),
            pl.BlockSpec((1, d), lambda i: (0, 0)),
            pl.BlockSpec((d, D_FF), lambda i: (0, 0), pipeline_mode=resident),
            pl.BlockSpec((d, D_FF), lambda i: (0, 1), pipeline_mode=resident),
            pl.BlockSpec((D_FF, d), lambda i: (0, 0), pipeline_mode=resident),
        ] + row_blocks,
        out_specs=[pl.BlockSpec((tm, d), lambda i: (i, 0))] + row_blocks,
        out_shape=[jax.ShapeDtypeStruct((m, d), F32)]
        + [jax.ShapeDtypeStruct(a.shape, BF16) for a in narrow],
        compiler_params=_cparams(("parallel",)),
        name="mixer_ffn",
    )(x, y, p, kv3, wy, wo, g.reshape(1, d), w_gu, w_gu, w_down, *narrow)
    return out[0], tuple(out[1:])


ROUTE_E1, ROUTE_E2, ROUTE_R1, ROUTE_R2, ROUTE_G1, ROUTE_G2 = range(6)
ROUTE_FIELDS = 8


def _route_tile(x, first, g_ref, wr_ref, tril_ref, route_ref, routet_ref, cnt_ref, carry_scr):
    @pl.when(first)
    def _():
        carry_scr[...] = jnp.zeros_like(carry_scr)

    n = _rms(x, g_ref[...])
    n_hi = n.astype(BF16)
    n_lo = (n - n_hi.astype(F32)).astype(BF16)
    logits = _dot(jnp.concatenate([n_hi, n_hi, n_lo], axis=1), wr_ref[...])
    lane = lax.broadcasted_iota(jnp.int32, logits.shape, 1)
    neg = jnp.float32(-jnp.inf)
    lg = jnp.where(lane < N_EXPERTS, logits, neg)
    m1 = jnp.max(lg, axis=-1, keepdims=True)
    i1 = jnp.min(jnp.where(lg == m1, lane, LANES), axis=-1, keepdims=True)
    lg2 = jnp.where(lane == i1, neg, lg)
    m2 = jnp.max(lg2, axis=-1, keepdims=True)
    i2 = jnp.min(jnp.where(lg2 == m2, lane, LANES), axis=-1, keepdims=True)
    e2 = jnp.exp(m2 - m1)
    g1 = 1.0 / (1.0 + e2)
    g2 = e2 / (1.0 + e2)
    chosen = jnp.where(lane == i1, 1.0, jnp.where(lane == i2, 1.0, 0.0))
    before = _dot(tril_ref[...], chosen.astype(BF16)) + carry_scr[...]
    r1 = jnp.sum(jnp.where(lane == i1, before, 0.0), axis=-1, keepdims=True)
    r2 = jnp.sum(jnp.where(lane == i2, before, 0.0), axis=-1, keepdims=True)
    carry_scr[...] += jnp.sum(chosen, axis=0, keepdims=True)
    cnt_ref[...] = carry_scr[...]
    fields = (i1.astype(F32), i2.astype(F32), r1, r2, g1, g2)
    route = jnp.zeros(logits.shape, F32)
    for idx, val in enumerate(fields):
        route = jnp.where(lane == idx, val, route)
    route_ref[...] = route
    routet_ref[...] = jnp.transpose(route)[:ROUTE_FIELDS]


def _mixer_route_kernel(x_ref, y_ref, q_ref, kv_ref, wy_ref, wo_ref, g_ref, wr_ref, tril_ref,
                        out_ref, route_ref, routet_ref, cnt_ref, carry_scr):
    x = _mixer_tile(x_ref[0], y_ref[0], q_ref[0], kv_ref[0], wy_ref, wo_ref)
    out_ref[0] = x
    first = jnp.logical_and(pl.program_id(0) == 0, pl.program_id(1) == 0)
    _route_tile(x, first, g_ref, wr_ref, tril_ref, route_ref, routet_ref, cnt_ref, carry_scr)


def _mixer_route(x3, y3, p3, q_block, kv3, wy, wo, g, w_router, tt):
    b, t, d = x3.shape
    per_batch = t // tt
    wr_pad = jnp.zeros((d, LANES), F32).at[:, :N_EXPERTS].set(w_router)
    wr_hi = wr_pad.astype(BF16)
    wr_lo = (wr_pad - wr_hi.astype(F32)).astype(BF16)
    wr_split = jnp.concatenate([wr_hi, wr_lo, wr_hi], axis=0)
    idx = jnp.arange(tt)
    tril = (idx[:, None] > idx[None, :]).astype(BF16)
    const = lambda shape: pl.BlockSpec(shape, lambda i, j: (0,) * len(shape))
    return pl.pallas_call(
        _mixer_route_kernel,
        grid=(b, per_batch),
        in_specs=[
            pl.BlockSpec((1, tt, d), lambda i, j: (i, j, 0)),
            pl.BlockSpec((1, tt, MIX), lambda i, j: (i, j, 0)),
            pl.BlockSpec((1, tt, MEMW), lambda i, j: (i, j, q_block)),
            pl.BlockSpec((1, N_MEM, 2 * MEMW), lambda i, j: (i, 0, 0)),
            const((MIX, d)), const((MEMW, d)), const((1, d)), const((3 * d, LANES)), const((tt, tt)),
        ],
        out_specs=[pl.BlockSpec((1, tt, d), lambda i, j: (i, j, 0)),
                   pl.BlockSpec((tt, LANES), lambda i, j: (i * per_batch + j, 0)),
                   pl.BlockSpec((ROUTE_FIELDS, tt), lambda i, j: (0, i * per_batch + j)),
                   const((1, LANES))],
        out_shape=[jax.ShapeDtypeStruct((b, t, d), F32),
                   jax.ShapeDtypeStruct((b * t, LANES), F32),
                   jax.ShapeDtypeStruct((ROUTE_FIELDS, b * t), F32),
                   jax.ShapeDtypeStruct((1, LANES), F32)],
        scratch_shapes=[pltpu.VMEM((1, LANES), F32)],
        compiler_params=_cparams(("arbitrary", "arbitrary")),
        name="mixer_route",
    )(x3, y3, p3, kv3, wy, wo, g.reshape(1, d), wr_split, tril)


def _row_copy(src, src_row, dst, dst_row, sem):
    return pltpu.make_async_copy(src.at[pl.ds(src_row, 1)], dst.at[pl.ds(dst_row, 1)], sem)


ROW_LOOP_UNROLL = 8


def _row_loop(tm, body):
    def step(i, carry):
        body(i)
        return carry
    lax.fori_loop(0, tm, step, 0, unroll=ROW_LOOP_UNROLL)


def _dispatch_kernel(p1_ref, p2_ref, fill_ref, x_ref, g_ref, xs_ref, n_scr, zero_scr, sem, fill_sem):
    tm = x_ref.shape[0]
    tile_rows = zero_scr.shape[0]
    step = pl.program_id(0)

    def fill_copy(f):
        return pltpu.make_async_copy(
            zero_scr, xs_ref.at[pl.ds(pl.multiple_of(fill_ref[f] * tile_rows, tile_rows), tile_rows)],
            fill_sem)

    @pl.when(step == 0)
    def _():
        zero_scr[...] = jnp.zeros_like(zero_scr)
        for f in range(fill_ref.shape[0]):
            @pl.when(fill_ref[f] >= 0)
            def _():
                fill_copy(f).start()
        for f in range(fill_ref.shape[0]):
            @pl.when(fill_ref[f] >= 0)
            def _():
                fill_copy(f).wait()

    def copies(s, i):
        slot = s % 2
        tok = s * tm + i
        return (_row_copy(n_scr.at[slot], i, xs_ref, p1_ref[tok], sem.at[slot]),
                _row_copy(n_scr.at[slot], i, xs_ref, p2_ref[tok], sem.at[slot]))

    def start(s):
        def body(i):
            first, second = copies(s, i)
            first.start()
            second.start(priority=1)
        _row_loop(tm, body)

    def wait(s):
        def body(i):
            first, second = copies(s, i)
            first.wait()
            second.wait()
        _row_loop(tm, body)

    n_scr[step % 2] = _rms(x_ref[...], g_ref[...])
    start(step)

    @pl.when(step > 0)
    def _():
        wait(step - 1)

    @pl.when(step == pl.num_programs(0) - 1)
    def _():
        wait(step)


def _dispatch(x, g, pos1, pos2, fill, rows, tm, tile_rows):
    m, d = x.shape
    return pl.pallas_call(
        _dispatch_kernel,
        grid_spec=pltpu.PrefetchScalarGridSpec(
            num_scalar_prefetch=3,
            grid=(m // tm,),
            in_specs=[
                pl.BlockSpec((tm, d), lambda i, p1, p2, fl: (i, 0)),
                pl.BlockSpec((1, d), lambda i, p1, p2, fl: (0, 0)),
            ],
            out_specs=pl.BlockSpec(memory_space=pl.ANY),
            scratch_shapes=[pltpu.VMEM((2, tm, d), F32), pltpu.VMEM((tile_rows, d), F32),
                            pltpu.SemaphoreType.DMA((2,)), pltpu.SemaphoreType.DMA],
        ),
        out_shape=jax.ShapeDtypeStruct((rows, d), F32),
        compiler_params=_cparams(("arbitrary",)),
        name="dispatch",
    )(pos1, pos2, fill, x, g.reshape(1, d))


def _experts_kernel(te_ref, nv_ref, x_ref, wg_ref, wu_ref, wd_ref, y_ref):
    del te_ref
    valid = pl.program_id(0) < nv_ref[0]

    @pl.when(valid)
    def _():
        y_ref[...] = _swiglu(x_ref[...].astype(BF16), wg_ref, wu_ref, wd_ref)

    @pl.when(jnp.logical_not(valid))
    def _():
        y_ref[...] = jnp.zeros_like(y_ref)


def _experts(xs, tile_expert, n_valid, w_gu, w_down, tm):
    rows, d = xs.shape
    return pl.pallas_call(
        _experts_kernel,
        grid_spec=pltpu.PrefetchScalarGridSpec(
            num_scalar_prefetch=2,
            grid=(rows // tm,),
            in_specs=[
                pl.BlockSpec((tm, d), lambda t, te, nv: (t, 0)),
                pl.BlockSpec((None, d, D_FF), lambda t, te, nv: (te[t], 0, 0)),
                pl.BlockSpec((None, d, D_FF), lambda t, te, nv: (te[t], 0, 1)),
                pl.BlockSpec((None, D_FF, d), lambda t, te, nv: (te[t], 0, 0)),
            ],
            out_specs=pl.BlockSpec((tm, d), lambda t, te, nv: (t, 0)),
        ),
        out_shape=jax.ShapeDtypeStruct((rows, d), F32),
        compiler_params=_cparams(("arbitrary",)),
        name="experts",
    )(tile_expert, n_valid, xs, w_gu, w_gu, w_down)


def _combine_kernel(p1_ref, p2_ref, x_ref, route_ref, fg_ref, ys_ref, out_ref, buf, sem, *, final):
    tm = x_ref.shape[0]
    step = pl.program_id(0)

    def copies(s, i):
        slot = s % 2
        tok = s * tm + i
        return (_row_copy(ys_ref, p1_ref[tok], buf.at[slot, 0], i, sem.at[slot]),
                _row_copy(ys_ref, p2_ref[tok], buf.at[slot, 1], i, sem.at[slot]))

    def start(s):
        def body(i):
            first, second = copies(s, i)
            first.start()
            second.start(priority=1)
        _row_loop(tm, body)

    def wait(s):
        def body(i):
            first, second = copies(s, i)
            first.wait()
            second.wait()
        _row_loop(tm, body)

    @pl.when(step == 0)
    def _():
        start(step)

    @pl.when(step + 1 < pl.num_programs(0))
    def _():
        start(step + 1)

    wait(step)
    rows = buf[step % 2]
    route = route_ref[...]
    g1 = route[:, ROUTE_G1:ROUTE_G1 + 1]
    g2 = route[:, ROUTE_G2:ROUTE_G2 + 1]
    out = x_ref[...] + g1 * rows[0] + g2 * rows[1]
    out_ref[...] = _rms(out, fg_ref[...]) if final else out


def _combine(x, route, ys, pos1, pos2, final_g, tm):
    m, d = x.shape
    final = final_g is not None
    fg = final_g.reshape(1, d) if final else jnp.ones((1, d), F32)
    return pl.pallas_call(
        functools.partial(_combine_kernel, final=final),
        grid_spec=pltpu.PrefetchScalarGridSpec(
            num_scalar_prefetch=2,
            grid=(m // tm,),
            in_specs=[
                pl.BlockSpec((tm, d), lambda i, p1, p2: (i, 0)),
                pl.BlockSpec((tm, LANES), lambda i, p1, p2: (i, 0)),
                pl.BlockSpec((1, d), lambda i, p1, p2: (0, 0)),
                pl.BlockSpec(memory_space=pl.ANY),
            ],
            out_specs=pl.BlockSpec((tm, d), lambda i, p1, p2: (i, 0)),
            scratch_shapes=[pltpu.VMEM((2, 2, tm, d), F32), pltpu.SemaphoreType.DMA((2,))],
        ),
        out_shape=jax.ShapeDtypeStruct((m, d), F32),
        compiler_params=_cparams(("arbitrary",)),
        name="combine",
    )(pos1, pos2, x, route, fg, ys)


def _moe(x, g, routing, w_gu, w_down, final_g, tm_rows, tm_expert):
    m, d = x.shape
    route, route_t, counts = routing
    counts = counts[0, :N_EXPERTS].astype(jnp.int32)
    tiles = (counts + tm_expert - 1) // tm_expert
    experts = jnp.arange(N_EXPERTS, dtype=jnp.int32)
    tile_end = jnp.sum(jnp.where(experts[None, :] <= experts[:, None], tiles[None, :], 0), axis=1)
    row_start = (tile_end - tiles) * tm_expert

    def slot(e_row, r_row):
        e = route_t[e_row].astype(jnp.int32)
        start = jnp.sum(jnp.where(e[:, None] == experts[None, :], row_start[None, :], 0), axis=1)
        return start + route_t[r_row].astype(jnp.int32)

    pos1 = slot(ROUTE_E1, ROUTE_R1)
    pos2 = slot(ROUTE_E2, ROUTE_R2)
    n_tiles = 2 * m // tm_expert + N_EXPERTS
    tile_ids = jnp.arange(n_tiles, dtype=jnp.int32)
    tile_expert = jnp.minimum(
        jnp.sum((tile_ids[:, None] >= tile_end[None, :]).astype(jnp.int32), axis=1), N_EXPERTS - 1)
    n_valid = tile_end[-1:]
    tail = n_valid + experts
    fill = jnp.concatenate([jnp.where(tiles > 0, tile_end - 1, -1),
                            jnp.where(tail < n_tiles, tail, -1)])
    xs = _dispatch(x, g, pos1, pos2, fill, n_tiles * tm_expert, tm_rows, tm_expert)
    ys = _experts(xs, tile_expert, n_valid, w_gu, w_down, tm_expert)
    return _combine(x, route, ys, pos1, pos2, final_g, tm_rows)


def _final_norm_kernel(x_ref, g_ref, o_ref):
    o_ref[...] = _rms(x_ref[...], g_ref[...])


def _final_norm(x, g, tm):
    m, d = x.shape
    return pl.pallas_call(
        _final_norm_kernel,
        grid=(m // tm,),
        in_specs=[pl.BlockSpec((tm, d), lambda i: (i, 0)), pl.BlockSpec((1, d), lambda i: (0, 0))],
        out_specs=pl.BlockSpec((tm, d), lambda i: (i, 0)),
        out_shape=jax.ShapeDtypeStruct((m, d), F32),
        compiler_params=_cparams(("parallel",)),
        name="final_norm",
    )(x, g.reshape(1, d))


def _gelu(x):
    return 0.5 * x * (1.0 + lax.erf(x * (1.0 / math.sqrt(2.0))))


def _gmlp_kernel(u_ref, v_ref, lng_ref, lnb_ref, ws_ref, bias_ref, y_ref):
    tt = u_ref.shape[1]
    v = _gelu(v_ref[0].astype(F32))
    mu = jnp.mean(v, axis=-1, keepdims=True)
    d = v - mu
    var = jnp.mean(d * d, axis=-1, keepdims=True)
    vn = d * lax.rsqrt(var + LN_EPS) * lng_ref[...] + lnb_ref[...]
    row = lax.broadcasted_iota(jnp.int32, (GCHUNK, GCHUNK), 0)
    col = lax.broadcasted_iota(jnp.int32, (GCHUNK, GCHUNK), 1)
    causal = row >= col
    first_head = lax.broadcasted_iota(jnp.int32, (1, LANES), 1) < HEAD
    ws = [jnp.where(causal, ws_ref[gi], 0.0).astype(BF16) for gi in range(MIX_HEADS)]
    for c in range(tt // GCHUNK):
        rows = slice(c * GCHUNK, (c + 1) * GCHUNK)
        outs = []
        for j in range(PAIRS):
            vp = vn[rows, j * LANES:(j + 1) * LANES].astype(BF16)
            outs.append(jnp.where(first_head, _dot(ws[2 * j], vp), _dot(ws[2 * j + 1], vp)))
        mixed = jnp.concatenate(outs, axis=1) + bias_ref[...]
        u = _gelu(u_ref[0, rows, :].astype(F32))
        y_ref[0, rows, :] = (u * mixed).astype(y_ref.dtype)


def _gmlp(p3, ln_g, ln_b, w_s, bias, tt):
    b, t, _ = p3.shape
    return pl.pallas_call(
        _gmlp_kernel,
        grid=(b, t // tt),
        in_specs=[
            pl.BlockSpec((1, tt, MIX), lambda i, j: (i, j, 0)),
            pl.BlockSpec((1, tt, MIX), lambda i, j: (i, j, 1)),
            pl.BlockSpec((1, MIX), lambda i, j: (0, 0)),
            pl.BlockSpec((1, MIX), lambda i, j: (0, 0)),
            pl.BlockSpec((MIX_HEADS, GCHUNK, GCHUNK), lambda i, j: (0, 0, 0)),
            pl.BlockSpec((GCHUNK, MIX), lambda i, j: (0, 0)),
        ],
        out_specs=pl.BlockSpec((1, tt, MIX), lambda i, j: (i, j, 0)),
        out_shape=jax.ShapeDtypeStruct((b, t, MIX), BF16),
        compiler_params=_cparams(("parallel", "parallel")),
        name="gmlp",
    )(p3, p3, ln_g.reshape(1, MIX), ln_b.reshape(1, MIX), w_s, bias)


def _rwkv_kernel(r_ref, k_ref, v_ref, l_ref,
                 mur_ref, muk_ref, muv_ref, mul_ref, wcat_ref,
                 w0_ref, a0_ref, kk_ref, ka_ref, rk_ref, lnw_ref, lnb_ref,
                 tril_ref,
                 y_ref,
                 pr_scr, pk_scr, pv_scr, pl_scr, state_scr,
                 at_scr, bt_scr, kt_scr, rt_scr, bp_scr, kp_scr, v_scr, pe_scr,
                 rk_scr, g_scr, y_scr,
                 wq_scr, bk_scr, pet_scr, u0_scr, y0_scr):
    tt = r_ref.shape[1]
    L = SCAN_CHUNK

    @pl.when(pl.program_id(1) == 0)
    def _():
        pr_scr[...] = jnp.zeros_like(pr_scr)
        pk_scr[...] = jnp.zeros_like(pk_scr)
        pv_scr[...] = jnp.zeros_like(pv_scr)
        pl_scr[...] = jnp.zeros_like(pl_scr)
        state_scr[...] = jnp.zeros_like(state_scr)

    GL = PREPARE_CHUNKS * L
    pair_h0 = lax.broadcasted_iota(jnp.int32, (1, LANES), 1) < HEAD

    def head_sums(x):
        s0 = jnp.sum(jnp.where(pair_h0, x, 0.0), axis=-1, keepdims=True)
        s1 = jnp.sum(jnp.where(pair_h0, 0.0, x), axis=-1, keepdims=True)
        return jnp.where(pair_h0, s0, s1)

    def shifted(ref, prev_scr, mu_ref, g):
        cur = ref[0, g * GL:(g + 1) * GL, :].astype(F32)
        if g == 0:
            before = prev_scr[...]
        else:
            before = ref[0, g * GL - SUBLANES_BF16:g * GL, :].astype(F32)[SUBLANES_BF16 - 1:]
        row = lax.broadcasted_iota(jnp.int32, cur.shape, 0)
        prev = jnp.where(row == 0, before, pltpu.roll(cur, 1, axis=0))
        return cur + (prev - cur) * mu_ref[...]

    def prologue(g):
        rows = slice(g * GL, (g + 1) * GL)
        r = shifted(r_ref, pr_scr, mur_ref, g)
        k = shifted(k_ref, pk_scr, muk_ref, g)
        v = shifted(v_ref, pv_scr, muv_ref, g)
        lo = shifted(l_ref, pl_scr, mul_ref, g)
        llane = lax.broadcasted_iota(jnp.int32, lo.shape, 1)
        z = jnp.where(llane < 64, jnp.tanh(lo), jnp.where(llane < 128, lo, jax.nn.sigmoid(lo)))
        proj = _dot(z.astype(BF16), wcat_ref[...])
        yield
        ld = -math.exp(-0.5) * jax.nn.sigmoid(w0_ref[...] + proj[:, :MIX])
        a = jax.nn.sigmoid(a0_ref[...] + proj[:, MIX:2 * MIX])
        kk = k * kk_ref[...]
        kk_sq = kk * kk
        ss = jnp.concatenate(
            [head_sums(kk_sq[:, j * LANES:(j + 1) * LANES]) for j in range(PAIRS)], axis=1)
        chunk_cum = [_split_dot(tril_ref[...], ld[c * L:(c + 1) * L]) for c in range(PREPARE_CHUNKS)]
        yield
        kk = kk * lax.rsqrt(jnp.maximum(ss, 1e-24))
        k = k * (1.0 + (a - 1.0) * ka_ref[...])
        b = kk * a
        cum = jnp.concatenate(chunk_cum, axis=0)
        cum_end = jnp.concatenate(
            [jnp.broadcast_to(cc[L - 1:L], (L, MIX)) for cc in chunk_cum], axis=0)
        p_inv = jnp.exp(-cum)
        to_end = jnp.exp(cum_end - cum)
        at_scr[rows] = -(kk * jnp.exp(cum - ld))
        bt_scr[rows] = b * p_inv
        kt_scr[rows] = k * p_inv
        rt_scr[rows] = r * jnp.exp(cum)
        yield
        bp_scr[rows] = b * to_end
        kp_scr[rows] = k * to_end
        v_scr[rows] = v
        pe_scr[rows] = jnp.exp(cum_end)
        rk_scr[rows] = r * k * rk_ref[...]
        g_scr[rows] = proj[:, 2 * MIX:]
        yield

    lane = lax.broadcasted_iota(jnp.int32, (1, LANES), 1)
    h0 = lane < HEAD
    row = lax.broadcasted_iota(jnp.int32, (L, LANES), 0)
    col = lax.broadcasted_iota(jnp.int32, (L, LANES), 1) % HEAD
    strict = row > col
    incl = row >= col
    srow = lax.broadcasted_iota(jnp.int32, (LANES, LANES), 0) < HEAD
    scol = lax.broadcasted_iota(jnp.int32, (LANES, LANES), 1) < HEAD
    same_head = srow == scol
    eye = jnp.where(lax.broadcasted_iota(jnp.int32, (LANES, LANES), 0)
                    == lax.broadcasted_iota(jnp.int32, (LANES, LANES), 1), 1.0, 0.0)
    pairs = range(PAIRS)

    def cat(x, y):
        return jnp.concatenate([x, y], axis=0)

    def first(x):
        return jnp.where(h0, x, 0.0)

    def second(x):
        return jnp.where(h0, 0.0, x)

    def lanes(x, y):
        return jnp.concatenate([x, y], axis=1)

    def tile(ref, c, j):
        return ref[c * L:(c + 1) * L, j * LANES:(j + 1) * LANES]


    def prepare(chunks):
        items = [(c, j) for c in chunks for j in pairs]
        idx = range(len(items))
        at = [tile(at_scr, c, j) for c, j in items]
        bt = [tile(bt_scr, c, j) for c, j in items]
        kt = [tile(kt_scr, c, j) for c, j in items]
        rt = [tile(rt_scr, c, j) for c, j in items]
        vc = [tile(v_scr, c, j) for c, j in items]
        g_0 = [_dot_nt(cat(first(at[i]), first(rt[i])).astype(BF16), cat(bt[i], kt[i]).astype(BF16))
               for i in idx]
        g_1 = [_dot_nt(cat(second(at[i]), second(rt[i])).astype(BF16), cat(kt[i], bt[i]).astype(BF16))
               for i in idx]
        yield
        top0 = [jnp.where(strict, g_0[i][:L], 0.0) for i in idx]
        top1 = [jnp.where(strict, g_1[i][:L], 0.0) for i in idx]
        vv = [cat(vc[i], vc[i]).astype(BF16) for i in idx]
        akv0 = [_dot(second(top0[i]).astype(BF16), vv[i]) for i in idx]
        akv1 = [_dot(first(top1[i]).astype(BF16), vv[i]) for i in idx]
        yield
        akv = [jnp.where(h0, akv0[i], akv1[i]) for i in idx]
        z = [lanes(cat(first(at[i]), second(at[i])), cat(first(akv[i]), second(akv[i])))
             .astype(BF16) for i in idx]
        apow = [cat(first(top0[i]), second(top1[i])) for i in idx]
        tinv = [eye + apow[i] for i in idx]
        apow = [_dot(apow[i].astype(BF16), apow[i].astype(BF16)) for i in idx]
        yield
        for step in range(5):
            ab = [apow[i].astype(BF16) for i in idx]
            if step < 4:
                both = [_dot(cat(apow[i], tinv[i]).astype(BF16), ab[i]) for i in idx]
                apow = [both[i][:LANES] for i in idx]
                tinv = [tinv[i] + both[i][LANES:] for i in idx]
            else:
                tinv = [tinv[i] + _dot(tinv[i].astype(BF16), ab[i]) for i in idx]
            yield
        z = [_dot(tinv[i].astype(BF16), z[i]) for i in idx]
        yield
        w = [z[i][:L, :LANES] + z[i][L:, :LANES] for i in idx]
        u0 = [z[i][:L, LANES:] + z[i][L:, LANES:] for i in idx]
        zero = jnp.zeros((L, LANES), F32)
        wu = [lanes(w[i], u0[i]) for i in idx]
        zv = [lanes(zero, vc[i]) for i in idx]
        r_0 = [_dot(jnp.where(incl, g_0[i][L:], 0.0).astype(BF16), cat(wu[i], zv[i]).astype(BF16))
               for i in idx]
        r_1 = [_dot(jnp.where(incl, g_1[i][L:], 0.0).astype(BF16), cat(zv[i], wu[i]).astype(BF16))
               for i in idx]
        yield
        for i, (c, j) in enumerate(items):
            q = rt[i] + jnp.where(h0, r_0[i][:, :LANES], r_1[i][:, :LANES])
            wq_scr[c, j] = cat(w[i], q).astype(BF16)
            u0_scr[c, j] = u0[i]
            y0_scr[c, j] = jnp.where(h0, r_0[i][:, LANES:], r_1[i][:, LANES:])
            bk_scr[c, j] = jnp.transpose(cat(tile(bp_scr, c, j), tile(kp_scr, c, j))).astype(BF16)
            pe_row = pe_scr[c * L:c * L + 1, j * LANES:(j + 1) * LANES]
            pet_scr[c, j] = jnp.transpose(jnp.broadcast_to(pe_row, (LANES, LANES)))
        yield

    def advance(c):
        h = [state_scr[j] for j in pairs]
        wqh = [_dot(wq_scr[c, j], h[j].astype(BF16)) for j in pairs]
        yield
        uv = [cat(wqh[j][:L] + u0_scr[c, j], tile(v_scr, c, j)).astype(BF16) for j in pairs]
        upd = [_dot(bk_scr[c, j], uv[j]) for j in pairs]
        for j in pairs:
            y_scr[c * L:(c + 1) * L, j * LANES:(j + 1) * LANES] = wqh[j][L:] + y0_scr[c, j]
        yield
        for j in pairs:
            state_scr[j] = h[j] * pet_scr[c, j] + jnp.where(same_head, upd[j], 0.0)
        yield

    def chain(gens):
        for gen in gens:
            yield from gen

    def epilogue(g):
        rows = slice(g * GL, (g + 1) * GL)
        tiles = [(rows, slice(j * LANES, (j + 1) * LANES)) for j in pairs]
        y = [y_scr[t] for t in tiles]
        mean = [head_sums(y[j]) * (1.0 / HEAD) for j in pairs]
        rk = [head_sums(rk_scr[tiles[j]]) for j in pairs]
        yield
        d = [y[j] - mean[j] for j in pairs]
        var = [head_sums(d[j] * d[j]) * (1.0 / HEAD) for j in pairs]
        yield
        outs = []
        for j in pairs:
            cols = tiles[j][1]
            yn = d[j] * lax.rsqrt(var[j] + GN_EPS) * lnw_ref[:, cols] + lnb_ref[:, cols]
            outs.append((yn + rk[j] * v_scr[tiles[j]]) * g_scr[tiles[j]])
        y_ref[0, rows, :] = jnp.concatenate(outs, axis=1).astype(y_ref.dtype)
        yield

    n_groups = tt // GL
    for _ in prologue(0):
        pass
    behind = iter(())
    for g in range(n_groups):
        ahead = prologue(g + 1) if g + 1 < n_groups else iter(())
        for _ in prepare(range(g * PREPARE_CHUNKS, (g + 1) * PREPARE_CHUNKS)):
            next(behind, None)
            next(ahead, None)
        for _ in ahead:
            pass
        for _ in behind:
            pass
        behind = chain([advance(c) for c in range(g * PREPARE_CHUNKS, (g + 1) * PREPARE_CHUNKS)]
                       + [epilogue(g)])
    for _ in behind:
        pass

    for ref, scr in ((r_ref, pr_scr), (k_ref, pk_scr), (v_ref, pv_scr), (l_ref, pl_scr)):
        scr[...] = ref[0, tt - SUBLANES_BF16:tt, :].astype(F32)[SUBLANES_BF16 - 1:]


def _rwkv(p3, mu, wcat, w0, a0, k_k, k_a, r_k, lnx_w, lnx_b, tt):
    b, t, _ = p3.shape
    row = lambda x: x.reshape(1, -1)
    idx = jnp.arange(SCAN_CHUNK)
    tril = (idx[:, None] >= idx[None, :]).astype(BF16)
    tril = jnp.concatenate([tril, tril], axis=1)
    const = lambda shape: pl.BlockSpec(shape, lambda i, j: (0,) * len(shape))
    tile = pltpu.VMEM((tt, MIX), F32)
    return pl.pallas_call(
        _rwkv_kernel,
        grid=(b, t // tt),
        in_specs=[
            pl.BlockSpec((1, tt, MIX), lambda i, j: (i, j, 0)),
            pl.BlockSpec((1, tt, MIX), lambda i, j: (i, j, 1)),
            pl.BlockSpec((1, tt, MIX), lambda i, j: (i, j, 2)),
            pl.BlockSpec((1, tt, LORA_W), lambda i, j: (i, j, 3 * MIX // LORA_W)),
            const((1, MIX)), const((1, MIX)), const((1, MIX)), const((1, LORA_W)),
            const((LORA_W, 3 * MIX)),
            const((1, MIX)), const((1, MIX)), const((1, MIX)), const((1, MIX)),
            const((1, MIX)), const((1, MIX)), const((1, MIX)),
            const((SCAN_CHUNK, 2 * SCAN_CHUNK)),
        ],
        out_specs=pl.BlockSpec((1, tt, MIX), lambda i, j: (i, j, 0)),
        out_shape=jax.ShapeDtypeStruct((b, t, MIX), BF16),
        scratch_shapes=[
            pltpu.VMEM((1, MIX), F32), pltpu.VMEM((1, MIX), F32), pltpu.VMEM((1, MIX), F32),
            pltpu.VMEM((1, LORA_W), F32),
            pltpu.VMEM((PAIRS, LANES, LANES), F32),
        ] + [tile] * 11 + [
            pltpu.VMEM((tt // SCAN_CHUNK, PAIRS, LANES, LANES), BF16),
            pltpu.VMEM((tt // SCAN_CHUNK, PAIRS, LANES, LANES), BF16),
            pltpu.VMEM((tt // SCAN_CHUNK, PAIRS, LANES, LANES), F32),
            pltpu.VMEM((tt // SCAN_CHUNK, PAIRS, SCAN_CHUNK, LANES), F32),
            pltpu.VMEM((tt // SCAN_CHUNK, PAIRS, SCAN_CHUNK, LANES), F32),
        ],
        compiler_params=_cparams(("parallel", "arbitrary")),
        name="rwkv7",
    )(p3, p3, p3, p3,
      row(mu[:MIX]), row(mu[MIX:2 * MIX]), row(mu[2 * MIX:3 * MIX]), row(mu[3 * MIX:]),
      wcat, row(w0), row(a0), row(k_k), row(k_a), row(r_k), row(lnx_w), row(lnx_b),
      tril)


def kernel(x, mem, mem_norm_g, norm1_g, w_kv_mem, w_out, norm2_g, rwkv_w_in, rwkv_mu, rwkv_w0, rwkv_w2, rwkv_a0, rwkv_a2, rwkv_g2, rwkv_k_k, rwkv_k_a, rwkv_r_k, rwkv_lnx_w, rwkv_lnx_b, ffn_w_gu, ffn_w_down, gmlp_w_in, gmlp_v_ln_g, gmlp_v_ln_b, gmlp_w_s, gmlp_b_s, moe_router, moe_w_gu, moe_w_down, final_norm_g):
    b, t, d = x.shape
    n_tok = b * t
    depth = norm1_g.shape[0]
    xs = x.reshape(n_tok, d)
    mem2 = mem.reshape(b * N_MEM, d)
    for i in range(depth):
        j = i // 2
        kv = _norm_matmul(mem2, mem_norm_g, w_kv_mem[i].astype(BF16), TOKEN_TILE).reshape(b, N_MEM, 2 * MEMW)
        if i % 2 == 0:
            p = _norm_matmul(xs, norm1_g[i], rwkv_w_in[j].astype(BF16), TOKEN_TILE)
            p3 = p.reshape(b, t, RWKV_COLS + MEMW)
            wcat = jnp.zeros((LORA_W, 3 * MIX), F32)
            wcat = wcat.at[0:64, 0:MIX].set(rwkv_w2[j])
            wcat = wcat.at[64:128, MIX:2 * MIX].set(rwkv_a2[j])
            wcat = wcat.at[128:256, 2 * MIX:].set(rwkv_g2[j])
            y = _rwkv(p3, rwkv_mu[j], wcat.astype(BF16), rwkv_w0[j], rwkv_a0[j], rwkv_k_k[j],
                      rwkv_k_a[j], rwkv_r_k[j].reshape(MIX), rwkv_lnx_w[j], rwkv_lnx_b[j], RWKV_TILE)
            q_block = RWKV_COLS // MEMW
        else:
            p = _norm_matmul(xs, norm1_g[i], gmlp_w_in[j].astype(BF16), TOKEN_TILE)
            p3 = p.reshape(b, t, 2 * MIX + MEMW)
            bias = jnp.repeat(jnp.transpose(gmlp_b_s[j]), HEAD, axis=1)
            y = _gmlp(p3, gmlp_v_ln_g[j], gmlp_v_ln_b[j], gmlp_w_s[j], bias, GMLP_TILE)
            q_block = 2 * MIX // MEMW
        wo = w_out[i].astype(BF16)
        if i % 2 == 0:
            pending = ()
            if j < moe_w_gu.shape[0]:
                pending = (moe_w_gu[j].reshape(-1, 2 * D_FF), moe_w_down[j].reshape(-1, d))
            xs, narrowed = _mixer_ffn(xs, y.reshape(n_tok, MIX), p, q_block, kv, wo[:MIX], wo[MIX:],
                                      norm2_g[i], ffn_w_gu[j].astype(BF16),
                                      ffn_w_down[j].astype(BF16), FFN_TILE, pending)
        else:
            last = i == depth - 1
            x3, *routing = _mixer_route(xs.reshape(b, t, d), y, p3, q_block, kv, wo[:MIX], wo[MIX:],
                                        norm2_g[i], moe_router[j], TOKEN_TILE)
            w_gu, w_down = (narrowed[0].reshape(moe_w_gu.shape[1:]),
                            narrowed[1].reshape(moe_w_down.shape[1:]))
            xs = _moe(x3.reshape(n_tok, d), norm2_g[i], routing, w_gu, w_down,
                      final_norm_g if last else None, ROW_COPY_TILE, EXPERT_TILE)
    if depth % 2 == 1:
        xs = _final_norm(xs, final_norm_g, TOKEN_TILE)
    return xs.reshape(b, t, d)
```

```python
import functools
import math

import jax
import jax.numpy as jnp
from jax import lax
from jax.experimental import pallas as pl
from jax.experimental.pallas import tpu as pltpu

F32 = jnp.float32
BF16 = jnp.bfloat16

D_MODEL = 1024
HEAD = 64
MIX = 768
MIX_HEADS = MIX // HEAD
MEMW = 256
MEM_HEADS = 4
N_MEM = 256
LORA_W = 256
RWKV_COLS = 3 * MIX + LORA_W
GCHUNK = 128
D_FF = 2816
N_EXPERTS = 8
RMS_EPS = 1e-6
GN_EPS = 64e-5
LN_EPS = 1e-5

LANES = 128
SUBLANES_BF16 = 16
PAIRS = MIX // LANES
SCAN_CHUNK = 64
PREPARE_CHUNKS = 2
VMEM_LIMIT = 56 * 1024 * 1024

TOKEN_TILE = 512
PROJ_TILE = 1024
MIXER_TILE = 1024
RWKV_TILE = 512
GMLP_TILE = 512
FFN_TILE = 256
ROW_COPY_TILE = 512
EXPERT_TILE = 512


def _cparams(sem):
    return pltpu.CompilerParams(dimension_semantics=sem, vmem_limit_bytes=VMEM_LIMIT)


def _rms(x, g):
    ms = jnp.mean(x * x, axis=-1, keepdims=True)
    return x * lax.rsqrt(ms + RMS_EPS) * g


def _dot(a, b):
    return jnp.dot(a, b, preferred_element_type=F32)


def _dot_nt(a, b):
    return lax.dot_general(a, b, (((1,), (1,)), ((), ())), preferred_element_type=F32)


def _dot_tn(a, b):
    return lax.dot_general(a, b, (((0,), (0,)), ((), ())), preferred_element_type=F32)


def _split_dot(m2_bf16, x):
    hi = x.astype(BF16)
    lo = (x - hi.astype(F32)).astype(BF16)
    return _dot(m2_bf16, jnp.concatenate([hi, lo], axis=0))


def _norm_matmul_kernel(x_ref, g_ref, w_ref, o_ref):
    n = _rms(x_ref[...], g_ref[...]).astype(BF16)
    o_ref[...] = _dot(n, w_ref[...]).astype(o_ref.dtype)


def _norm_matmul(x, g, w, tm):
    m, k = x.shape
    n = w.shape[1]
    return pl.pallas_call(
        _norm_matmul_kernel,
        grid=(m // tm,),
        in_specs=[
            pl.BlockSpec((tm, k), lambda i: (i, 0)),
            pl.BlockSpec((1, k), lambda i: (0, 0)),
            pl.BlockSpec((k, n), lambda i: (0, 0)),
        ],
        out_specs=pl.BlockSpec((tm, n), lambda i: (i, 0)),
        out_shape=jax.ShapeDtypeStruct((m, n), BF16),
        compiler_params=_cparams(("parallel",)),
        name="norm_matmul",
    )(x, g.reshape(1, k), w)


def _mem_attn(q, kv):
    k = kv[:, :MEMW].astype(BF16)
    v = kv[:, MEMW:].astype(BF16)
    lane = lax.broadcasted_iota(jnp.int32, (1, MEMW), 1)
    acc = jnp.zeros(q.shape, F32)
    for h in range(MEM_HEADS):
        m = (lane >= h * HEAD) & (lane < (h + 1) * HEAD)
        qh = jnp.where(m, q, 0.0).astype(BF16)
        s = _dot_nt(qh, k) * (1.0 / math.sqrt(HEAD))
        s = s - jnp.max(s, axis=-1, keepdims=True)
        p = jnp.exp(s)
        l = jnp.sum(p, axis=-1, keepdims=True)
        pv = _dot(p.astype(BF16), v)
        acc = acc + jnp.where(m, pv / l, 0.0)
    return acc


def _mixer_out_kernel(x_ref, y_ref, q_ref, kv_ref, wy_ref, wo_ref, out_ref):
    o = _mem_attn(q_ref[0], kv_ref[0])
    out_ref[0] = (x_ref[0]
                  + _dot(y_ref[0], wy_ref[...])
                  + _dot(o.astype(BF16), wo_ref[...]))


def _mixer_out(x3, y3, p3, q_block, kv3, wy, wo, tt):
    b, t, d = x3.shape
    return pl.pallas_call(
        _mixer_out_kernel,
        grid=(b, t // tt),
        in_specs=[
            pl.BlockSpec((1, tt, d), lambda i, j: (i, j, 0)),
            pl.BlockSpec((1, tt, MIX), lambda i, j: (i, j, 0)),
            pl.BlockSpec((1, tt, MEMW), lambda i, j: (i, j, q_block)),
            pl.BlockSpec((1, N_MEM, 2 * MEMW), lambda i, j: (i, 0, 0)),
            pl.BlockSpec((MIX, d), lambda i, j: (0, 0)),
            pl.BlockSpec((MEMW, d), lambda i, j: (0, 0)),
        ],
        out_specs=pl.BlockSpec((1, tt, d), lambda i, j: (i, j, 0)),
        out_shape=jax.ShapeDtypeStruct((b, t, d), F32),
        compiler_params=_cparams(("parallel", "parallel")),
        name="mixer_out",
    )(x3, y3, p3, kv3, wy, wo)


def _swiglu(n, wg_ref, wu_ref, wd_ref):
    gate = _dot(n, wg_ref[...])
    up = _dot(n, wu_ref[...])
    h = gate * jax.nn.sigmoid(gate) * up
    return _dot(h.astype(BF16), wd_ref[...])


def _ffn_kernel(x_ref, g_ref, wg_ref, wu_ref, wd_ref, *refs):
    narrow = len(refs) > 1
    out_ref = refs[2] if narrow else refs[0]
    if narrow:
        refs[3][...] = refs[0][...].astype(BF16)
    x = x_ref[...]
    out_ref[...] = x + _swiglu(_rms(x, g_ref[...]).astype(BF16), wg_ref, wu_ref, wd_ref)
    if narrow:
        refs[4][...] = refs[1][...].astype(BF16)


def _ffn(x, g, w_gu, w_down, tm, narrow=()):
    m, d = x.shape
    steps = m // tm
    resident = pl.Buffered(1)
    row_blocks = [pl.BlockSpec((a.shape[0] // steps, a.shape[1]), lambda i: (i, 0)) for a in narrow]
    out = pl.pallas_call(
        _ffn_kernel,
        grid=(steps,),
        in_specs=[
            pl.BlockSpec((tm, d), lambda i: (i, 0)),
            pl.BlockSpec((1, d), lambda i: (0, 0)),
            pl.BlockSpec((d, D_FF), lambda i: (0, 0), pipeline_mode=resident),
            pl.BlockSpec((d, D_FF), lambda i: (0, 1), pipeline_mode=resident),
            pl.BlockSpec((D_FF, d), lambda i: (0, 0), pipeline_mode=resident),
        ] + row_blocks,
        out_specs=[pl.BlockSpec((tm, d), lambda i: (i, 0))] + row_blocks,
        out_shape=[jax.ShapeDtypeStruct((m, d), F32)]
        + [jax.ShapeDtypeStruct(a.shape, BF16) for a in narrow],
        compiler_params=_cparams(("parallel",)),
        name="ffn",
    )(x, g.reshape(1, d), w_gu, w_gu, w_down, *narrow)
    return out[0], tuple(out[1:])


ROUTE_E1, ROUTE_E2, ROUTE_R1, ROUTE_R2, ROUTE_G1, ROUTE_G2 = range(6)
ROUTE_FIELDS = 8


def _route_tile(x, first, g_ref, wr_ref, tril_ref, route_ref, routet_ref, cnt_ref, carry_scr):
    @pl.when(first)
    def _():
        carry_scr[...] = jnp.zeros_like(carry_scr)

    n = _rms(x, g_ref[...])
    n_hi = n.astype(BF16)
    n_lo = (n - n_hi.astype(F32)).astype(BF16)
    logits = _dot(jnp.concatenate([n_hi, n_hi, n_lo], axis=1), wr_ref[...])
    lane = lax.broadcasted_iota(jnp.int32, logits.shape, 1)
    neg = jnp.float32(-jnp.inf)
    lg = jnp.where(lane < N_EXPERTS, logits, neg)
    m1 = jnp.max(lg, axis=-1, keepdims=True)
    i1 = jnp.min(jnp.where(lg == m1, lane, LANES), axis=-1, keepdims=True)
    lg2 = jnp.where(lane == i1, neg, lg)
    m2 = jnp.max(lg2, axis=-1, keepdims=True)
    i2 = jnp.min(jnp.where(lg2 == m2, lane, LANES), axis=-1, keepdims=True)
    e2 = jnp.exp(m2 - m1)
    g1 = 1.0 / (1.0 + e2)
    g2 = e2 / (1.0 + e2)
    chosen = jnp.where(lane == i1, 1.0, jnp.where(lane == i2, 1.0, 0.0))
    before = _dot(tril_ref[...], chosen.astype(BF16)) + carry_scr[...]
    r1 = jnp.sum(jnp.where(lane == i1, before, 0.0), axis=-1, keepdims=True)
    r2 = jnp.sum(jnp.where(lane == i2, before, 0.0), axis=-1, keepdims=True)
    carry_scr[...] += jnp.sum(chosen, axis=0, keepdims=True)
    cnt_ref[...] = carry_scr[...]
    fields = (i1.astype(F32), i2.astype(F32), r1, r2, g1, g2)
    route = jnp.zeros(logits.shape, F32)
    for idx, val in enumerate(fields):
        route = jnp.where(lane == idx, val, route)
    route_ref[...] = route
    routet_ref[...] = jnp.transpose(route)[:ROUTE_FIELDS]


def _route_kernel(x_ref, g_ref, wr_ref, tril_ref, route_ref, routet_ref, cnt_ref, carry_scr):
    _route_tile(x_ref[...], pl.program_id(0) == 0, g_ref, wr_ref, tril_ref,
                route_ref, routet_ref, cnt_ref, carry_scr)


def _route(x, g, w_router, tm):
    m, d = x.shape
    wr_pad = jnp.zeros((d, LANES), F32).at[:, :N_EXPERTS].set(w_router)
    wr_hi = wr_pad.astype(BF16)
    wr_lo = (wr_pad - wr_hi.astype(F32)).astype(BF16)
    wr_split = jnp.concatenate([wr_hi, wr_lo, wr_hi], axis=0)
    idx = jnp.arange(tm)
    tril = (idx[:, None] > idx[None, :]).astype(BF16)
    return pl.pallas_call(
        _route_kernel,
        grid=(m // tm,),
        in_specs=[
            pl.BlockSpec((tm, d), lambda i: (i, 0)),
            pl.BlockSpec((1, d), lambda i: (0, 0)),
            pl.BlockSpec((3 * d, LANES), lambda i: (0, 0)),
            pl.BlockSpec((tm, tm), lambda i: (0, 0)),
        ],
        out_specs=[pl.BlockSpec((tm, LANES), lambda i: (i, 0)),
                   pl.BlockSpec((ROUTE_FIELDS, tm), lambda i: (0, i)),
                   pl.BlockSpec((1, LANES), lambda i: (0, 0))],
        out_shape=[jax.ShapeDtypeStruct((m, LANES), F32),
                   jax.ShapeDtypeStruct((ROUTE_FIELDS, m), F32),
                   jax.ShapeDtypeStruct((1, LANES), F32)],
        scratch_shapes=[pltpu.VMEM((1, LANES), F32)],
        compiler_params=_cparams(("arbitrary",)),
        name="route",
    )(x, g.reshape(1, d), wr_split, tril)


def _row_copy(src, src_row, dst, dst_row, sem):
    return pltpu.make_async_copy(src.at[pl.ds(src_row, 1)], dst.at[pl.ds(dst_row, 1)], sem)


ROW_LOOP_UNROLL = 8


def _row_loop(tm, body):
    def step(i, carry):
        body(i)
        return carry
    lax.fori_loop(0, tm, step, 0, unroll=ROW_LOOP_UNROLL)


def _dispatch_kernel(p1_ref, p2_ref, fill_ref, x_ref, g_ref, xs_ref, n_scr, zero_scr, sem, fill_sem):
    tm = x_ref.shape[0]
    tile_rows = zero_scr.shape[0]
    step = pl.program_id(0)

    def fill_copy(f):
        return pltpu.make_async_copy(
            zero_scr, xs_ref.at[pl.ds(pl.multiple_of(fill_ref[f] * tile_rows, tile_rows), tile_rows)],
            fill_sem)

    @pl.when(step == 0)
    def _():
        zero_scr[...] = jnp.zeros_like(zero_scr)
        for f in range(fill_ref.shape[0]):
            @pl.when(fill_ref[f] >= 0)
            def _():
                fill_copy(f).start()
        for f in range(fill_ref.shape[0]):
            @pl.when(fill_ref[f] >= 0)
            def _():
                fill_copy(f).wait()

    def copies(s, i):
        slot = s % 2
        tok = s * tm + i
        return (_row_copy(n_scr.at[slot], i, xs_ref, p1_ref[tok], sem.at[slot]),
                _row_copy(n_scr.at[slot], i, xs_ref, p2_ref[tok], sem.at[slot]))

    def start(s):
        def body(i):
            first, second = copies(s, i)
            first.start()
            second.start(priority=1)
        _row_loop(tm, body)

    def wait(s):
        def body(i):
            first, second = copies(s, i)
            first.wait()
            second.wait()
        _row_loop(tm, body)

    n_scr[step % 2] = _rms(x_ref[...], g_ref[...])
    start(step)

    @pl.when(step > 0)
    def _():
        wait(step - 1)

    @pl.when(step == pl.num_programs(0) - 1)
    def _():
        wait(step)


def _dispatch(x, g, pos1, pos2, fill, rows, tm, tile_rows):
    m, d = x.shape
    return pl.pallas_call(
        _dispatch_kernel,
        grid_spec=pltpu.PrefetchScalarGridSpec(
            num_scalar_prefetch=3,
            grid=(m // tm,),
            in_specs=[
                pl.BlockSpec((tm, d), lambda i, p1, p2, fl: (i, 0)),
                pl.BlockSpec((1, d), lambda i, p1, p2, fl: (0, 0)),
            ],
            out_specs=pl.BlockSpec(memory_space=pl.ANY),
            scratch_shapes=[pltpu.VMEM((2, tm, d), F32), pltpu.VMEM((tile_rows, d), F32),
                            pltpu.SemaphoreType.DMA((2,)), pltpu.SemaphoreType.DMA],
        ),
        out_shape=jax.ShapeDtypeStruct((rows, d), F32),
        compiler_params=_cparams(("arbitrary",)),
        name="dispatch",
    )(pos1, pos2, fill, x, g.reshape(1, d))


def _experts_kernel(te_ref, nv_ref, x_ref, wg_ref, wu_ref, wd_ref, y_ref):
    del te_ref
    valid = pl.program_id(0) < nv_ref[0]

    @pl.when(valid)
    def _():
        y_ref[...] = _swiglu(x_ref[...].astype(BF16), wg_ref, wu_ref, wd_ref)

    @pl.when(jnp.logical_not(valid))
    def _():
        y_ref[...] = jnp.zeros_like(y_ref)


def _experts(xs, tile_expert, n_valid, w_gu, w_down, tm):
    rows, d = xs.shape
    return pl.pallas_call(
        _experts_kernel,
        grid_spec=pltpu.PrefetchScalarGridSpec(
            num_scalar_prefetch=2,
            grid=(rows // tm,),
            in_specs=[
                pl.BlockSpec((tm, d), lambda t, te, nv: (t, 0)),
                pl.BlockSpec((None, d, D_FF), lambda t, te, nv: (te[t], 0, 0)),
                pl.BlockSpec((None, d, D_FF), lambda t, te, nv: (te[t], 0, 1)),
                pl.BlockSpec((None, D_FF, d), lambda t, te, nv: (te[t], 0, 0)),
            ],
            out_specs=pl.BlockSpec((tm, d), lambda t, te, nv: (t, 0)),
        ),
        out_shape=jax.ShapeDtypeStruct((rows, d), F32),
        compiler_params=_cparams(("arbitrary",)),
        name="experts",
    )(tile_expert, n_valid, xs, w_gu, w_gu, w_down)


def _combine_kernel(p1_ref, p2_ref, x_ref, route_ref, fg_ref, ys_ref, out_ref, buf, sem, *, final):
    tm = x_ref.shape[0]
    step = pl.program_id(0)

    def copies(s, i):
        slot = s % 2
        tok = s * tm + i
        return (_row_copy(ys_ref, p1_ref[tok], buf.at[slot, 0], i, sem.at[slot]),
                _row_copy(ys_ref, p2_ref[tok], buf.at[slot, 1], i, sem.at[slot]))

    def start(s):
        def body(i):
            first, second = copies(s, i)
            first.start()
            second.start(priority=1)
        _row_loop(tm, body)

    def wait(s):
        def body(i):
            first, second = copies(s, i)
            first.wait()
            second.wait()
        _row_loop(tm, body)

    @pl.when(step == 0)
    def _():
        start(step)

    @pl.when(step + 1 < pl.num_programs(0))
    def _():
        start(step + 1)

    wait(step)
    rows = buf[step % 2]
    route = route_ref[...]
    g1 = route[:, ROUTE_G1:ROUTE_G1 + 1]
    g2 = route[:, ROUTE_G2:ROUTE_G2 + 1]
    out = x_ref[...] + g1 * rows[0] + g2 * rows[1]
    out_ref[...] = _rms(out, fg_ref[...]) if final else out


def _combine(x, route, ys, pos1, pos2, final_g, tm):
    m, d = x.shape
    final = final_g is not None
    fg = final_g.reshape(1, d) if final else jnp.ones((1, d), F32)
    return pl.pallas_call(
        functools.partial(_combine_kernel, final=final),
        grid_spec=pltpu.PrefetchScalarGridSpec(
            num_scalar_prefetch=2,
            grid=(m // tm,),
            in_specs=[
                pl.BlockSpec((tm, d), lambda i, p1, p2: (i, 0)),
                pl.BlockSpec((tm, LANES), lambda i, p1, p2: (i, 0)),
                pl.BlockSpec((1, d), lambda i, p1, p2: (0, 0)),
                pl.BlockSpec(memory_space=pl.ANY),
            ],
            out_specs=pl.BlockSpec((tm, d), lambda i, p1, p2: (i, 0)),
            scratch_shapes=[pltpu.VMEM((2, 2, tm, d), F32), pltpu.SemaphoreType.DMA((2,))],
        ),
        out_shape=jax.ShapeDtypeStruct((m, d), F32),
        compiler_params=_cparams(("arbitrary",)),
        name="combine",
    )(pos1, pos2, x, route, fg, ys)


def _moe(x, g, w_router, w_gu, w_down, final_g, tm_route, tm_rows, tm_expert):
    m, d = x.shape
    route, route_t, counts = _route(x, g, w_router, tm_route)
    counts = counts[0, :N_EXPERTS].astype(jnp.int32)
    tiles = (counts + tm_expert - 1) // tm_expert
    experts = jnp.arange(N_EXPERTS, dtype=jnp.int32)
    tile_end = jnp.sum(jnp.where(experts[None, :] <= experts[:, None], tiles[None, :], 0), axis=1)
    row_start = (tile_end - tiles) * tm_expert

    def slot(e_row, r_row):
        e = route_t[e_row].astype(jnp.int32)
        start = jnp.sum(jnp.where(e[:, None] == experts[None, :], row_start[None, :], 0), axis=1)
        return start + route_t[r_row].astype(jnp.int32)

    pos1 = slot(ROUTE_E1, ROUTE_R1)
    pos2 = slot(ROUTE_E2, ROUTE_R2)
    n_tiles = 2 * m // tm_expert + N_EXPERTS
    tile_ids = jnp.arange(n_tiles, dtype=jnp.int32)
    tile_expert = jnp.minimum(
        jnp.sum((tile_ids[:, None] >= tile_end[None, :]).astype(jnp.int32), axis=1), N_EXPERTS - 1)
    n_valid = tile_end[-1:]
    tail = n_valid + experts
    fill = jnp.concatenate([jnp.where(tiles > 0, tile_end - 1, -1),
                            jnp.where(tail < n_tiles, tail, -1)])
    xs = _dispatch(x, g, pos1, pos2, fill, n_tiles * tm_expert, tm_rows, tm_expert)
    ys = _experts(xs, tile_expert, n_valid, w_gu, w_down, tm_expert)
    return _combine(x, route, ys, pos1, pos2, final_g, tm_rows)


def _final_norm_kernel(x_ref, g_ref, o_ref):
    o_ref[...] = _rms(x_ref[...], g_ref[...])


def _final_norm(x, g, tm):
    m, d = x.shape
    return pl.pallas_call(
        _final_norm_kernel,
        grid=(m // tm,),
        in_specs=[pl.BlockSpec((tm, d), lambda i: (i, 0)), pl.BlockSpec((1, d), lambda i: (0, 0))],
        out_specs=pl.BlockSpec((tm, d), lambda i: (i, 0)),
        out_shape=jax.ShapeDtypeStruct((m, d), F32),
        compiler_params=_cparams(("parallel",)),
        name="final_norm",
    )(x, g.reshape(1, d))


def _gelu(x):
    return 0.5 * x * (1.0 + lax.erf(x * (1.0 / math.sqrt(2.0))))


def _gmlp_kernel(u_ref, v_ref, lng_ref, lnb_ref, ws_ref, bias_ref, y_ref):
    tt = u_ref.shape[1]
    v = _gelu(v_ref[0].astype(F32))
    mu = jnp.mean(v, axis=-1, keepdims=True)
    d = v - mu
    var = jnp.mean(d * d, axis=-1, keepdims=True)
    vn = d * lax.rsqrt(var + LN_EPS) * lng_ref[...] + lnb_ref[...]
    row = lax.broadcasted_iota(jnp.int32, (GCHUNK, GCHUNK), 0)
    col = lax.broadcasted_iota(jnp.int32, (GCHUNK, GCHUNK), 1)
    causal = row >= col
    first_head = lax.broadcasted_iota(jnp.int32, (1, LANES), 1) < HEAD
    ws = [jnp.where(causal, ws_ref[gi], 0.0).astype(BF16) for gi in range(MIX_HEADS)]
    for c in range(tt // GCHUNK):
        rows = slice(c * GCHUNK, (c + 1) * GCHUNK)
        outs = []
        for j in range(PAIRS):
            vp = vn[rows, j * LANES:(j + 1) * LANES].astype(BF16)
            outs.append(jnp.where(first_head, _dot(ws[2 * j], vp), _dot(ws[2 * j + 1], vp)))
        mixed = jnp.concatenate(outs, axis=1) + bias_ref[...]
        u = _gelu(u_ref[0, rows, :].astype(F32))
        y_ref[0, rows, :] = (u * mixed).astype(y_ref.dtype)


def _gmlp(p3, ln_g, ln_b, w_s, bias, tt):
    b, t, _ = p3.shape
    return pl.pallas_call(
        _gmlp_kernel,
        grid=(b, t // tt),
        in_specs=[
            pl.BlockSpec((1, tt, MIX), lambda i, j: (i, j, 0)),
            pl.BlockSpec((1, tt, MIX), lambda i, j: (i, j, 1)),
            pl.BlockSpec((1, MIX), lambda i, j: (0, 0)),
            pl.BlockSpec((1, MIX), lambda i, j: (0, 0)),
            pl.BlockSpec((MIX_HEADS, GCHUNK, GCHUNK), lambda i, j: (0, 0, 0)),
            pl.BlockSpec((GCHUNK, MIX), lambda i, j: (0, 0)),
        ],
        out_specs=pl.BlockSpec((1, tt, MIX), lambda i, j: (i, j, 0)),
        out_shape=jax.ShapeDtypeStruct((b, t, MIX), BF16),
        compiler_params=_cparams(("parallel", "parallel")),
        name="gmlp",
    )(p3, p3, ln_g.reshape(1, MIX), ln_b.reshape(1, MIX), w_s, bias)


def _rwkv_kernel(r_ref, k_ref, v_ref, l_ref,
                 mur_ref, muk_ref, muv_ref, mul_ref, wcat_ref,
                 w0_ref, a0_ref, kk_ref, ka_ref, rk_ref, lnw_ref, lnb_ref,
                 tril_ref,
                 y_ref,
                 pr_scr, pk_scr, pv_scr, pl_scr, state_scr,
                 at_scr, bt_scr, kt_scr, rt_scr, bp_scr, kp_scr, v_scr, pe_scr,
                 rk_scr, g_scr, y_scr,
                 wq_scr, bk_scr, pet_scr, u0_scr, y0_scr):
    tt = r_ref.shape[1]
    L = SCAN_CHUNK

    @pl.when(pl.program_id(1) == 0)
    def _():
        pr_scr[...] = jnp.zeros_like(pr_scr)
        pk_scr[...] = jnp.zeros_like(pk_scr)
        pv_scr[...] = jnp.zeros_like(pv_scr)
        pl_scr[...] = jnp.zeros_like(pl_scr)
        state_scr[...] = jnp.zeros_like(state_scr)

    GL = PREPARE_CHUNKS * L
    pair_h0 = lax.broadcasted_iota(jnp.int32, (1, LANES), 1) < HEAD

    def head_sums(x):
        s0 = jnp.sum(jnp.where(pair_h0, x, 0.0), axis=-1, keepdims=True)
        s1 = jnp.sum(jnp.where(pair_h0, 0.0, x), axis=-1, keepdims=True)
        return jnp.where(pair_h0, s0, s1)

    def shifted(ref, prev_scr, mu_ref, g):
        cur = ref[0, g * GL:(g + 1) * GL, :].astype(F32)
        if g == 0:
            before = prev_scr[...]
        else:
            before = ref[0, g * GL - SUBLANES_BF16:g * GL, :].astype(F32)[SUBLANES_BF16 - 1:]
        row = lax.broadcasted_iota(jnp.int32, cur.shape, 0)
        prev = jnp.where(row == 0, before, pltpu.roll(cur, 1, axis=0))
        return cur + (prev - cur) * mu_ref[...]

    def prologue(g):
        rows = slice(g * GL, (g + 1) * GL)
        r = shifted(r_ref, pr_scr, mur_ref, g)
        k = shifted(k_ref, pk_scr, muk_ref, g)
        v = shifted(v_ref, pv_scr, muv_ref, g)
        lo = shifted(l_ref, pl_scr, mul_ref, g)
        llane = lax.broadcasted_iota(jnp.int32, lo.shape, 1)
        z = jnp.where(llane < 64, jnp.tanh(lo), jnp.where(llane < 128, lo, jax.nn.sigmoid(lo)))
        proj = _dot(z.astype(BF16), wcat_ref[...])
        yield
        ld = -math.exp(-0.5) * jax.nn.sigmoid(w0_ref[...] + proj[:, :MIX])
        a = jax.nn.sigmoid(a0_ref[...] + proj[:, MIX:2 * MIX])
        kk = k * kk_ref[...]
        kk_sq = kk * kk
        ss = jnp.concatenate(
            [head_sums(kk_sq[:, j * LANES:(j + 1) * LANES]) for j in range(PAIRS)], axis=1)
        chunk_cum = [_split_dot(tril_ref[...], ld[c * L:(c + 1) * L]) for c in range(PREPARE_CHUNKS)]
        yield
        kk = kk * lax.rsqrt(jnp.maximum(ss, 1e-24))
        k = k * (1.0 + (a - 1.0) * ka_ref[...])
        b = kk * a
        cum = jnp.concatenate(chunk_cum, axis=0)
        cum_end = jnp.concatenate(
            [jnp.broadcast_to(cc[L - 1:L], (L, MIX)) for cc in chunk_cum], axis=0)
        p_inv = jnp.exp(-cum)
        to_end = jnp.exp(cum_end - cum)
        at_scr[rows] = -(kk * jnp.exp(cum - ld))
        bt_scr[rows] = b * p_inv
        kt_scr[rows] = k * p_inv
        rt_scr[rows] = r * jnp.exp(cum)
        yield
        bp_scr[rows] = b * to_end
        kp_scr[rows] = k * to_end
        v_scr[rows] = v
        pe_scr[rows] = jnp.exp(cum_end)
        rk_scr[rows] = r * k * rk_ref[...]
        g_scr[rows] = proj[:, 2 * MIX:]
        yield

    lane = lax.broadcasted_iota(jnp.int32, (1, LANES), 1)
    h0 = lane < HEAD
    row = lax.broadcasted_iota(jnp.int32, (L, LANES), 0)
    col = lax.broadcasted_iota(jnp.int32, (L, LANES), 1) % HEAD
    strict = row > col
    incl = row >= col
    srow = lax.broadcasted_iota(jnp.int32, (LANES, LANES), 0) < HEAD
    scol = lax.broadcasted_iota(jnp.int32, (LANES, LANES), 1) < HEAD
    same_head = srow == scol
    eye = jnp.where(lax.broadcasted_iota(jnp.int32, (LANES, LANES), 0)
                    == lax.broadcasted_iota(jnp.int32, (LANES, LANES), 1), 1.0, 0.0)
    pairs = range(PAIRS)

    def cat(x, y):
        return jnp.concatenate([x, y], axis=0)

    def first(x):
        return jnp.where(h0, x, 0.0)

    def second(x):
        return jnp.where(h0, 0.0, x)

    def lanes(x, y):
        return jnp.concatenate([x, y], axis=1)

    def tile(ref, c, j):
        return ref[c * L:(c + 1) * L, j * LANES:(j + 1) * LANES]


    def prepare(chunks):
        items = [(c, j) for c in chunks for j in pairs]
        idx = range(len(items))
        at = [tile(at_scr, c, j) for c, j in items]
        bt = [tile(bt_scr, c, j) for c, j in items]
        kt = [tile(kt_scr, c, j) for c, j in items]
        rt = [tile(rt_scr, c, j) for c, j in items]
        vc = [tile(v_scr, c, j) for c, j in items]
        g_0 = [_dot_nt(cat(first(at[i]), first(rt[i])).astype(BF16), cat(bt[i], kt[i]).astype(BF16))
               for i in idx]
        g_1 = [_dot_nt(cat(second(at[i]), second(rt[i])).astype(BF16), cat(kt[i], bt[i]).astype(BF16))
               for i in idx]
        yield
        top0 = [jnp.where(strict, g_0[i][:L], 0.0) for i in idx]
        top1 = [jnp.where(strict, g_1[i][:L], 0.0) for i in idx]
        vv = [cat(vc[i], vc[i]).astype(BF16) for i in idx]
        akv0 = [_dot(second(top0[i]).astype(BF16), vv[i]) for i in idx]
        akv1 = [_dot(first(top1[i]).astype(BF16), vv[i]) for i in idx]
        yield
        akv = [jnp.where(h0, akv0[i], akv1[i]) for i in idx]
        z = [lanes(cat(first(at[i]), second(at[i])), cat(first(akv[i]), second(akv[i])))
             .astype(BF16) for i in idx]
        apow = [cat(first(top0[i]), second(top1[i])) for i in idx]
        tinv = [eye + apow[i] for i in idx]
        apow = [_dot(apow[i].astype(BF16), apow[i].astype(BF16)) for i in idx]
        yield
        for step in range(5):
            ab = [apow[i].astype(BF16) for i in idx]
            if step < 4:
                both = [_dot(cat(apow[i], tinv[i]).astype(BF16), ab[i]) for i in idx]
                apow = [both[i][:LANES] for i in idx]
                tinv = [tinv[i] + both[i][LANES:] for i in idx]
            else:
                tinv = [tinv[i] + _dot(tinv[i].astype(BF16), ab[i]) for i in idx]
            yield
        z = [_dot(tinv[i].astype(BF16), z[i]) for i in idx]
        yield
        w = [z[i][:L, :LANES] + z[i][L:, :LANES] for i in idx]
        u0 = [z[i][:L, LANES:] + z[i][L:, LANES:] for i in idx]
        zero = jnp.zeros((L, LANES), F32)
        wu = [lanes(w[i], u0[i]) for i in idx]
        zv = [lanes(zero, vc[i]) for i in idx]
        r_0 = [_dot(jnp.where(incl, g_0[i][L:], 0.0).astype(BF16), cat(wu[i], zv[i]).astype(BF16))
               for i in idx]
        r_1 = [_dot(jnp.where(incl, g_1[i][L:], 0.0).astype(BF16), cat(zv[i], wu[i]).astype(BF16))
               for i in idx]
        yield
        for i, (c, j) in enumerate(items):
            q = rt[i] + jnp.where(h0, r_0[i][:, :LANES], r_1[i][:, :LANES])
            wq_scr[c, j] = cat(w[i], q).astype(BF16)
            u0_scr[c, j] = u0[i]
            y0_scr[c, j] = jnp.where(h0, r_0[i][:, LANES:], r_1[i][:, LANES:])
            bk_scr[c, j] = jnp.transpose(cat(tile(bp_scr, c, j), tile(kp_scr, c, j))).astype(BF16)
            pe_row = pe_scr[c * L:c * L + 1, j * LANES:(j + 1) * LANES]
            pet_scr[c, j] = jnp.transpose(jnp.broadcast_to(pe_row, (LANES, LANES)))
        yield

    def advance(c):
        h = [state_scr[j] for j in pairs]
        wqh = [_dot(wq_scr[c, j], h[j].astype(BF16)) for j in pairs]
        yield
        uv = [cat(wqh[j][:L] + u0_scr[c, j], tile(v_scr, c, j)).astype(BF16) for j in pairs]
        upd = [_dot(bk_scr[c, j], uv[j]) for j in pairs]
        for j in pairs:
            y_scr[c * L:(c + 1) * L, j * LANES:(j + 1) * LANES] = wqh[j][L:] + y0_scr[c, j]
        yield
        for j in pairs:
            state_scr[j] = h[j] * pet_scr[c, j] + jnp.where(same_head, upd[j], 0.0)
        yield

    def chain(gens):
        for gen in gens:
            yield from gen

    def epilogue(g):
        rows = slice(g * GL, (g + 1) * GL)
        tiles = [(rows, slice(j * LANES, (j + 1) * LANES)) for j in pairs]
        y = [y_scr[t] for t in tiles]
        mean = [head_sums(y[j]) * (1.0 / HEAD) for j in pairs]
        rk = [head_sums(rk_scr[tiles[j]]) for j in pairs]
        yield
        d = [y[j] - mean[j] for j in pairs]
        var = [head_sums(d[j] * d[j]) * (1.0 / HEAD) for j in pairs]
        yield
        outs = []
        for j in pairs:
            cols = tiles[j][1]
            yn = d[j] * lax.rsqrt(var[j] + GN_EPS) * lnw_ref[:, cols] + lnb_ref[:, cols]
            outs.append((yn + rk[j] * v_scr[tiles[j]]) * g_scr[tiles[j]])
        y_ref[0, rows, :] = jnp.concatenate(outs, axis=1).astype(y_ref.dtype)
        yield

    n_groups = tt // GL
    for _ in prologue(0):
        pass
    behind = iter(())
    for g in range(n_groups):
        ahead = prologue(g + 1) if g + 1 < n_groups else iter(())
        for _ in prepare(range(g * PREPARE_CHUNKS, (g + 1) * PREPARE_CHUNKS)):
            next(behind, None)
            next(ahead, None)
        for _ in ahead:
            pass
        for _ in behind:
            pass
        behind = chain([advance(c) for c in range(g * PREPARE_CHUNKS, (g + 1) * PREPARE_CHUNKS)]
                       + [epilogue(g)])
    for _ in behind:
        pass

    for ref, scr in ((r_ref, pr_scr), (k_ref, pk_scr), (v_ref, pv_scr), (l_ref, pl_scr)):
        scr[...] = ref[0, tt - SUBLANES_BF16:tt, :].astype(F32)[SUBLANES_BF16 - 1:]


def _rwkv(p3, mu, wcat, w0, a0, k_k, k_a, r_k, lnx_w, lnx_b, tt):
    b, t, _ = p3.shape
    row = lambda x: x.reshape(1, -1)
    idx = jnp.arange(SCAN_CHUNK)
    tril = (idx[:, None] >= idx[None, :]).astype(BF16)
    tril = jnp.concatenate([tril, tril], axis=1)
    const = lambda shape: pl.BlockSpec(shape, lambda i, j: (0,) * len(shape))
    tile = pltpu.VMEM((tt, MIX), F32)
    return pl.pallas_call(
        _rwkv_kernel,
        grid=(b, t // tt),
        in_specs=[
            pl.BlockSpec((1, tt, MIX), lambda i, j: (i, j, 0)),
            pl.BlockSpec((1, tt, MIX), lambda i, j: (i, j, 1)),
            pl.BlockSpec((1, tt, MIX), lambda i, j: (i, j, 2)),
            pl.BlockSpec((1, tt, LORA_W), lambda i, j: (i, j, 3 * MIX // LORA_W)),
            const((1, MIX)), const((1, MIX)), const((1, MIX)), const((1, LORA_W)),
            const((LORA_W, 3 * MIX)),
            const((1, MIX)), const((1, MIX)), const((1, MIX)), const((1, MIX)),
            const((1, MIX)), const((1, MIX)), const((1, MIX)),
            const((SCAN_CHUNK, 2 * SCAN_CHUNK)),
        ],
        out_specs=pl.BlockSpec((1, tt, MIX), lambda i, j: (i, j, 0)),
        out_shape=jax.ShapeDtypeStruct((b, t, MIX), BF16),
        scratch_shapes=[
            pltpu.VMEM((1, MIX), F32), pltpu.VMEM((1, MIX), F32), pltpu.VMEM((1, MIX), F32),
            pltpu.VMEM((1, LORA_W), F32),
            pltpu.VMEM((PAIRS, LANES, LANES), F32),
        ] + [tile] * 11 + [
            pltpu.VMEM((tt // SCAN_CHUNK, PAIRS, LANES, LANES), BF16),
            pltpu.VMEM((tt // SCAN_CHUNK, PAIRS, LANES, LANES), BF16),
            pltpu.VMEM((tt // SCAN_CHUNK, PAIRS, LANES, LANES), F32),
            pltpu.VMEM((tt // SCAN_CHUNK, PAIRS, SCAN_CHUNK, LANES), F32),
            pltpu.VMEM((tt // SCAN_CHUNK, PAIRS, SCAN_CHUNK, LANES), F32),
        ],
        compiler_params=_cparams(("parallel", "arbitrary")),
        name="rwkv7",
    )(p3, p3, p3, p3,
      row(mu[:MIX]), row(mu[MIX:2 * MIX]), row(mu[2 * MIX:3 * MIX]), row(mu[3 * MIX:]),
      wcat, row(w0), row(a0), row(k_k), row(k_a), row(r_k), row(lnx_w), row(lnx_b),
      tril)


def kernel(x, mem, mem_norm_g, norm1_g, w_kv_mem, w_out, norm2_g, rwkv_w_in, rwkv_mu, rwkv_w0, rwkv_w2, rwkv_a0, rwkv_a2, rwkv_g2, rwkv_k_k, rwkv_k_a, rwkv_r_k, rwkv_lnx_w, rwkv_lnx_b, ffn_w_gu, ffn_w_down, gmlp_w_in, gmlp_v_ln_g, gmlp_v_ln_b, gmlp_w_s, gmlp_b_s, moe_router, moe_w_gu, moe_w_down, final_norm_g):
    b, t, d = x.shape
    n_tok = b * t
    depth = norm1_g.shape[0]
    xs = x.reshape(n_tok, d)
    mem2 = mem.reshape(b * N_MEM, d)
    for i in range(depth):
        j = i // 2
        kv = _norm_matmul(mem2, mem_norm_g, w_kv_mem[i].astype(BF16), PROJ_TILE).reshape(b, N_MEM, 2 * MEMW)
        if i % 2 == 0:
            p = _norm_matmul(xs, norm1_g[i], rwkv_w_in[j].astype(BF16), PROJ_TILE)
            p3 = p.reshape(b, t, RWKV_COLS + MEMW)
            wcat = jnp.zeros((LORA_W, 3 * MIX), F32)
            wcat = wcat.at[0:64, 0:MIX].set(rwkv_w2[j])
            wcat = wcat.at[64:128, MIX:2 * MIX].set(rwkv_a2[j])
            wcat = wcat.at[128:256, 2 * MIX:].set(rwkv_g2[j])
            y = _rwkv(p3, rwkv_mu[j], wcat.astype(BF16), rwkv_w0[j], rwkv_a0[j], rwkv_k_k[j],
                      rwkv_k_a[j], rwkv_r_k[j].reshape(MIX), rwkv_lnx_w[j], rwkv_lnx_b[j], RWKV_TILE)
            q_block = RWKV_COLS // MEMW
        else:
            p = _norm_matmul(xs, norm1_g[i], gmlp_w_in[j].astype(BF16), PROJ_TILE)
            p3 = p.reshape(b, t, 2 * MIX + MEMW)
            bias = jnp.repeat(jnp.transpose(gmlp_b_s[j]), HEAD, axis=1)
            y = _gmlp(p3, gmlp_v_ln_g[j], gmlp_v_ln_b[j], gmlp_w_s[j], bias, GMLP_TILE)
            q_block = 2 * MIX // MEMW
        wo = w_out[i].astype(BF16)
        xs = _mixer_out(xs.reshape(b, t, d), y, p3, q_block, kv, wo[:MIX], wo[MIX:],
                        MIXER_TILE).reshape(n_tok, d)
        if i % 2 == 0:
            pending = ()
            if j < moe_w_gu.shape[0]:
                pending = (moe_w_gu[j].reshape(-1, 2 * D_FF), moe_w_down[j].reshape(-1, d))
            xs, narrowed = _ffn(xs, norm2_g[i], ffn_w_gu[j].astype(BF16), ffn_w_down[j].astype(BF16),
                                FFN_TILE, pending)
        else:
            last = i == depth - 1
            w_gu, w_down = (narrowed[0].reshape(moe_w_gu.shape[1:]),
                            narrowed[1].reshape(moe_w_down.shape[1:]))
            xs = _moe(xs, norm2_g[i], moe_router[j], w_gu, w_down, final_norm_g if last else None,
                      TOKEN_TILE, ROW_COPY_TILE, EXPERT_TILE)
    if depth % 2 == 1:
        xs = _final_norm(xs, final_norm_g, TOKEN_TILE)
    return xs.reshape(b, t, d)
```

```python
import functools
import math

import jax
import jax.numpy as jnp
from jax import lax
from jax.experimental import pallas as pl
from jax.experimental.pallas import tpu as pltpu

F32 = jnp.float32
BF16 = jnp.bfloat16

D_MODEL = 1024
HEAD = 64
MIX = 768
MIX_HEADS = MIX // HEAD
MEMW = 256
MEM_HEADS = 4
N_MEM = 256
LORA_W = 256
RWKV_COLS = 3 * MIX + LORA_W
GCHUNK = 128
D_FF = 2816
N_EXPERTS = 8
RMS_EPS = 1e-6
GN_EPS = 64e-5
LN_EPS = 1e-5

LANES = 128
SUBLANES_BF16 = 16
PAIRS = MIX // LANES
SCAN_CHUNK = 64
PREPARE_CHUNKS = 2
VMEM_LIMIT = 56 * 1024 * 1024

TOKEN_TILE = 512
PROJ_TILE = 1024
MIXER_TILE = 1024
RWKV_TILE = 512
GMLP_TILE = 512
FFN_TILE = 256
ROW_COPY_TILE = 512
EXPERT_TILE = 512


def _cparams(sem):
    return pltpu.CompilerParams(dimension_semantics=sem, vmem_limit_bytes=VMEM_LIMIT)


def _rms(x, g):
    ms = jnp.mean(x * x, axis=-1, keepdims=True)
    return x * lax.rsqrt(ms + RMS_EPS) * g


def _dot(a, b):
    return jnp.dot(a, b, preferred_element_type=F32)


def _dot_nt(a, b):
    return lax.dot_general(a, b, (((1,), (1,)), ((), ())), preferred_element_type=F32)


def _dot_tn(a, b):
    return lax.dot_general(a, b, (((0,), (0,)), ((), ())), preferred_element_type=F32)


def _split_dot(m2_bf16, x):
    hi = x.astype(BF16)
    lo = (x - hi.astype(F32)).astype(BF16)
    return _dot(m2_bf16, jnp.concatenate([hi, lo], axis=0))


def _norm_matmul_kernel(x_ref, g_ref, w_ref, *refs):
    n_narrow = len(refs) // 2
    n = _rms(x_ref[...], g_ref[...]).astype(BF16)
    refs[n_narrow][...] = _dot(n, w_ref[...]).astype(refs[n_narrow].dtype)
    for src, dst in zip(refs[:n_narrow], refs[n_narrow + 1:]):
        dst[...] = src[...].astype(BF16)


def _norm_matmul(x, g, w, tm, narrow=()):
    m, k = x.shape
    n = w.shape[1]
    steps = m // tm
    row_blocks = [pl.BlockSpec((a.shape[0] // steps, a.shape[1]), lambda i: (i, 0)) for a in narrow]
    out = pl.pallas_call(
        _norm_matmul_kernel,
        grid=(steps,),
        in_specs=[
            pl.BlockSpec((tm, k), lambda i: (i, 0)),
            pl.BlockSpec((1, k), lambda i: (0, 0)),
            pl.BlockSpec((k, n), lambda i: (0, 0)),
        ] + row_blocks,
        out_specs=[pl.BlockSpec((tm, n), lambda i: (i, 0))] + row_blocks,
        out_shape=[jax.ShapeDtypeStruct((m, n), BF16)]
        + [jax.ShapeDtypeStruct(a.shape, BF16) for a in narrow],
        compiler_params=_cparams(("parallel",)),
        name="norm_matmul",
    )(x, g.reshape(1, k), w, *narrow)
    return out[0], tuple(out[1:])


def _mem_attn(q, kv):
    k = kv[:, :MEMW].astype(BF16)
    v = kv[:, MEMW:].astype(BF16)
    lane = lax.broadcasted_iota(jnp.int32, (1, MEMW), 1)
    acc = jnp.zeros(q.shape, F32)
    for h in range(MEM_HEADS):
        m = (lane >= h * HEAD) & (lane < (h + 1) * HEAD)
        qh = jnp.where(m, q, 0.0).astype(BF16)
        s = _dot_nt(qh, k) * (1.0 / math.sqrt(HEAD))
        s = s - jnp.max(s, axis=-1, keepdims=True)
        p = jnp.exp(s)
        l = jnp.sum(p, axis=-1, keepdims=True)
        pv = _dot(p.astype(BF16), v)
        acc = acc + jnp.where(m, pv / l, 0.0)
    return acc


def _mixer_out_kernel(x_ref, y_ref, q_ref, kv_ref, wy_ref, wo_ref, out_ref):
    o = _mem_attn(q_ref[0], kv_ref[0])
    out_ref[0] = (x_ref[0]
                  + _dot(y_ref[0], wy_ref[...])
                  + _dot(o.astype(BF16), wo_ref[...]))


def _mixer_out(x3, y3, p3, q_block, kv3, wy, wo, tt):
    b, t, d = x3.shape
    return pl.pallas_call(
        _mixer_out_kernel,
        grid=(b, t // tt),
        in_specs=[
            pl.BlockSpec((1, tt, d), lambda i, j: (i, j, 0)),
            pl.BlockSpec((1, tt, MIX), lambda i, j: (i, j, 0)),
            pl.BlockSpec((1, tt, MEMW), lambda i, j: (i, j, q_block)),
            pl.BlockSpec((1, N_MEM, 2 * MEMW), lambda i, j: (i, 0, 0)),
            pl.BlockSpec((MIX, d), lambda i, j: (0, 0)),
            pl.BlockSpec((MEMW, d), lambda i, j: (0, 0)),
        ],
        out_specs=pl.BlockSpec((1, tt, d), lambda i, j: (i, j, 0)),
        out_shape=jax.ShapeDtypeStruct((b, t, d), F32),
        compiler_params=_cparams(("parallel", "parallel")),
        name="mixer_out",
    )(x3, y3, p3, kv3, wy, wo)


def _swiglu(n, wg_ref, wu_ref, wd_ref):
    gate = _dot(n, wg_ref[...])
    up = _dot(n, wu_ref[...])
    h = gate * jax.nn.sigmoid(gate) * up
    return _dot(h.astype(BF16), wd_ref[...])


def _ffn_kernel(x_ref, g_ref, wg_ref, wu_ref, wd_ref, *refs):
    narrow = len(refs) > 1
    out_ref = refs[2] if narrow else refs[0]
    if narrow:
        refs[3][...] = refs[0][...].astype(BF16)
    x = x_ref[...]
    out_ref[...] = x + _swiglu(_rms(x, g_ref[...]).astype(BF16), wg_ref, wu_ref, wd_ref)
    if narrow:
        refs[4][...] = refs[1][...].astype(BF16)


def _ffn(x, g, w_gu, w_down, tm, narrow=()):
    m, d = x.shape
    steps = m // tm
    resident = pl.Buffered(1)
    row_blocks = [pl.BlockSpec((a.shape[0] // steps, a.shape[1]), lambda i: (i, 0)) for a in narrow]
    out = pl.pallas_call(
        _ffn_kernel,
        grid=(steps,),
        in_specs=[
            pl.BlockSpec((tm, d), lambda i: (i, 0)),
            pl.BlockSpec((1, d), lambda i: (0, 0)),
            pl.BlockSpec((d, D_FF), lambda i: (0, 0), pipeline_mode=resident),
            pl.BlockSpec((d, D_FF), lambda i: (0, 1), pipeline_mode=resident),
            pl.BlockSpec((D_FF, d), lambda i: (0, 0), pipeline_mode=resident),
        ] + row_blocks,
        out_specs=[pl.BlockSpec((tm, d), lambda i: (i, 0))] + row_blocks,
        out_shape=[jax.ShapeDtypeStruct((m, d), F32)]
        + [jax.ShapeDtypeStruct(a.shape, BF16) for a in narrow],
        compiler_params=_cparams(("parallel",)),
        name="ffn",
    )(x, g.reshape(1, d), w_gu, w_gu, w_down, *narrow)
    return out[0], tuple(out[1:])


ROUTE_E1, ROUTE_E2, ROUTE_R1, ROUTE_R2, ROUTE_G1, ROUTE_G2 = range(6)
ROUTE_FIELDS = 8


def _route_tile(x, first, g_ref, wr_ref, tril_ref, route_ref, routet_ref, cnt_ref, carry_scr):
    @pl.when(first)
    def _():
        carry_scr[...] = jnp.zeros_like(carry_scr)

    n = _rms(x, g_ref[...])
    n_hi = n.astype(BF16)
    n_lo = (n - n_hi.astype(F32)).astype(BF16)
    logits = _dot(jnp.concatenate([n_hi, n_hi, n_lo], axis=1), wr_ref[...])
    lane = lax.broadcasted_iota(jnp.int32, logits.shape, 1)
    neg = jnp.float32(-jnp.inf)
    lg = jnp.where(lane < N_EXPERTS, logits, neg)
    m1 = jnp.max(lg, axis=-1, keepdims=True)
    i1 = jnp.min(jnp.where(lg == m1, lane, LANES), axis=-1, keepdims=True)
    lg2 = jnp.where(lane == i1, neg, lg)
    m2 = jnp.max(lg2, axis=-1, keepdims=True)
    i2 = jnp.min(jnp.where(lg2 == m2, lane, LANES), axis=-1, keepdims=True)
    e2 = jnp.exp(m2 - m1)
    g1 = 1.0 / (1.0 + e2)
    g2 = e2 / (1.0 + e2)
    chosen = jnp.where(lane == i1, 1.0, jnp.where(lane == i2, 1.0, 0.0))
    before = _dot(tril_ref[...], chosen.astype(BF16)) + carry_scr[...]
    r1 = jnp.sum(jnp.where(lane == i1, before, 0.0), axis=-1, keepdims=True)
    r2 = jnp.sum(jnp.where(lane == i2, before, 0.0), axis=-1, keepdims=True)
    carry_scr[...] += jnp.sum(chosen, axis=0, keepdims=True)
    cnt_ref[...] = carry_scr[...]
    fields = (i1.astype(F32), i2.astype(F32), r1, r2, g1, g2)
    route = jnp.zeros(lane.shape, F32)
    for idx, val in enumerate(fields):
        route = jnp.where(lane == idx, val, route)
    route_ref[...] = route
    routet_ref[...] = jnp.transpose(route)[:ROUTE_FIELDS]


def _route_kernel(x_ref, g_ref, wr_ref, tril_ref, route_ref, routet_ref, cnt_ref, carry_scr):
    _route_tile(x_ref[...], pl.program_id(0) == 0, g_ref, wr_ref, tril_ref,
                route_ref, routet_ref, cnt_ref, carry_scr)


def _route(x, g, w_router, tm):
    m, d = x.shape
    wr_pad = jnp.zeros((d, LANES), F32).at[:, :N_EXPERTS].set(w_router)
    wr_hi = wr_pad.astype(BF16)
    wr_lo = (wr_pad - wr_hi.astype(F32)).astype(BF16)
    wr_split = jnp.concatenate([wr_hi, wr_lo, wr_hi], axis=0)
    idx = jnp.arange(tm)
    tril = (idx[:, None] > idx[None, :]).astype(BF16)
    return pl.pallas_call(
        _route_kernel,
        grid=(m // tm,),
        in_specs=[
            pl.BlockSpec((tm, d), lambda i: (i, 0)),
            pl.BlockSpec((1, d), lambda i: (0, 0)),
            pl.BlockSpec((3 * d, LANES), lambda i: (0, 0)),
            pl.BlockSpec((tm, tm), lambda i: (0, 0)),
        ],
        out_specs=[pl.BlockSpec((tm, LANES), lambda i: (i, 0)),
                   pl.BlockSpec((ROUTE_FIELDS, tm), lambda i: (0, i)),
                   pl.BlockSpec((1, LANES), lambda i: (0, 0))],
        out_shape=[jax.ShapeDtypeStruct((m, LANES), F32),
                   jax.ShapeDtypeStruct((ROUTE_FIELDS, m), F32),
                   jax.ShapeDtypeStruct((1, LANES), F32)],
        scratch_shapes=[pltpu.VMEM((1, LANES), F32)],
        compiler_params=_cparams(("arbitrary",)),
        name="route",
    )(x, g.reshape(1, d), wr_split, tril)


def _row_copy(src, src_row, dst, dst_row, sem):
    return pltpu.make_async_copy(src.at[pl.ds(src_row, 1)], dst.at[pl.ds(dst_row, 1)], sem)


ROW_LOOP_UNROLL = 8


def _row_loop(tm, body):
    def step(i, carry):
        body(i)
        return carry
    lax.fori_loop(0, tm, step, 0, unroll=ROW_LOOP_UNROLL)


def _dispatch_kernel(p1_ref, p2_ref, fill_ref, x_ref, g_ref, xs_ref, n_scr, zero_scr, sem, fill_sem):
    tm = x_ref.shape[0]
    tile_rows = zero_scr.shape[0]
    step = pl.program_id(0)

    def fill_copy(f):
        return pltpu.make_async_copy(
            zero_scr, xs_ref.at[pl.ds(pl.multiple_of(fill_ref[f] * tile_rows, tile_rows), tile_rows)],
            fill_sem)

    @pl.when(step == 0)
    def _():
        zero_scr[...] = jnp.zeros_like(zero_scr)
        for f in range(fill_ref.shape[0]):
            @pl.when(fill_ref[f] >= 0)
            def _():
                fill_copy(f).start()
        for f in range(fill_ref.shape[0]):
            @pl.when(fill_ref[f] >= 0)
            def _():
                fill_copy(f).wait()

    def copies(s, i):
        slot = s % 2
        tok = s * tm + i
        return (_row_copy(n_scr.at[slot], i, xs_ref, p1_ref[tok], sem.at[slot]),
                _row_copy(n_scr.at[slot], i, xs_ref, p2_ref[tok], sem.at[slot]))

    def start(s):
        def body(i):
            first, second = copies(s, i)
            first.start()
            second.start(priority=1)
        _row_loop(tm, body)

    def wait(s):
        def body(i):
            first, second = copies(s, i)
            first.wait()
            second.wait()
        _row_loop(tm, body)

    n_scr[step % 2] = _rms(x_ref[...], g_ref[...])
    start(step)

    @pl.when(step > 0)
    def _():
        wait(step - 1)

    @pl.when(step == pl.num_programs(0) - 1)
    def _():
        wait(step)


def _dispatch(x, g, pos1, pos2, fill, rows, tm, tile_rows):
    m, d = x.shape
    return pl.pallas_call(
        _dispatch_kernel,
        grid_spec=pltpu.PrefetchScalarGridSpec(
            num_scalar_prefetch=3,
            grid=(m // tm,),
            in_specs=[
                pl.BlockSpec((tm, d), lambda i, p1, p2, fl: (i, 0)),
                pl.BlockSpec((1, d), lambda i, p1, p2, fl: (0, 0)),
            ],
            out_specs=pl.BlockSpec(memory_space=pl.ANY),
            scratch_shapes=[pltpu.VMEM((2, tm, d), F32), pltpu.VMEM((tile_rows, d), F32),
                            pltpu.SemaphoreType.DMA((2,)), pltpu.SemaphoreType.DMA],
        ),
        out_shape=jax.ShapeDtypeStruct((rows, d), F32),
        compiler_params=_cparams(("arbitrary",)),
        name="dispatch",
    )(pos1, pos2, fill, x, g.reshape(1, d))


def _experts_kernel(te_ref, nv_ref, x_ref, wg_ref, wu_ref, wd_ref, y_ref):
    del te_ref
    valid = pl.program_id(0) < nv_ref[0]

    @pl.when(valid)
    def _():
        y_ref[...] = _swiglu(x_ref[...].astype(BF16), wg_ref, wu_ref, wd_ref)

    @pl.when(jnp.logical_not(valid))
    def _():
        y_ref[...] = jnp.zeros_like(y_ref)


def _experts(xs, tile_expert, n_valid, w_gu, w_down, tm):
    rows, d = xs.shape
    return pl.pallas_call(
        _experts_kernel,
        grid_spec=pltpu.PrefetchScalarGridSpec(
            num_scalar_prefetch=2,
            grid=(rows // tm,),
            in_specs=[
                pl.BlockSpec((tm, d), lambda t, te, nv: (t, 0)),
                pl.BlockSpec((None, d, D_FF), lambda t, te, nv: (te[t], 0, 0)),
                pl.BlockSpec((None, d, D_FF), lambda t, te, nv: (te[t], 0, 1)),
                pl.BlockSpec((None, D_FF, d), lambda t, te, nv: (te[t], 0, 0)),
            ],
            out_specs=pl.BlockSpec((tm, d), lambda t, te, nv: (t, 0)),
        ),
        out_shape=jax.ShapeDtypeStruct((rows, d), F32),
        compiler_params=_cparams(("arbitrary",)),
        name="experts",
    )(tile_expert, n_valid, xs, w_gu, w_gu, w_down)


def _combine_kernel(p1_ref, p2_ref, x_ref, route_ref, fg_ref, ys_ref, out_ref, buf, sem, *, final):
    tm = x_ref.shape[0]
    step = pl.program_id(0)

    def copies(s, i):
        slot = s % 2
        tok = s * tm + i
        return (_row_copy(ys_ref, p1_ref[tok], buf.at[slot, 0], i, sem.at[slot]),
                _row_copy(ys_ref, p2_ref[tok], buf.at[slot, 1], i, sem.at[slot]))

    def start(s):
        def body(i):
            first, second = copies(s, i)
            first.start()
            second.start(priority=1)
        _row_loop(tm, body)

    def wait(s):
        def body(i):
            first, second = copies(s, i)
            first.wait()
            second.wait()
        _row_loop(tm, body)

    @pl.when(step == 0)
    def _():
        start(step)

    @pl.when(step + 1 < pl.num_programs(0))
    def _():
        start(step + 1)

    wait(step)
    rows = buf[step % 2]
    route = route_ref[...]
    g1 = route[:, ROUTE_G1:ROUTE_G1 + 1]
    g2 = route[:, ROUTE_G2:ROUTE_G2 + 1]
    out = x_ref[...] + g1 * rows[0] + g2 * rows[1]
    out_ref[...] = _rms(out, fg_ref[...]) if final else out


def _combine(x, route, ys, pos1, pos2, final_g, tm):
    m, d = x.shape
    final = final_g is not None
    fg = final_g.reshape(1, d) if final else jnp.ones((1, d), F32)
    return pl.pallas_call(
        functools.partial(_combine_kernel, final=final),
        grid_spec=pltpu.PrefetchScalarGridSpec(
            num_scalar_prefetch=2,
            grid=(m // tm,),
            in_specs=[
                pl.BlockSpec((tm, d), lambda i, p1, p2: (i, 0)),
                pl.BlockSpec((tm, LANES), lambda i, p1, p2: (i, 0)),
                pl.BlockSpec((1, d), lambda i, p1, p2: (0, 0)),
                pl.BlockSpec(memory_space=pl.ANY),
            ],
            out_specs=pl.BlockSpec((tm, d), lambda i, p1, p2: (i, 0)),
            scratch_shapes=[pltpu.VMEM((2, 2, tm, d), F32), pltpu.SemaphoreType.DMA((2,))],
        ),
        out_shape=jax.ShapeDtypeStruct((m, d), F32),
        compiler_params=_cparams(("arbitrary",)),
        name="combine",
    )(pos1, pos2, x, route, fg, ys)


def _moe(x, g, w_router, w_gu, w_down, final_g, tm_route, tm_rows, tm_expert):
    m, d = x.shape
    route, route_t, counts = _route(x, g, w_router, tm_route)
    counts = counts[0, :N_EXPERTS].astype(jnp.int32)
    tiles = (counts + tm_expert - 1) // tm_expert
    experts = jnp.arange(N_EXPERTS, dtype=jnp.int32)
    tile_end = jnp.sum(jnp.where(experts[None, :] <= experts[:, None], tiles[None, :], 0), axis=1)
    row_start = (tile_end - tiles) * tm_expert

    def slot(e_row, r_row):
        e = route_t[e_row].astype(jnp.int32)
        start = jnp.sum(jnp.where(e[:, None] == experts[None, :], row_start[None, :], 0), axis=1)
        return start + route_t[r_row].astype(jnp.int32)

    pos1 = slot(ROUTE_E1, ROUTE_R1)
    pos2 = slot(ROUTE_E2, ROUTE_R2)
    n_tiles = 2 * m // tm_expert + N_EXPERTS
    tile_ids = jnp.arange(n_tiles, dtype=jnp.int32)
    tile_expert = jnp.minimum(
        jnp.sum((tile_ids[:, None] >= tile_end[None, :]).astype(jnp.int32), axis=1), N_EXPERTS - 1)
    n_valid = tile_end[-1:]
    tail = n_valid + experts
    fill = jnp.concatenate([jnp.where(tiles > 0, tile_end - 1, -1),
                            jnp.where(tail < n_tiles, tail, -1)])
    xs = _dispatch(x, g, pos1, pos2, fill, n_tiles * tm_expert, tm_rows, tm_expert)
    ys = _experts(xs, tile_expert, n_valid, w_gu, w_down, tm_expert)
    return _combine(x, route, ys, pos1, pos2, final_g, tm_rows)


def _final_norm_kernel(x_ref, g_ref, o_ref):
    o_ref[...] = _rms(x_ref[...], g_ref[...])


def _final_norm(x, g, tm):
    m, d = x.shape
    return pl.pallas_call(
        _final_norm_kernel,
        grid=(m // tm,),
        in_specs=[pl.BlockSpec((tm, d), lambda i: (i, 0)), pl.BlockSpec((1, d), lambda i: (0, 0))],
        out_specs=pl.BlockSpec((tm, d), lambda i: (i, 0)),
        out_shape=jax.ShapeDtypeStruct((m, d), F32),
        compiler_params=_cparams(("parallel",)),
        name="final_norm",
    )(x, g.reshape(1, d))


def _gelu(x):
    return 0.5 * x * (1.0 + lax.erf(x * (1.0 / math.sqrt(2.0))))


def _gmlp_kernel(u_ref, v_ref, lng_ref, lnb_ref, ws_ref, bias_ref, y_ref):
    tt = u_ref.shape[1]
    v = _gelu(v_ref[0].astype(F32))
    mu = jnp.mean(v, axis=-1, keepdims=True)
    d = v - mu
    var = jnp.mean(d * d, axis=-1, keepdims=True)
    vn = d * lax.rsqrt(var + LN_EPS) * lng_ref[...] + lnb_ref[...]
    row = lax.broadcasted_iota(jnp.int32, (GCHUNK, GCHUNK), 0)
    col = lax.broadcasted_iota(jnp.int32, (GCHUNK, GCHUNK), 1)
    causal = row >= col
    first_head = lax.broadcasted_iota(jnp.int32, (1, LANES), 1) < HEAD
    ws = [jnp.where(causal, ws_ref[gi], 0.0).astype(BF16) for gi in range(MIX_HEADS)]
    for c in range(tt // GCHUNK):
        rows = slice(c * GCHUNK, (c + 1) * GCHUNK)
        outs = []
        for j in range(PAIRS):
            vp = vn[rows, j * LANES:(j + 1) * LANES].astype(BF16)
            outs.append(jnp.where(first_head, _dot(ws[2 * j], vp), _dot(ws[2 * j + 1], vp)))
        mixed = jnp.concatenate(outs, axis=1) + bias_ref[...]
        u = _gelu(u_ref[0, rows, :].astype(F32))
        y_ref[0, rows, :] = (u * mixed).astype(y_ref.dtype)


def _gmlp(p3, ln_g, ln_b, w_s, bias, tt):
    b, t, _ = p3.shape
    return pl.pallas_call(
        _gmlp_kernel,
        grid=(b, t // tt),
        in_specs=[
            pl.BlockSpec((1, tt, MIX), lambda i, j: (i, j, 0)),
            pl.BlockSpec((1, tt, MIX), lambda i, j: (i, j, 1)),
            pl.BlockSpec((1, MIX), lambda i, j: (0, 0)),
            pl.BlockSpec((1, MIX), lambda i, j: (0, 0)),
            pl.BlockSpec((MIX_HEADS, GCHUNK, GCHUNK), lambda i, j: (0, 0, 0)),
            pl.BlockSpec((GCHUNK, MIX), lambda i, j: (0, 0)),
        ],
        out_specs=pl.BlockSpec((1, tt, MIX), lambda i, j: (i, j, 0)),
        out_shape=jax.ShapeDtypeStruct((b, t, MIX), BF16),
        compiler_params=_cparams(("parallel", "parallel")),
        name="gmlp",
    )(p3, p3, ln_g.reshape(1, MIX), ln_b.reshape(1, MIX), w_s, bias)


def _rwkv_kernel(r_ref, k_ref, v_ref, l_ref,
                 mur_ref, muk_ref, muv_ref, mul_ref, wcat_ref,
                 w0_ref, a0_ref, kk_ref, ka_ref, rk_ref, lnw_ref, lnb_ref,
                 tril_ref,
                 y_ref,
                 pr_scr, pk_scr, pv_scr, pl_scr, state_scr,
                 at_scr, bt_scr, kt_scr, rt_scr, bp_scr, kp_scr, v_scr, pe_scr,
                 rk_scr, g_scr, y_scr,
                 wq_scr, bk_scr, pet_scr, u0_scr, y0_scr):
    tt = r_ref.shape[1]
    L = SCAN_CHUNK

    @pl.when(pl.program_id(1) == 0)
    def _():
        pr_scr[...] = jnp.zeros_like(pr_scr)
        pk_scr[...] = jnp.zeros_like(pk_scr)
        pv_scr[...] = jnp.zeros_like(pv_scr)
        pl_scr[...] = jnp.zeros_like(pl_scr)
        state_scr[...] = jnp.zeros_like(state_scr)

    GL = PREPARE_CHUNKS * L
    pair_h0 = lax.broadcasted_iota(jnp.int32, (1, LANES), 1) < HEAD

    def head_sums(x):
        s0 = jnp.sum(jnp.where(pair_h0, x, 0.0), axis=-1, keepdims=True)
        s1 = jnp.sum(jnp.where(pair_h0, 0.0, x), axis=-1, keepdims=True)
        return jnp.where(pair_h0, s0, s1)

    def shifted(ref, prev_scr, mu_ref, g):
        cur = ref[0, g * GL:(g + 1) * GL, :].astype(F32)
        if g == 0:
            before = prev_scr[...]
        else:
            before = ref[0, g * GL - SUBLANES_BF16:g * GL, :].astype(F32)[SUBLANES_BF16 - 1:]
        row = lax.broadcasted_iota(jnp.int32, cur.shape, 0)
        prev = jnp.where(row == 0, before, pltpu.roll(cur, 1, axis=0))
        return cur + (prev - cur) * mu_ref[...]

    def prologue(g):
        rows = slice(g * GL, (g + 1) * GL)
        r = shifted(r_ref, pr_scr, mur_ref, g)
        k = shifted(k_ref, pk_scr, muk_ref, g)
        v = shifted(v_ref, pv_scr, muv_ref, g)
        lo = shifted(l_ref, pl_scr, mul_ref, g)
        llane = lax.broadcasted_iota(jnp.int32, lo.shape, 1)
        z = jnp.where(llane < 64, jnp.tanh(lo), jnp.where(llane < 128, lo, jax.nn.sigmoid(lo)))
        proj = _dot(z.astype(BF16), wcat_ref[...])
        yield
        ld = -math.exp(-0.5) * jax.nn.sigmoid(w0_ref[...] + proj[:, :MIX])
        a = jax.nn.sigmoid(a0_ref[...] + proj[:, MIX:2 * MIX])
        kk = k * kk_ref[...]
        kk_sq = kk * kk
        ss = jnp.concatenate(
            [head_sums(kk_sq[:, j * LANES:(j + 1) * LANES]) for j in range(PAIRS)], axis=1)
        chunk_cum = [_split_dot(tril_ref[...], ld[c * L:(c + 1) * L]) for c in range(PREPARE_CHUNKS)]
        yield
        kk = kk * lax.rsqrt(jnp.maximum(ss, 1e-24))
        k = k * (1.0 + (a - 1.0) * ka_ref[...])
        b = kk * a
        cum = jnp.concatenate(chunk_cum, axis=0)
        cum_end = jnp.concatenate(
            [jnp.broadcast_to(cc[L - 1:L], (L, MIX)) for cc in chunk_cum], axis=0)
        p_inv = jnp.exp(-cum)
        to_end = jnp.exp(cum_end - cum)
        at_scr[rows] = -(kk * jnp.exp(cum - ld))
        bt_scr[rows] = b * p_inv
        kt_scr[rows] = k * p_inv
        rt_scr[rows] = r * jnp.exp(cum)
        yield
        bp_scr[rows] = b * to_end
        kp_scr[rows] = k * to_end
        v_scr[rows] = v
        pe_scr[rows] = jnp.exp(cum_end)
        rk_scr[rows] = r * k * rk_ref[...]
        g_scr[rows] = proj[:, 2 * MIX:]
        yield

    lane = lax.broadcasted_iota(jnp.int32, (1, LANES), 1)
    h0 = lane < HEAD
    row = lax.broadcasted_iota(jnp.int32, (L, LANES), 0)
    col = lax.broadcasted_iota(jnp.int32, (L, LANES), 1) % HEAD
    strict = row > col
    incl = row >= col
    srow = lax.broadcasted_iota(jnp.int32, (LANES, LANES), 0) < HEAD
    scol = lax.broadcasted_iota(jnp.int32, (LANES, LANES), 1) < HEAD
    same_head = srow == scol
    eye = jnp.where(lax.broadcasted_iota(jnp.int32, (LANES, LANES), 0)
                    == lax.broadcasted_iota(jnp.int32, (LANES, LANES), 1), 1.0, 0.0)
    pairs = range(PAIRS)

    def cat(x, y):
        return jnp.concatenate([x, y], axis=0)

    def first(x):
        return jnp.where(h0, x, 0.0)

    def second(x):
        return jnp.where(h0, 0.0, x)

    def lanes(x, y):
        return jnp.concatenate([x, y], axis=1)

    def tile(ref, c, j):
        return ref[c * L:(c + 1) * L, j * LANES:(j + 1) * LANES]


    def prepare(chunks):
        items = [(c, j) for c in chunks for j in pairs]
        idx = range(len(items))
        at = [tile(at_scr, c, j) for c, j in items]
        bt = [tile(bt_scr, c, j) for c, j in items]
        kt = [tile(kt_scr, c, j) for c, j in items]
        rt = [tile(rt_scr, c, j) for c, j in items]
        vc = [tile(v_scr, c, j) for c, j in items]
        g_0 = [_dot_nt(cat(first(at[i]), first(rt[i])).astype(BF16), cat(bt[i], kt[i]).astype(BF16))
               for i in idx]
        g_1 = [_dot_nt(cat(second(at[i]), second(rt[i])).astype(BF16), cat(kt[i], bt[i]).astype(BF16))
               for i in idx]
        yield
        top0 = [jnp.where(strict, g_0[i][:L], 0.0) for i in idx]
        top1 = [jnp.where(strict, g_1[i][:L], 0.0) for i in idx]
        vv = [cat(vc[i], vc[i]).astype(BF16) for i in idx]
        akv0 = [_dot(second(top0[i]).astype(BF16), vv[i]) for i in idx]
        akv1 = [_dot(first(top1[i]).astype(BF16), vv[i]) for i in idx]
        yield
        akv = [jnp.where(h0, akv0[i], akv1[i]) for i in idx]
        z = [lanes(cat(first(at[i]), second(at[i])), cat(first(akv[i]), second(akv[i])))
             .astype(BF16) for i in idx]
        apow = [cat(first(top0[i]), second(top1[i])) for i in idx]
        tinv = [eye + apow[i] for i in idx]
        apow = [_dot(apow[i].astype(BF16), apow[i].astype(BF16)) for i in idx]
        yield
        for step in range(5):
            ab = [apow[i].astype(BF16) for i in idx]
            if step < 4:
                both = [_dot(cat(apow[i], tinv[i]).astype(BF16), ab[i]) for i in idx]
                apow = [both[i][:LANES] for i in idx]
                tinv = [tinv[i] + both[i][LANES:] for i in idx]
            else:
                tinv = [tinv[i] + _dot(tinv[i].astype(BF16), ab[i]) for i in idx]
            yield
        z = [_dot(tinv[i].astype(BF16), z[i]) for i in idx]
        yield
        w = [z[i][:L, :LANES] + z[i][L:, :LANES] for i in idx]
        u0 = [z[i][:L, LANES:] + z[i][L:, LANES:] for i in idx]
        zero = jnp.zeros((L, LANES), F32)
        wu = [lanes(w[i], u0[i]) for i in idx]
        zv = [lanes(zero, vc[i]) for i in idx]
        r_0 = [_dot(jnp.where(incl, g_0[i][L:], 0.0).astype(BF16), cat(wu[i], zv[i]).astype(BF16))
               for i in idx]
        r_1 = [_dot(jnp.where(incl, g_1[i][L:], 0.0).astype(BF16), cat(zv[i], wu[i]).astype(BF16))
               for i in idx]
        yield
        for i, (c, j) in enumerate(items):
            q = rt[i] + jnp.where(h0, r_0[i][:, :LANES], r_1[i][:, :LANES])
            wq_scr[c, j] = cat(w[i], q).astype(BF16)
            u0_scr[c, j] = u0[i]
            y0_scr[c, j] = jnp.where(h0, r_0[i][:, LANES:], r_1[i][:, LANES:])
            bk_scr[c, j] = jnp.transpose(cat(tile(bp_scr, c, j), tile(kp_scr, c, j))).astype(BF16)
            pe_row = pe_scr[c * L:c * L + 1, j * LANES:(j + 1) * LANES]
            pet_scr[c, j] = jnp.transpose(jnp.broadcast_to(pe_row, (LANES, LANES)))
        yield

    def advance(c):
        h = [state_scr[j] for j in pairs]
        wqh = [_dot(wq_scr[c, j], h[j].astype(BF16)) for j in pairs]
        yield
        uv = [cat(wqh[j][:L] + u0_scr[c, j], tile(v_scr, c, j)).astype(BF16) for j in pairs]
        upd = [_dot(bk_scr[c, j], uv[j]) for j in pairs]
        for j in pairs:
            y_scr[c * L:(c + 1) * L, j * LANES:(j + 1) * LANES] = wqh[j][L:] + y0_scr[c, j]
        yield
        for j in pairs:
            state_scr[j] = h[j] * pet_scr[c, j] + jnp.where(same_head, upd[j], 0.0)
        yield

    def chain(gens):
        for gen in gens:
            yield from gen

    def epilogue(g):
        rows = slice(g * GL, (g + 1) * GL)
        tiles = [(rows, slice(j * LANES, (j + 1) * LANES)) for j in pairs]
        y = [y_scr[t] for t in tiles]
        mean = [head_sums(y[j]) * (1.0 / HEAD) for j in pairs]
        rk = [head_sums(rk_scr[tiles[j]]) for j in pairs]
        yield
        d = [y[j] - mean[j] for j in pairs]
        var = [head_sums(d[j] * d[j]) * (1.0 / HEAD) for j in pairs]
        yield
        outs = []
        for j in pairs:
            cols = tiles[j][1]
            yn = d[j] * lax.rsqrt(var[j] + GN_EPS) * lnw_ref[:, cols] + lnb_ref[:, cols]
            outs.append((yn + rk[j] * v_scr[tiles[j]]) * g_scr[tiles[j]])
        y_ref[0, rows, :] = jnp.concatenate(outs, axis=1).astype(y_ref.dtype)
        yield

    n_groups = tt // GL
    for _ in prologue(0):
        pass
    behind = iter(())
    for g in range(n_groups):
        ahead = prologue(g + 1) if g + 1 < n_groups else iter(())
        for _ in prepare(range(g * PREPARE_CHUNKS, (g + 1) * PREPARE_CHUNKS)):
            next(behind, None)
            next(ahead, None)
        for _ in ahead:
            pass
        for _ in behind:
            pass
        behind = chain([advance(c) for c in range(g * PREPARE_CHUNKS, (g + 1) * PREPARE_CHUNKS)]
                       + [epilogue(g)])
    for _ in behind:
        pass

    for ref, scr in ((r_ref, pr_scr), (k_ref, pk_scr), (v_ref, pv_scr), (l_ref, pl_scr)):
        scr[...] = ref[0, tt - SUBLANES_BF16:tt, :].astype(F32)[SUBLANES_BF16 - 1:]


def _rwkv(p3, mu, wcat, w0, a0, k_k, k_a, r_k, lnx_w, lnx_b, tt):
    b, t, _ = p3.shape
    row = lambda x: x.reshape(1, -1)
    idx = jnp.arange(SCAN_CHUNK)
    tril = (idx[:, None] >= idx[None, :]).astype(BF16)
    tril = jnp.concatenate([tril, tril], axis=1)
    const = lambda shape: pl.BlockSpec(shape, lambda i, j: (0,) * len(shape))
    tile = pltpu.VMEM((tt, MIX), F32)
    return pl.pallas_call(
        _rwkv_kernel,
        grid=(b, t // tt),
        in_specs=[
            pl.BlockSpec((1, tt, MIX), lambda i, j: (i, j, 0)),
            pl.BlockSpec((1, tt, MIX), lambda i, j: (i, j, 1)),
            pl.BlockSpec((1, tt, MIX), lambda i, j: (i, j, 2)),
            pl.BlockSpec((1, tt, LORA_W), lambda i, j: (i, j, 3 * MIX // LORA_W)),
            const((1, MIX)), const((1, MIX)), const((1, MIX)), const((1, LORA_W)),
            const((LORA_W, 3 * MIX)),
            const((1, MIX)), const((1, MIX)), const((1, MIX)), const((1, MIX)),
            const((1, MIX)), const((1, MIX)), const((1, MIX)),
            const((SCAN_CHUNK, 2 * SCAN_CHUNK)),
        ],
        out_specs=pl.BlockSpec((1, tt, MIX), lambda i, j: (i, j, 0)),
        out_shape=jax.ShapeDtypeStruct((b, t, MIX), BF16),
        scratch_shapes=[
            pltpu.VMEM((1, MIX), F32), pltpu.VMEM((1, MIX), F32), pltpu.VMEM((1, MIX), F32),
            pltpu.VMEM((1, LORA_W), F32),
            pltpu.VMEM((PAIRS, LANES, LANES), F32),
        ] + [tile] * 11 + [
            pltpu.VMEM((tt // SCAN_CHUNK, PAIRS, LANES, LANES), BF16),
            pltpu.VMEM((tt // SCAN_CHUNK, PAIRS, LANES, LANES), BF16),
            pltpu.VMEM((tt // SCAN_CHUNK, PAIRS, LANES, LANES), F32),
            pltpu.VMEM((tt // SCAN_CHUNK, PAIRS, SCAN_CHUNK, LANES), F32),
            pltpu.VMEM((tt // SCAN_CHUNK, PAIRS, SCAN_CHUNK, LANES), F32),
        ],
        compiler_params=_cparams(("parallel", "arbitrary")),
        name="rwkv7",
    )(p3, p3, p3, p3,
      row(mu[:MIX]), row(mu[MIX:2 * MIX]), row(mu[2 * MIX:3 * MIX]), row(mu[3 * MIX:]),
      wcat, row(w0), row(a0), row(k_k), row(k_a), row(r_k), row(lnx_w), row(lnx_b),
      tril)


def kernel(x, mem, mem_norm_g, norm1_g, w_kv_mem, w_out, norm2_g, rwkv_w_in, rwkv_mu, rwkv_w0, rwkv_w2, rwkv_a0, rwkv_a2, rwkv_g2, rwkv_k_k, rwkv_k_a, rwkv_r_k, rwkv_lnx_w, rwkv_lnx_b, ffn_w_gu, ffn_w_down, gmlp_w_in, gmlp_v_ln_g, gmlp_v_ln_b, gmlp_w_s, gmlp_b_s, moe_router, moe_w_gu, moe_w_down, final_norm_g):
    b, t, d = x.shape
    n_tok = b * t
    depth = norm1_g.shape[0]
    xs = x.reshape(n_tok, d)
    mem2 = mem.reshape(b * N_MEM, d)
    ready = {}

    def bf16(name, w):
        return ready.pop(name) if name in ready else w.astype(BF16)

    for i in range(depth):
        j = i // 2
        kv, _ = _norm_matmul(mem2, mem_norm_g, w_kv_mem[i].astype(BF16), PROJ_TILE)
        kv = kv.reshape(b, N_MEM, 2 * MEMW)
        if i % 2 == 0:
            todo = {("ffn_gu", j): ffn_w_gu[j], ("ffn_down", j): ffn_w_down[j]}
            if i + 1 < depth:
                todo[("gmlp_in", j)] = gmlp_w_in[j]
                todo[("w_out", i + 1)] = w_out[i + 1]
            p, copies = _norm_matmul(xs, norm1_g[i], rwkv_w_in[j].astype(BF16), PROJ_TILE,
                                     tuple(todo.values()))
            ready.update(zip(todo.keys(), copies))
            p3 = p.reshape(b, t, RWKV_COLS + MEMW)
            wcat = jnp.zeros((LORA_W, 3 * MIX), F32)
            wcat = wcat.at[0:64, 0:MIX].set(rwkv_w2[j])
            wcat = wcat.at[64:128, MIX:2 * MIX].set(rwkv_a2[j])
            wcat = wcat.at[128:256, 2 * MIX:].set(rwkv_g2[j])
            y = _rwkv(p3, rwkv_mu[j], wcat.astype(BF16), rwkv_w0[j], rwkv_a0[j], rwkv_k_k[j],
                      rwkv_k_a[j], rwkv_r_k[j].reshape(MIX), rwkv_lnx_w[j], rwkv_lnx_b[j], RWKV_TILE)
            q_block = RWKV_COLS // MEMW
        else:
            p, _ = _norm_matmul(xs, norm1_g[i], bf16(("gmlp_in", j), gmlp_w_in[j]), PROJ_TILE)
            p3 = p.reshape(b, t, 2 * MIX + MEMW)
            bias = jnp.repeat(jnp.transpose(gmlp_b_s[j]), HEAD, axis=1)
            y = _gmlp(p3, gmlp_v_ln_g[j], gmlp_v_ln_b[j], gmlp_w_s[j], bias, GMLP_TILE)
            q_block = 2 * MIX // MEMW
        wo = bf16(("w_out", i), w_out[i])
        xs = _mixer_out(xs.reshape(b, t, d), y, p3, q_block, kv, wo[:MIX], wo[MIX:],
                        MIXER_TILE).reshape(n_tok, d)
        if i % 2 == 0:
            pending = ()
            if j < moe_w_gu.shape[0]:
                pending = (moe_w_gu[j].reshape(-1, 2 * D_FF), moe_w_down[j].reshape(-1, d))
            xs, narrowed = _ffn(xs, norm2_g[i], bf16(("ffn_gu", j), ffn_w_gu[j]),
                                bf16(("ffn_down", j), ffn_w_down[j]), FFN_TILE, pending)
        else:
            last = i == depth - 1
            w_gu, w_down = (narrowed[0].reshape(moe_w_gu.shape[1:]),
                            narrowed[1].reshape(moe_w_down.shape[1:]))
            xs = _moe(xs, norm2_g[i], moe_router[j], w_gu, w_down, final_norm_g if last else None,
                      TOKEN_TILE, ROW_COPY_TILE, EXPERT_TILE)
    if depth % 2 == 1:
        xs = _final_norm(xs, final_norm_g, TOKEN_TILE)
    return xs.reshape(b, t, d)
```

```python
import functools
import math

import jax
import jax.numpy as jnp
from jax import lax
from jax.experimental import pallas as pl
from jax.experimental.pallas import tpu as pltpu

F32 = jnp.float32
BF16 = jnp.bfloat16

D_MODEL = 1024
HEAD = 64
MIX = 768
MIX_HEADS = MIX // HEAD
MEMW = 256
MEM_HEADS = 4
N_MEM = 256
LORA_W = 256
RWKV_COLS = 3 * MIX + LORA_W
GCHUNK = 128
D_FF = 2816
N_EXPERTS = 8
RMS_EPS = 1e-6
GN_EPS = 64e-5
LN_EPS = 1e-5

LANES = 128
SUBLANES_BF16 = 16
PAIRS = MIX // LANES
SCAN_CHUNK = 64
PREPARE_CHUNKS = 2
VMEM_LIMIT = 56 * 1024 * 1024

TOKEN_TILE = 512
PROJ_TILE = 1024
MIXER_TILE = 1024
RWKV_TILE = 512
GMLP_TILE = 512
FFN_TILE = 256
ROW_COPY_TILE = 512
EXPERT_TILE = 512


def _cparams(sem):
    return pltpu.CompilerParams(dimension_semantics=sem, vmem_limit_bytes=VMEM_LIMIT)


def _rms(x, g):
    ms = jnp.mean(x * x, axis=-1, keepdims=True)
    return x * lax.rsqrt(ms + RMS_EPS) * g


def _dot(a, b):
    return jnp.dot(a, b, preferred_element_type=F32)


def _dot_nt(a, b):
    return lax.dot_general(a, b, (((1,), (1,)), ((), ())), preferred_element_type=F32)


def _dot_tn(a, b):
    return lax.dot_general(a, b, (((0,), (0,)), ((), ())), preferred_element_type=F32)


def _split_dot(m2_bf16, x):
    hi = x.astype(BF16)
    lo = (x - hi.astype(F32)).astype(BF16)
    return _dot(m2_bf16, jnp.concatenate([hi, lo], axis=0))


def _norm_matmul_kernel(x_ref, g_ref, w_ref, *refs):
    n_narrow = len(refs) // 2
    n = _rms(x_ref[...], g_ref[...]).astype(BF16)
    refs[n_narrow][...] = _dot(n, w_ref[...]).astype(refs[n_narrow].dtype)
    for src, dst in zip(refs[:n_narrow], refs[n_narrow + 1:]):
        dst[...] = src[...].astype(BF16)


def _norm_matmul(x, g, w, tm, narrow=()):
    m, k = x.shape
    n = w.shape[1]
    steps = m // tm
    row_blocks = [pl.BlockSpec((a.shape[0] // steps, a.shape[1]), lambda i: (i, 0)) for a in narrow]
    out = pl.pallas_call(
        _norm_matmul_kernel,
        grid=(steps,),
        in_specs=[
            pl.BlockSpec((tm, k), lambda i: (i, 0)),
            pl.BlockSpec((1, k), lambda i: (0, 0)),
            pl.BlockSpec((k, n), lambda i: (0, 0)),
        ] + row_blocks,
        out_specs=[pl.BlockSpec((tm, n), lambda i: (i, 0))] + row_blocks,
        out_shape=[jax.ShapeDtypeStruct((m, n), BF16)]
        + [jax.ShapeDtypeStruct(a.shape, BF16) for a in narrow],
        compiler_params=_cparams(("parallel",)),
        name="norm_matmul",
    )(x, g.reshape(1, k), w, *narrow)
    return out[0], tuple(out[1:])


def _mem_attn(q, kv):
    k = kv[:, :MEMW].astype(BF16)
    v = kv[:, MEMW:].astype(BF16)
    lane = lax.broadcasted_iota(jnp.int32, (1, MEMW), 1)
    acc = jnp.zeros(q.shape, F32)
    for h in range(MEM_HEADS):
        m = (lane >= h * HEAD) & (lane < (h + 1) * HEAD)
        qh = jnp.where(m, q, 0.0).astype(BF16)
        s = _dot_nt(qh, k) * (1.0 / math.sqrt(HEAD))
        s = s - jnp.max(s, axis=-1, keepdims=True)
        p = jnp.exp(s)
        l = jnp.sum(p, axis=-1, keepdims=True)
        pv = _dot(p.astype(BF16), v)
        acc = acc + jnp.where(m, pv / l, 0.0)
    return acc


def _mixer_out_kernel(x_ref, y_ref, q_ref, kv_ref, wy_ref, wo_ref, out_ref):
    o = _mem_attn(q_ref[0], kv_ref[0])
    out_ref[0] = (x_ref[0]
                  + _dot(y_ref[0], wy_ref[...])
                  + _dot(o.astype(BF16), wo_ref[...]))


def _mixer_out(x3, y3, p3, q_block, kv3, wy, wo, tt):
    b, t, d = x3.shape
    return pl.pallas_call(
        _mixer_out_kernel,
        grid=(b, t // tt),
        in_specs=[
            pl.BlockSpec((1, tt, d), lambda i, j: (i, j, 0)),
            pl.BlockSpec((1, tt, MIX), lambda i, j: (i, j, 0)),
            pl.BlockSpec((1, tt, MEMW), lambda i, j: (i, j, q_block)),
            pl.BlockSpec((1, N_MEM, 2 * MEMW), lambda i, j: (i, 0, 0)),
            pl.BlockSpec((MIX, d), lambda i, j: (0, 0)),
            pl.BlockSpec((MEMW, d), lambda i, j: (0, 0)),
        ],
        out_specs=pl.BlockSpec((1, tt, d), lambda i, j: (i, j, 0)),
        out_shape=jax.ShapeDtypeStruct((b, t, d), F32),
        compiler_params=_cparams(("parallel", "parallel")),
        name="mixer_out",
    )(x3, y3, p3, kv3, wy, wo)


def _swiglu(n, wg_ref, wu_ref, wd_ref):
    gate = _dot(n, wg_ref[...])
    up = _dot(n, wu_ref[...])
    h = gate * jax.nn.sigmoid(gate) * up
    return _dot(h.astype(BF16), wd_ref[...])


def _ffn_kernel(x_ref, g_ref, wg_ref, wu_ref, wd_ref, *refs):
    narrow = len(refs) > 1
    out_ref = refs[2] if narrow else refs[0]
    if narrow:
        refs[3][...] = refs[0][...].astype(BF16)
    x = x_ref[...]
    out_ref[...] = x + _swiglu(_rms(x, g_ref[...]).astype(BF16), wg_ref, wu_ref, wd_ref)
    if narrow:
        refs[4][...] = refs[1][...].astype(BF16)


def _ffn(x, g, w_gu, w_down, tm, narrow=()):
    m, d = x.shape
    steps = m // tm
    resident = pl.Buffered(1)
    row_blocks = [pl.BlockSpec((a.shape[0] // steps, a.shape[1]), lambda i: (i, 0)) for a in narrow]
    out = pl.pallas_call(
        _ffn_kernel,
        grid=(steps,),
        in_specs=[
            pl.BlockSpec((tm, d), lambda i: (i, 0)),
            pl.BlockSpec((1, d), lambda i: (0, 0)),
            pl.BlockSpec((d, D_FF), lambda i: (0, 0), pipeline_mode=resident),
            pl.BlockSpec((d, D_FF), lambda i: (0, 1), pipeline_mode=resident),
            pl.BlockSpec((D_FF, d), lambda i: (0, 0), pipeline_mode=resident),
        ] + row_blocks,
        out_specs=[pl.BlockSpec((tm, d), lambda i: (i, 0))] + row_blocks,
        out_shape=[jax.ShapeDtypeStruct((m, d), F32)]
        + [jax.ShapeDtypeStruct(a.shape, BF16) for a in narrow],
        compiler_params=_cparams(("parallel",)),
        name="ffn",
    )(x, g.reshape(1, d), w_gu, w_gu, w_down, *narrow)
    return out[0], tuple(out[1:])


ROUTE_E1, ROUTE_E2, ROUTE_R1, ROUTE_R2, ROUTE_G1, ROUTE_G2 = range(6)
ROUTE_FIELDS = 8


def _route_tile(x, first, g_ref, wr_ref, tril_ref, route_ref, routet_ref, cnt_ref, carry_scr):
    @pl.when(first)
    def _():
        carry_scr[...] = jnp.zeros_like(carry_scr)

    n = _rms(x, g_ref[...])
    n_hi = n.astype(BF16)
    n_lo = (n - n_hi.astype(F32)).astype(BF16)
    logits = _dot(jnp.concatenate([n_hi, n_hi, n_lo], axis=1), wr_ref[...])
    lane = lax.broadcasted_iota(jnp.int32, logits.shape, 1)
    neg = jnp.float32(-jnp.inf)
    lg = jnp.where(lane < N_EXPERTS, logits, neg)
    m1 = jnp.max(lg, axis=-1, keepdims=True)
    i1 = jnp.min(jnp.where(lg == m1, lane, LANES), axis=-1, keepdims=True)
    lg2 = jnp.where(lane == i1, neg, lg)
    m2 = jnp.max(lg2, axis=-1, keepdims=True)
    i2 = jnp.min(jnp.where(lg2 == m2, lane, LANES), axis=-1, keepdims=True)
    e2 = jnp.exp(m2 - m1)
    g1 = 1.0 / (1.0 + e2)
    g2 = e2 / (1.0 + e2)
    chosen = jnp.where(lane == i1, 1.0, jnp.where(lane == i2, 1.0, 0.0))
    before = _dot(tril_ref[...], chosen.astype(BF16)) + carry_scr[...]
    r1 = jnp.sum(jnp.where(lane == i1, before, 0.0), axis=-1, keepdims=True)
    r2 = jnp.sum(jnp.where(lane == i2, before, 0.0), axis=-1, keepdims=True)
    carry_scr[...] += jnp.sum(chosen, axis=0, keepdims=True)
    cnt_ref[...] = carry_scr[...]
    fields = (i1.astype(F32), i2.astype(F32), r1, r2, g1, g2)
    route = jnp.zeros(lane.shape, F32)
    for idx, val in enumerate(fields):
        route = jnp.where(lane == idx, val, route)
    route_ref[...] = route
    routet_ref[...] = jnp.transpose(route)[:ROUTE_FIELDS]


def _route_kernel(x_ref, g_ref, wr_ref, tril_ref, route_ref, routet_ref, cnt_ref, carry_scr):
    _route_tile(x_ref[...], pl.program_id(0) == 0, g_ref, wr_ref, tril_ref,
                route_ref, routet_ref, cnt_ref, carry_scr)


def _route(x, g, w_router, tm):
    m, d = x.shape
    wr_pad = jnp.zeros((d, LANES), F32).at[:, :N_EXPERTS].set(w_router)
    wr_hi = wr_pad.astype(BF16)
    wr_lo = (wr_pad - wr_hi.astype(F32)).astype(BF16)
    wr_split = jnp.concatenate([wr_hi, wr_lo, wr_hi], axis=0)
    idx = jnp.arange(tm)
    tril = (idx[:, None] > idx[None, :]).astype(BF16)
    return pl.pallas_call(
        _route_kernel,
        grid=(m // tm,),
        in_specs=[
            pl.BlockSpec((tm, d), lambda i: (i, 0)),
            pl.BlockSpec((1, d), lambda i: (0, 0)),
            pl.BlockSpec((3 * d, LANES), lambda i: (0, 0)),
            pl.BlockSpec((tm, tm), lambda i: (0, 0)),
        ],
        out_specs=[pl.BlockSpec((tm, LANES), lambda i: (i, 0)),
                   pl.BlockSpec((ROUTE_FIELDS, tm), lambda i: (0, i)),
                   pl.BlockSpec((1, LANES), lambda i: (0, 0))],
        out_shape=[jax.ShapeDtypeStruct((m, LANES), F32),
                   jax.ShapeDtypeStruct((ROUTE_FIELDS, m), F32),
                   jax.ShapeDtypeStruct((1, LANES), F32)],
        scratch_shapes=[pltpu.VMEM((1, LANES), F32)],
        compiler_params=_cparams(("arbitrary",)),
        name="route",
    )(x, g.reshape(1, d), wr_split, tril)


ROW_GROUP = 8


def _row_loop(tm, body):
    def step(group, carry):
        for sub in range(ROW_GROUP):
            body(group, sub)
        return carry
    lax.fori_loop(0, tm // ROW_GROUP, step, 0)


def _dispatch_kernel(p1_ref, p2_ref, fill_ref, x_ref, g_ref, xs_ref, n_scr, zero_scr, sem, fill_sem):
    tm = x_ref.shape[0]
    tile_rows = zero_scr.shape[0]
    step = pl.program_id(0)

    def fill_copy(f):
        return pltpu.make_async_copy(
            zero_scr, xs_ref.at[pl.ds(pl.multiple_of(fill_ref[f] * tile_rows, tile_rows), tile_rows)],
            fill_sem)

    @pl.when(step == 0)
    def _():
        zero_scr[...] = jnp.zeros_like(zero_scr)
        for f in range(fill_ref.shape[0]):
            @pl.when(fill_ref[f] >= 0)
            def _():
                fill_copy(f).start()
        for f in range(fill_ref.shape[0]):
            @pl.when(fill_ref[f] >= 0)
            def _():
                fill_copy(f).wait()

    def copies(s, group, sub):
        slot = s % 2
        tok = s * tm + group * ROW_GROUP + sub
        src = n_scr.at[slot, group, pl.ds(sub, 1)]
        return (pltpu.make_async_copy(src, xs_ref.at[pl.ds(p1_ref[tok], 1)], sem.at[slot]),
                pltpu.make_async_copy(src, xs_ref.at[pl.ds(p2_ref[tok], 1)], sem.at[slot]))

    def start(s):
        def body(group, sub):
            first, second = copies(s, group, sub)
            first.start()
            second.start(priority=1)
        _row_loop(tm, body)

    def wait(s):
        def body(group, sub):
            first, second = copies(s, group, sub)
            first.wait()
            second.wait()
        _row_loop(tm, body)

    n_scr[step % 2] = _rms(x_ref[...], g_ref[...]).reshape(n_scr.shape[1:])
    start(step)

    @pl.when(step > 0)
    def _():
        wait(step - 1)

    @pl.when(step == pl.num_programs(0) - 1)
    def _():
        wait(step)


def _dispatch(x, g, pos1, pos2, fill, rows, tm, tile_rows):
    m, d = x.shape
    return pl.pallas_call(
        _dispatch_kernel,
        grid_spec=pltpu.PrefetchScalarGridSpec(
            num_scalar_prefetch=3,
            grid=(m // tm,),
            in_specs=[
                pl.BlockSpec((tm, d), lambda i, p1, p2, fl: (i, 0)),
                pl.BlockSpec((1, d), lambda i, p1, p2, fl: (0, 0)),
            ],
            out_specs=pl.BlockSpec(memory_space=pl.ANY),
            scratch_shapes=[pltpu.VMEM((2, tm // ROW_GROUP, ROW_GROUP, d), F32),
                            pltpu.VMEM((tile_rows, d), F32),
                            pltpu.SemaphoreType.DMA((2,)), pltpu.SemaphoreType.DMA],
        ),
        out_shape=jax.ShapeDtypeStruct((rows, d), F32),
        compiler_params=_cparams(("arbitrary",)),
        name="dispatch",
    )(pos1, pos2, fill, x, g.reshape(1, d))


def _experts_kernel(te_ref, nv_ref, x_ref, wg_ref, wu_ref, wd_ref, y_ref):
    del te_ref
    valid = pl.program_id(0) < nv_ref[0]

    @pl.when(valid)
    def _():
        y_ref[...] = _swiglu(x_ref[...].astype(BF16), wg_ref, wu_ref, wd_ref)

    @pl.when(jnp.logical_not(valid))
    def _():
        y_ref[...] = jnp.zeros_like(y_ref)


def _experts(xs, tile_expert, n_valid, w_gu, w_down, tm):
    rows, d = xs.shape
    return pl.pallas_call(
        _experts_kernel,
        grid_spec=pltpu.PrefetchScalarGridSpec(
            num_scalar_prefetch=2,
            grid=(rows // tm,),
            in_specs=[
                pl.BlockSpec((tm, d), lambda t, te, nv: (t, 0)),
                pl.BlockSpec((None, d, D_FF), lambda t, te, nv: (te[t], 0, 0)),
                pl.BlockSpec((None, d, D_FF), lambda t, te, nv: (te[t], 0, 1)),
                pl.BlockSpec((None, D_FF, d), lambda t, te, nv: (te[t], 0, 0)),
            ],
            out_specs=pl.BlockSpec((tm, d), lambda t, te, nv: (t, 0)),
        ),
        out_shape=jax.ShapeDtypeStruct((rows, d), F32),
        compiler_params=_cparams(("arbitrary",)),
        name="experts",
    )(tile_expert, n_valid, xs, w_gu, w_gu, w_down)


def _combine_kernel(p1_ref, p2_ref, x_ref, route_ref, fg_ref, ys_ref, out_ref, buf, sem, *, final):
    tm = x_ref.shape[0]
    step = pl.program_id(0)

    def copies(s, group, sub):
        slot = s % 2
        tok = s * tm + group * ROW_GROUP + sub
        return (pltpu.make_async_copy(ys_ref.at[pl.ds(p1_ref[tok], 1)],
                                      buf.at[slot, 0, group, pl.ds(sub, 1)], sem.at[slot]),
                pltpu.make_async_copy(ys_ref.at[pl.ds(p2_ref[tok], 1)],
                                      buf.at[slot, 1, group, pl.ds(sub, 1)], sem.at[slot]))

    def start(s):
        def body(group, sub):
            first, second = copies(s, group, sub)
            first.start()
            second.start(priority=1)
        _row_loop(tm, body)

    def wait(s):
        def body(group, sub):
            first, second = copies(s, group, sub)
            first.wait()
            second.wait()
        _row_loop(tm, body)

    @pl.when(step == 0)
    def _():
        start(step)

    @pl.when(step + 1 < pl.num_programs(0))
    def _():
        start(step + 1)

    wait(step)
    rows = buf[step % 2].reshape(2, tm, x_ref.shape[1])
    route = route_ref[...]
    g1 = route[:, ROUTE_G1:ROUTE_G1 + 1]
    g2 = route[:, ROUTE_G2:ROUTE_G2 + 1]
    out = x_ref[...] + g1 * rows[0] + g2 * rows[1]
    out_ref[...] = _rms(out, fg_ref[...]) if final else out


def _combine(x, route, ys, pos1, pos2, final_g, tm):
    m, d = x.shape
    final = final_g is not None
    fg = final_g.reshape(1, d) if final else jnp.ones((1, d), F32)
    return pl.pallas_call(
        functools.partial(_combine_kernel, final=final),
        grid_spec=pltpu.PrefetchScalarGridSpec(
            num_scalar_prefetch=2,
            grid=(m // tm,),
            in_specs=[
                pl.BlockSpec((tm, d), lambda i, p1, p2: (i, 0)),
                pl.BlockSpec((tm, LANES), lambda i, p1, p2: (i, 0)),
                pl.BlockSpec((1, d), lambda i, p1, p2: (0, 0)),
                pl.BlockSpec(memory_space=pl.ANY),
            ],
            out_specs=pl.BlockSpec((tm, d), lambda i, p1, p2: (i, 0)),
            scratch_shapes=[pltpu.VMEM((2, 2, tm // ROW_GROUP, ROW_GROUP, d), F32),
                            pltpu.SemaphoreType.DMA((2,))],
        ),
        out_shape=jax.ShapeDtypeStruct((m, d), F32),
        compiler_params=_cparams(("arbitrary",)),
        name="combine",
    )(pos1, pos2, x, route, fg, ys)


def _moe(x, g, w_router, w_gu, w_down, final_g, tm_route, tm_rows, tm_expert):
    m, d = x.shape
    route, route_t, counts = _route(x, g, w_router, tm_route)
    counts = counts[0, :N_EXPERTS].astype(jnp.int32)
    tiles = (counts + tm_expert - 1) // tm_expert
    experts = jnp.arange(N_EXPERTS, dtype=jnp.int32)
    tile_end = jnp.sum(jnp.where(experts[None, :] <= experts[:, None], tiles[None, :], 0), axis=1)
    row_start = (tile_end - tiles) * tm_expert

    def slot(e_row, r_row):
        e = route_t[e_row].astype(jnp.int32)
        start = jnp.sum(jnp.where(e[:, None] == experts[None, :], row_start[None, :], 0), axis=1)
        return start + route_t[r_row].astype(jnp.int32)

    pos1 = slot(ROUTE_E1, ROUTE_R1)
    pos2 = slot(ROUTE_E2, ROUTE_R2)
    n_tiles = 2 * m // tm_expert + N_EXPERTS
    tile_ids = jnp.arange(n_tiles, dtype=jnp.int32)
    tile_expert = jnp.minimum(
        jnp.sum((tile_ids[:, None] >= tile_end[None, :]).astype(jnp.int32), axis=1), N_EXPERTS - 1)
    n_valid = tile_end[-1:]
    tail = n_valid + experts
    fill = jnp.concatenate([jnp.where(tiles > 0, tile_end - 1, -1),
                            jnp.where(tail < n_tiles, tail, -1)])
    xs = _dispatch(x, g, pos1, pos2, fill, n_tiles * tm_expert, tm_rows, tm_expert)
    ys = _experts(xs, tile_expert, n_valid, w_gu, w_down, tm_expert)
    return _combine(x, route, ys, pos1, pos2, final_g, tm_rows)


def _final_norm_kernel(x_ref, g_ref, o_ref):
    o_ref[...] = _rms(x_ref[...], g_ref[...])


def _final_norm(x, g, tm):
    m, d = x.shape
    return pl.pallas_call(
        _final_norm_kernel,
        grid=(m // tm,),
        in_specs=[pl.BlockSpec((tm, d), lambda i: (i, 0)), pl.BlockSpec((1, d), lambda i: (0, 0))],
        out_specs=pl.BlockSpec((tm, d), lambda i: (i, 0)),
        out_shape=jax.ShapeDtypeStruct((m, d), F32),
        compiler_params=_cparams(("parallel",)),
        name="final_norm",
    )(x, g.reshape(1, d))


def _gelu(x):
    return 0.5 * x * (1.0 + lax.erf(x * (1.0 / math.sqrt(2.0))))


def _gmlp_kernel(u_ref, v_ref, lng_ref, lnb_ref, ws_ref, bias_ref, y_ref):
    tt = u_ref.shape[1]
    v = _gelu(v_ref[0].astype(F32))
    mu = jnp.mean(v, axis=-1, keepdims=True)
    d = v - mu
    var = jnp.mean(d * d, axis=-1, keepdims=True)
    vn = d * lax.rsqrt(var + LN_EPS) * lng_ref[...] + lnb_ref[...]
    row = lax.broadcasted_iota(jnp.int32, (GCHUNK, GCHUNK), 0)
    col = lax.broadcasted_iota(jnp.int32, (GCHUNK, GCHUNK), 1)
    causal = row >= col
    first_head = lax.broadcasted_iota(jnp.int32, (1, LANES), 1) < HEAD
    ws = [jnp.where(causal, ws_ref[gi], 0.0).astype(BF16) for gi in range(MIX_HEADS)]
    for c in range(tt // GCHUNK):
        rows = slice(c * GCHUNK, (c + 1) * GCHUNK)
        outs = []
        for j in range(PAIRS):
            vp = vn[rows, j * LANES:(j + 1) * LANES].astype(BF16)
            outs.append(jnp.where(first_head, _dot(ws[2 * j], vp), _dot(ws[2 * j + 1], vp)))
        mixed = jnp.concatenate(outs, axis=1) + bias_ref[...]
        u = _gelu(u_ref[0, rows, :].astype(F32))
        y_ref[0, rows, :] = (u * mixed).astype(y_ref.dtype)


def _gmlp(p3, ln_g, ln_b, w_s, bias, tt):
    b, t, _ = p3.shape
    return pl.pallas_call(
        _gmlp_kernel,
        grid=(b, t // tt),
        in_specs=[
            pl.BlockSpec((1, tt, MIX), lambda i, j: (i, j, 0)),
            pl.BlockSpec((1, tt, MIX), lambda i, j: (i, j, 1)),
            pl.BlockSpec((1, MIX), lambda i, j: (0, 0)),
            pl.BlockSpec((1, MIX), lambda i, j: (0, 0)),
            pl.BlockSpec((MIX_HEADS, GCHUNK, GCHUNK), lambda i, j: (0, 0, 0)),
            pl.BlockSpec((GCHUNK, MIX), lambda i, j: (0, 0)),
        ],
        out_specs=pl.BlockSpec((1, tt, MIX), lambda i, j: (i, j, 0)),
        out_shape=jax.ShapeDtypeStruct((b, t, MIX), BF16),
        compiler_params=_cparams(("parallel", "parallel")),
        name="gmlp",
    )(p3, p3, ln_g.reshape(1, MIX), ln_b.reshape(1, MIX), w_s, bias)


def _rwkv_kernel(r_ref, k_ref, v_ref, l_ref,
                 mur_ref, muk_ref, muv_ref, mul_ref, wcat_ref,
                 w0_ref, a0_ref, kk_ref, ka_ref, rk_ref, lnw_ref, lnb_ref,
                 tril_ref,
                 y_ref,
                 pr_scr, pk_scr, pv_scr, pl_scr, state_scr,
                 at_scr, bt_scr, kt_scr, rt_scr, bp_scr, kp_scr, v_scr, pe_scr,
                 rk_scr, g_scr, y_scr,
                 wq_scr, bk_scr, pet_scr, u0_scr, y0_scr):
    tt = r_ref.shape[1]
    L = SCAN_CHUNK

    @pl.when(pl.program_id(1) == 0)
    def _():
        pr_scr[...] = jnp.zeros_like(pr_scr)
        pk_scr[...] = jnp.zeros_like(pk_scr)
        pv_scr[...] = jnp.zeros_like(pv_scr)
        pl_scr[...] = jnp.zeros_like(pl_scr)
        state_scr[...] = jnp.zeros_like(state_scr)

    GL = PREPARE_CHUNKS * L
    pair_h0 = lax.broadcasted_iota(jnp.int32, (1, LANES), 1) < HEAD

    def head_sums(x):
        s0 = jnp.sum(jnp.where(pair_h0, x, 0.0), axis=-1, keepdims=True)
        s1 = jnp.sum(jnp.where(pair_h0, 0.0, x), axis=-1, keepdims=True)
        return jnp.where(pair_h0, s0, s1)

    def shifted(ref, prev_scr, mu_ref, g):
        cur = ref[0, g * GL:(g + 1) * GL, :].astype(F32)
        if g == 0:
            before = prev_scr[...]
        else:
            before = ref[0, g * GL - SUBLANES_BF16:g * GL, :].astype(F32)[SUBLANES_BF16 - 1:]
        row = lax.broadcasted_iota(jnp.int32, cur.shape, 0)
        prev = jnp.where(row == 0, before, pltpu.roll(cur, 1, axis=0))
        return cur + (prev - cur) * mu_ref[...]

    def prologue(g):
        rows = slice(g * GL, (g + 1) * GL)
        r = shifted(r_ref, pr_scr, mur_ref, g)
        k = shifted(k_ref, pk_scr, muk_ref, g)
        v = shifted(v_ref, pv_scr, muv_ref, g)
        lo = shifted(l_ref, pl_scr, mul_ref, g)
        llane = lax.broadcasted_iota(jnp.int32, lo.shape, 1)
        z = jnp.where(llane < 64, jnp.tanh(lo), jnp.where(llane < 128, lo, jax.nn.sigmoid(lo)))
        proj = _dot(z.astype(BF16), wcat_ref[...])
        yield
        ld = -math.exp(-0.5) * jax.nn.sigmoid(w0_ref[...] + proj[:, :MIX])
        a = jax.nn.sigmoid(a0_ref[...] + proj[:, MIX:2 * MIX])
        kk = k * kk_ref[...]
        kk_sq = kk * kk
        ss = jnp.concatenate(
            [head_sums(kk_sq[:, j * LANES:(j + 1) * LANES]) for j in range(PAIRS)], axis=1)
        chunk_cum = [_split_dot(tril_ref[...], ld[c * L:(c + 1) * L]) for c in range(PREPARE_CHUNKS)]
        yield
        kk = kk * lax.rsqrt(jnp.maximum(ss, 1e-24))
        k = k * (1.0 + (a - 1.0) * ka_ref[...])
        b = kk * a
        cum = jnp.concatenate(chunk_cum, axis=0)
        cum_end = jnp.concatenate(
            [jnp.broadcast_to(cc[L - 1:L], (L, MIX)) for cc in chunk_cum], axis=0)
        p_inv = jnp.exp(-cum)
        to_end = jnp.exp(cum_end - cum)
        at_scr[rows] = -(kk * jnp.exp(cum - ld))
        bt_scr[rows] = b * p_inv
        kt_scr[rows] = k * p_inv
        rt_scr[rows] = r * jnp.exp(cum)
        yield
        bp_scr[rows] = b * to_end
        kp_scr[rows] = k * to_end
        v_scr[rows] = v
        pe_scr[rows] = jnp.exp(cum_end)
        rk_scr[rows] = r * k * rk_ref[...]
        g_scr[rows] = proj[:, 2 * MIX:]
        yield

    lane = lax.broadcasted_iota(jnp.int32, (1, LANES), 1)
    h0 = lane < HEAD
    row = lax.broadcasted_iota(jnp.int32, (L, LANES), 0)
    col = lax.broadcasted_iota(jnp.int32, (L, LANES), 1) % HEAD
    strict = row > col
    incl = row >= col
    srow = lax.broadcasted_iota(jnp.int32, (LANES, LANES), 0) < HEAD
    scol = lax.broadcasted_iota(jnp.int32, (LANES, LANES), 1) < HEAD
    same_head = srow == scol
    eye = jnp.where(lax.broadcasted_iota(jnp.int32, (LANES, LANES), 0)
                    == lax.broadcasted_iota(jnp.int32, (LANES, LANES), 1), 1.0, 0.0)
    pairs = range(PAIRS)

    def cat(x, y):
        return jnp.concatenate([x, y], axis=0)

    def first(x):
        return jnp.where(h0, x, 0.0)

    def second(x):
        return jnp.where(h0, 0.0, x)

    def lanes(x, y):
        return jnp.concatenate([x, y], axis=1)

    def tile(ref, c, j):
        return ref[c * L:(c + 1) * L, j * LANES:(j + 1) * LANES]


    def prepare(chunks):
        items = [(c, j) for c in chunks for j in pairs]
        idx = range(len(items))
        at = [tile(at_scr, c, j) for c, j in items]
        bt = [tile(bt_scr, c, j) for c, j in items]
        kt = [tile(kt_scr, c, j) for c, j in items]
        rt = [tile(rt_scr, c, j) for c, j in items]
        vc = [tile(v_scr, c, j) for c, j in items]
        g_0 = [_dot_nt(cat(first(at[i]), first(rt[i])).astype(BF16), cat(bt[i], kt[i]).astype(BF16))
               for i in idx]
        g_1 = [_dot_nt(cat(second(at[i]), second(rt[i])).astype(BF16), cat(kt[i], bt[i]).astype(BF16))
               for i in idx]
        yield
        top0 = [jnp.where(strict, g_0[i][:L], 0.0) for i in idx]
        top1 = [jnp.where(strict, g_1[i][:L], 0.0) for i in idx]
        vv = [cat(vc[i], vc[i]).astype(BF16) for i in idx]
        akv0 = [_dot(second(top0[i]).astype(BF16), vv[i]) for i in idx]
        akv1 = [_dot(first(top1[i]).astype(BF16), vv[i]) for i in idx]
        yield
        akv = [jnp.where(h0, akv0[i], akv1[i]) for i in idx]
        z = [lanes(cat(first(at[i]), second(at[i])), cat(first(akv[i]), second(akv[i])))
             .astype(BF16) for i in idx]
        apow = [cat(first(top0[i]), second(top1[i])) for i in idx]
        tinv = [eye + apow[i] for i in idx]
        apow = [_dot(apow[i].astype(BF16), apow[i].astype(BF16)) for i in idx]
        yield
        for step in range(5):
            ab = [apow[i].astype(BF16) for i in idx]
            if step < 4:
                both = [_dot(cat(apow[i], tinv[i]).astype(BF16), ab[i]) for i in idx]
                apow = [both[i][:LANES] for i in idx]
                tinv = [tinv[i] + both[i][LANES:] for i in idx]
            else:
                tinv = [tinv[i] + _dot(tinv[i].astype(BF16), ab[i]) for i in idx]
            yield
        z = [_dot(tinv[i].astype(BF16), z[i]) for i in idx]
        yield
        w = [z[i][:L, :LANES] + z[i][L:, :LANES] for i in idx]
        u0 = [z[i][:L, LANES:] + z[i][L:, LANES:] for i in idx]
        zero = jnp.zeros((L, LANES), F32)
        wu = [lanes(w[i], u0[i]) for i in idx]
        zv = [lanes(zero, vc[i]) for i in idx]
        r_0 = [_dot(jnp.where(incl, g_0[i][L:], 0.0).astype(BF16), cat(wu[i], zv[i]).astype(BF16))
               for i in idx]
        r_1 = [_dot(jnp.where(incl, g_1[i][L:], 0.0).astype(BF16), cat(zv[i], wu[i]).astype(BF16))
               for i in idx]
        yield
        for i, (c, j) in enumerate(items):
            q = rt[i] + jnp.where(h0, r_0[i][:, :LANES], r_1[i][:, :LANES])
            wq_scr[c, j] = cat(w[i], q).astype(BF16)
            u0_scr[c, j] = u0[i]
            y0_scr[c, j] = jnp.where(h0, r_0[i][:, LANES:], r_1[i][:, LANES:])
            bk_scr[c, j] = jnp.transpose(cat(tile(bp_scr, c, j), tile(kp_scr, c, j))).astype(BF16)
            pe_row = pe_scr[c * L:c * L + 1, j * LANES:(j + 1) * LANES]
            pet_scr[c, j] = jnp.transpose(jnp.broadcast_to(pe_row, (LANES, LANES)))
        yield

    def advance(c):
        h = [state_scr[j] for j in pairs]
        wqh = [_dot(wq_scr[c, j], h[j].astype(BF16)) for j in pairs]
        yield
        uv = [cat(wqh[j][:L] + u0_scr[c, j], tile(v_scr, c, j)).astype(BF16) for j in pairs]
        upd = [_dot(bk_scr[c, j], uv[j]) for j in pairs]
        for j in pairs:
            y_scr[c * L:(c + 1) * L, j * LANES:(j + 1) * LANES] = wqh[j][L:] + y0_scr[c, j]
        yield
        for j in pairs:
            state_scr[j] = h[j] * pet_scr[c, j] + jnp.where(same_head, upd[j], 0.0)
        yield

    def chain(gens):
        for gen in gens:
            yield from gen

    def epilogue(g):
        rows = slice(g * GL, (g + 1) * GL)
        tiles = [(rows, slice(j * LANES, (j + 1) * LANES)) for j in pairs]
        y = [y_scr[t] for t in tiles]
        mean = [head_sums(y[j]) * (1.0 / HEAD) for j in pairs]
        rk = [head_sums(rk_scr[tiles[j]]) for j in pairs]
        yield
        d = [y[j] - mean[j] for j in pairs]
        var = [head_sums(d[j] * d[j]) * (1.0 / HEAD) for j in pairs]
        yield
        outs = []
        for j in pairs:
            cols = tiles[j][1]
            yn = d[j] * lax.rsqrt(var[j] + GN_EPS) * lnw_ref[:, cols] + lnb_ref[:, cols]
            outs.append((yn + rk[j] * v_scr[tiles[j]]) * g_scr[tiles[j]])
        y_ref[0, rows, :] = jnp.concatenate(outs, axis=1).astype(y_ref.dtype)
        yield

    n_groups = tt // GL
    for _ in prologue(0):
        pass
    behind = iter(())
    for g in range(n_groups):
        ahead = prologue(g + 1) if g + 1 < n_groups else iter(())
        for _ in prepare(range(g * PREPARE_CHUNKS, (g + 1) * PREPARE_CHUNKS)):
            next(behind, None)
            next(ahead, None)
        for _ in ahead:
            pass
        for _ in behind:
            pass
        behind = chain([advance(c) for c in range(g * PREPARE_CHUNKS, (g + 1) * PREPARE_CHUNKS)]
                       + [epilogue(g)])
    for _ in behind:
        pass

    for ref, scr in ((r_ref, pr_scr), (k_ref, pk_scr), (v_ref, pv_scr), (l_ref, pl_scr)):
        scr[...] = ref[0, tt - SUBLANES_BF16:tt, :].astype(F32)[SUBLANES_BF16 - 1:]


def _rwkv(p3, mu, wcat, w0, a0, k_k, k_a, r_k, lnx_w, lnx_b, tt):
    b, t, _ = p3.shape
    row = lambda x: x.reshape(1, -1)
    idx = jnp.arange(SCAN_CHUNK)
    tril = (idx[:, None] >= idx[None, :]).astype(BF16)
    tril = jnp.concatenate([tril, tril], axis=1)
    const = lambda shape: pl.BlockSpec(shape, lambda i, j: (0,) * len(shape))
    tile = pltpu.VMEM((tt, MIX), F32)
    return pl.pallas_call(
        _rwkv_kernel,
        grid=(b, t // tt),
        in_specs=[
            pl.BlockSpec((1, tt, MIX), lambda i, j: (i, j, 0)),
            pl.BlockSpec((1, tt, MIX), lambda i, j: (i, j, 1)),
            pl.BlockSpec((1, tt, MIX), lambda i, j: (i, j, 2)),
            pl.BlockSpec((1, tt, LORA_W), lambda i, j: (i, j, 3 * MIX // LORA_W)),
            const((1, MIX)), const((1, MIX)), const((1, MIX)), const((1, LORA_W)),
            const((LORA_W, 3 * MIX)),
            const((1, MIX)), const((1, MIX)), const((1, MIX)), const((1, MIX)),
            const((1, MIX)), const((1, MIX)), const((1, MIX)),
            const((SCAN_CHUNK, 2 * SCAN_CHUNK)),
        ],
        out_specs=pl.BlockSpec((1, tt, MIX), lambda i, j: (i, j, 0)),
        out_shape=jax.ShapeDtypeStruct((b, t, MIX), BF16),
        scratch_shapes=[
            pltpu.VMEM((1, MIX), F32), pltpu.VMEM((1, MIX), F32), pltpu.VMEM((1, MIX), F32),
            pltpu.VMEM((1, LORA_W), F32),
            pltpu.VMEM((PAIRS, LANES, LANES), F32),
        ] + [tile] * 11 + [
            pltpu.VMEM((tt // SCAN_CHUNK, PAIRS, LANES, LANES), BF16),
            pltpu.VMEM((tt // SCAN_CHUNK, PAIRS, LANES, LANES), BF16),
            pltpu.VMEM((tt // SCAN_CHUNK, PAIRS, LANES, LANES), F32),
            pltpu.VMEM((tt // SCAN_CHUNK, PAIRS, SCAN_CHUNK, LANES), F32),
            pltpu.VMEM((tt // SCAN_CHUNK, PAIRS, SCAN_CHUNK, LANES), F32),
        ],
        compiler_params=_cparams(("parallel", "arbitrary")),
        name="rwkv7",
    )(p3, p3, p3, p3,
      row(mu[:MIX]), row(mu[MIX:2 * MIX]), row(mu[2 * MIX:3 * MIX]), row(mu[3 * MIX:]),
      wcat, row(w0), row(a0), row(k_k), row(k_a), row(r_k), row(lnx_w), row(lnx_b),
      tril)


def kernel(x, mem, mem_norm_g, norm1_g, w_kv_mem, w_out, norm2_g, rwkv_w_in, rwkv_mu, rwkv_w0, rwkv_w2, rwkv_a0, rwkv_a2, rwkv_g2, rwkv_k_k, rwkv_k_a, rwkv_r_k, rwkv_lnx_w, rwkv_lnx_b, ffn_w_gu, ffn_w_down, gmlp_w_in, gmlp_v_ln_g, gmlp_v_ln_b, gmlp_w_s, gmlp_b_s, moe_router, moe_w_gu, moe_w_down, final_norm_g):
    b, t, d = x.shape
    n_tok = b * t
    depth = norm1_g.shape[0]
    xs = x.reshape(n_tok, d)
    mem2 = mem.reshape(b * N_MEM, d)
    ready = {}

    def bf16(name, w):
        return ready.pop(name) if name in ready else w.astype(BF16)

    for i in range(depth):
        j = i // 2
        kv, _ = _norm_matmul(mem2, mem_norm_g, w_kv_mem[i].astype(BF16), PROJ_TILE)
        kv = kv.reshape(b, N_MEM, 2 * MEMW)
        if i % 2 == 0:
            todo = {("ffn_gu", j): ffn_w_gu[j], ("ffn_down", j): ffn_w_down[j]}
            if i + 1 < depth:
                todo[("gmlp_in", j)] = gmlp_w_in[j]
                todo[("w_out", i + 1)] = w_out[i + 1]
            p, copies = _norm_matmul(xs, norm1_g[i], rwkv_w_in[j].astype(BF16), PROJ_TILE,
                                     tuple(todo.values()))
            ready.update(zip(todo.keys(), copies))
            p3 = p.reshape(b, t, RWKV_COLS + MEMW)
            wcat = jnp.zeros((LORA_W, 3 * MIX), F32)
            wcat = wcat.at[0:64, 0:MIX].set(rwkv_w2[j])
            wcat = wcat.at[64:128, MIX:2 * MIX].set(rwkv_a2[j])
            wcat = wcat.at[128:256, 2 * MIX:].set(rwkv_g2[j])
            y = _rwkv(p3, rwkv_mu[j], wcat.astype(BF16), rwkv_w0[j], rwkv_a0[j], rwkv_k_k[j],
                      rwkv_k_a[j], rwkv_r_k[j].reshape(MIX), rwkv_lnx_w[j], rwkv_lnx_b[j], RWKV_TILE)
            q_block = RWKV_COLS // MEMW
        else:
            p, _ = _norm_matmul(xs, norm1_g[i], bf16(("gmlp_in", j), gmlp_w_in[j]), PROJ_TILE)
            p3 = p.reshape(b, t, 2 * MIX + MEMW)
            bias = jnp.repeat(jnp.transpose(gmlp_b_s[j]), HEAD, axis=1)
            y = _gmlp(p3, gmlp_v_ln_g[j], gmlp_v_ln_b[j], gmlp_w_s[j], bias, GMLP_TILE)
            q_block = 2 * MIX // MEMW
        wo = bf16(("w_out", i), w_out[i])
        xs = _mixer_out(xs.reshape(b, t, d), y, p3, q_block, kv, wo[:MIX], wo[MIX:],
                        MIXER_TILE).reshape(n_tok, d)
        if i % 2 == 0:
            pending = ()
            if j < moe_w_gu.shape[0]:
                pending = (moe_w_gu[j].reshape(-1, 2 * D_FF), moe_w_down[j].reshape(-1, d))
            xs, narrowed = _ffn(xs, norm2_g[i], bf16(("ffn_gu", j), ffn_w_gu[j]),
                                bf16(("ffn_down", j), ffn_w_down[j]), FFN_TILE, pending)
        else:
            last = i == depth - 1
            w_gu, w_down = (narrowed[0].reshape(moe_w_gu.shape[1:]),
                            narrowed[1].reshape(moe_w_down.shape[1:]))
            xs = _moe(xs, norm2_g[i], moe_router[j], w_gu, w_down, final_norm_g if last else None,
                      TOKEN_TILE, ROW_COPY_TILE, EXPERT_TILE)
    if depth % 2 == 1:
        xs = _final_norm(xs, final_norm_g, TOKEN_TILE)
    return xs.reshape(b, t, d)
```
